```python
import math
import jax
import jax.numpy as jnp
from jax import lax
import numpy as np

D_MODEL = 1024
BATCH = 8
SEQ = 8192
DEPTH = 2
DEC_BATCH = 8
DEC_SEQ = 64
PAST_LEN = 2048

CHUNK = 64
QBLK = 128
N_GROUPS = 4
GROUP_W = D_MODEL // N_GROUPS
MIX_W = N_GROUPS * GROUP_W

HA = 4
NOPE = 64
ROPE = 32
VDA = GROUP_W // HA
Q_LORA = 3 * D_MODEL // 16
KV_LORA = D_MODEL // 8
ROPE_THETA = 10000.0
MLA_SCALE = (NOPE + ROPE) ** -0.5

HB = 4
DB = GROUP_W // HB
SB_SCALE = DB ** -0.5

HC = 4
DKC = GROUP_W // HC
DVC = GROUP_W // HC
CONV_W = 4
GDN_CHUNK = CHUNK

S5_CG = 16
S5_G = GROUP_W // S5_CG
S5_P = 64

D_FF = 2816
N_EXP = 8
TOP_K = 2
D_FF_EXP = 1792
MOE_BLK = 512
N_DENSE = (DEPTH + 1) // 2
N_MOE = DEPTH // 2

DN_ALPHA = (2 * DEPTH) ** 0.25
DN_BETA = (8 * DEPTH) ** -0.25
LN_EPS = 1e-5
RMS_EPS = 1e-6
NEG_INF = -1e30

N_MLA_IN = Q_LORA + KV_LORA + ROPE
N_SB_IN = 3 * HB * DB
N_GDN_QKV = HC * (2 * DKC + DVC)
N_GDN_IN = N_GDN_QKV + 2 * HC + HC * DVC
N_S5_IN = GROUP_W
OFF_SB = N_MLA_IN
OFF_GDN = OFF_SB + N_SB_IN
OFF_S5 = OFF_GDN + N_GDN_IN
N_IN = OFF_S5 + N_S5_IN

kernel_name = 'hybrid_streaming_encoder_step'


def layer_norm(x, g, b):
    xf = x.astype(jnp.float32)
    mu = jnp.mean(xf, -1, keepdims=True)
    var = jnp.mean(jnp.square(xf - mu), -1, keepdims=True)
    return ((xf - mu) * lax.rsqrt(var + LN_EPS) * g + b).astype(x.dtype)


def rms_norm(x, g):
    xf = x.astype(jnp.float32)
    return (xf * lax.rsqrt(jnp.mean(xf * xf, -1, keepdims=True) + RMS_EPS) * g).astype(x.dtype)


def l2norm(x):
    return x * lax.rsqrt(jnp.sum(x * x, -1, keepdims=True) + 1e-6)


def rope(x, pos):
    half = ROPE // 2
    inv = ROPE_THETA ** (-jnp.arange(half, dtype=jnp.float32) / half)
    ang = pos.astype(jnp.float32)[:, None, None] * inv
    c, s = jnp.cos(ang), jnp.sin(ang)
    xf = x.astype(jnp.float32)
    x1, x2 = xf[..., :half], xf[..., half:]
    return jnp.concatenate([x1 * c - x2 * s, x1 * s + x2 * c], -1).astype(x.dtype)


def sweep_query_blocks(fn, q, q_pos):
    B, Sq = q.shape[:2]
    if Sq > QBLK and Sq % QBLK == 0:
        nb = Sq // QBLK
        qb = q.reshape((B, nb, QBLK) + q.shape[2:]).swapaxes(0, 1)
        pb = q_pos.reshape(nb, QBLK)
        out = lax.map(lambda a: fn(a[0], a[1]), (qb, pb))
        return out.swapaxes(0, 1).reshape((B, Sq) + out.shape[3:])
    return fn(q, q_pos)


def softmax_chunk_attend(q, k, v, q_pos, k_pos):
    s = jnp.einsum('bqhd,bkhd->bhqk', q, k, preferred_element_type=jnp.float32) * MLA_SCALE
    allowed = (k_pos[None, :] // CHUNK) <= (q_pos[:, None] // CHUNK)
    p = jax.nn.softmax(jnp.where(allowed, s, NEG_INF), axis=-1)
    return jnp.einsum('bhqk,bkhd->bqhd', p.astype(v.dtype), v)


def stick_breaking_attend(q, k, v, q_pos, k_pos):
    z = jnp.einsum('bqhd,bkhd->bhqk', q, k, preferred_element_type=jnp.float32) * SB_SCALE
    allowed = k_pos[None, :] < q_pos[:, None]
    log_beta = jax.nn.log_sigmoid(z)
    log_1m = jnp.where(allowed, jax.nn.log_sigmoid(-z), 0.0)
    after = lax.cumsum(log_1m, axis=3, reverse=True) - log_1m
    w = jnp.where(allowed, jnp.exp(log_beta + after), 0.0)
    return jnp.einsum('bhqk,bkhd->bqhd', w.astype(v.dtype), v)


def mla_mixer(p_in, cache_lat, cache_kr, q_pos, k_pos, q_norm, kv_norm, w_uq, w_ukv):
    B, T = p_in.shape[:2]
    c_q = p_in[..., :Q_LORA]
    c_kv = p_in[..., Q_LORA:Q_LORA + KV_LORA]
    k_r = p_in[..., Q_LORA + KV_LORA:]
    q = (rms_norm(c_q, q_norm) @ w_uq).reshape(B, T, HA, NOPE + ROPE)
    q = jnp.concatenate([q[..., :NOPE], rope(q[..., NOPE:], q_pos)], -1)
    lat_new = rms_norm(c_kv, kv_norm)
    kr_new = rope(k_r[:, :, None, :], q_pos)[:, :, 0, :]
    lat = jnp.concatenate([cache_lat.astype(lat_new.dtype), lat_new], 1)
    k_rot = jnp.concatenate([cache_kr.astype(kr_new.dtype), kr_new], 1)
    Sk = lat.shape[1]
    kv = (lat @ w_ukv).reshape(B, Sk, HA, NOPE + VDA)
    k = jnp.concatenate([kv[..., :NOPE], jnp.broadcast_to(k_rot[:, :, None, :], (B, Sk, HA, ROPE))], -1)
    v = kv[..., NOPE:]
    o = sweep_query_blocks(lambda qb, pb: softmax_chunk_attend(qb, k, v, pb, k_pos), q, q_pos)
    return o.reshape(B, T, GROUP_W), lat_new, kr_new


def sb_mixer(p_in, cache_k, cache_v, q_pos, k_pos):
    B, T = p_in.shape[:2]
    qkv = p_in.reshape(B, T, 3, HB, DB)
    q, k, v = qkv[:, :, 0], qkv[:, :, 1], qkv[:, :, 2]
    k_all = jnp.concatenate([cache_k.astype(k.dtype), k], 1)
    v_all = jnp.concatenate([cache_v.astype(v.dtype), v], 1)
    o = sweep_query_blocks(lambda qb, pb: stick_breaking_attend(qb, k_all, v_all, pb, k_pos), q, q_pos)
    return o.reshape(B, T, GROUP_W), k, v


def gated_delta_chunked(q, k, v, g, beta, S0, L):
    B, T, H, dk = k.shape
    dv = v.shape[-1]
    N = T // L

    def blk(a):
        return a.reshape((B, N, L) + a.shape[2:]).swapaxes(2, 3)

    q, k, v, g, beta = blk(q), blk(k), blk(v), blk(g), blk(beta)
    gc = jnp.cumsum(g, -1)
    tril_incl = jnp.tril(jnp.ones((L, L), bool))
    tril_strict = jnp.tril(jnp.ones((L, L), bool), -1)
    diff = gc[..., :, None] - gc[..., None, :]
    decay = jnp.where(tril_incl, jnp.exp(jnp.where(tril_incl, diff, 0.0)), 0.0)
    kb = k * beta[..., None]
    m = jnp.where(tril_strict, jnp.einsum('bnhid,bnhjd->bnhij', kb, k) * decay, 0.0)
    rhs = jnp.concatenate([v * beta[..., None], kb * jnp.exp(gc)[..., None]], -1)
    sol = lax.linalg.triangular_solve(jnp.eye(L, dtype=m.dtype) + m, rhs, left_side=True,
                                      lower=True, unit_diagonal=True)
    u, w = sol[..., :dv], sol[..., dv:]
    attn = jnp.where(tril_incl, jnp.einsum('bnhid,bnhjd->bnhij', q, k) * decay, 0.0)
    q_dec = q * jnp.exp(gc)[..., None]
    k_dec = k * jnp.exp(gc[..., -1:] - gc)[..., None]
    g_last = jnp.exp(gc[..., -1])

    def step(S, xs):
        u_c, w_c, a_c, qd_c, kd_c, gl_c = xs
        v_new = u_c - jnp.einsum('bhld,bhde->bhle', w_c, S)
        o_c = jnp.einsum('bhld,bhde->bhle', qd_c, S) + jnp.einsum('bhij,bhje->bhie', a_c, v_new)
        S = S * gl_c[..., None, None] + jnp.einsum('bhld,bhle->bhde', kd_c, v_new)
        return S, o_c

    xs = tuple(jnp.moveaxis(a, 1, 0) for a in (u, w, attn, q_dec, k_dec, g_last))
    S, o = lax.scan(step, S0, xs)
    return o.transpose(1, 0, 3, 2, 4).reshape(B, T, H, dv), S


def gdn_mixer(p_in, conv_state, ssm_state, conv_w, a_log, dt_bias, norm_g):
    B, T = p_in.shape[:2]
    o_a = N_GDN_QKV
    o_b = o_a + HC
    o_z = o_b + HC
    qkv_in = p_in[..., :o_a]
    xp = jnp.concatenate([conv_state.astype(qkv_in.dtype), qkv_in], 1)
    conv = sum(xp[:, i:i + T] * conv_w[i] for i in range(CONV_W))
    conv = jax.nn.silu(conv.astype(jnp.float32))
    q = conv[..., :HC * DKC].reshape(B, T, HC, DKC)
    k = conv[..., HC * DKC:2 * HC * DKC].reshape(B, T, HC, DKC)
    v = conv[..., 2 * HC * DKC:].reshape(B, T, HC, DVC)
    q = l2norm(q) * DKC ** -0.5
    k = l2norm(k)
    beta = jax.nn.sigmoid(p_in[..., o_b:o_z].astype(jnp.float32))
    g = -jnp.exp(a_log.astype(jnp.float32)) * jax.nn.softplus(
        p_in[..., o_a:o_b].astype(jnp.float32) + dt_bias.astype(jnp.float32))
    o, S_new = gated_delta_chunked(q, k, v, g, beta, ssm_state.astype(jnp.float32), min(GDN_CHUNK, T))
    z = p_in[..., o_z:].reshape(B, T, HC, DVC).astype(jnp.float32)
    o = rms_norm(o, norm_g) * jax.nn.silu(z)
    return o.reshape(B, T, GROUP_W).astype(p_in.dtype), xp[:, T:], S_new


def s5_mixer(u, h0_re, h0_im, lam_re, lam_im, log_step, b_re, b_im, c_re, c_im, d, w_glu):
    B, T = u.shape[:2]
    f32 = jnp.float32
    uf = u.astype(f32).reshape(B, T, S5_G, S5_CG)
    lam = lax.complex(lam_re.astype(f32), lam_im.astype(f32))
    dt = jnp.exp(log_step.astype(f32))[:, None]
    lam_bar = jnp.exp(lam * dt)
    b_bar = ((lam_bar - 1.0) / lam)[..., None] * lax.complex(b_re.astype(f32), b_im.astype(f32))
    c_mat = lax.complex(c_re.astype(f32), c_im.astype(f32))
    bu = jnp.einsum('gpc,btgc->btgp', b_bar, uf.astype(jnp.complex64))
    h0 = lax.complex(h0_re.astype(f32), h0_im.astype(f32))
    bu = bu.at[:, 0].add(lam_bar * h0)
    a = jnp.broadcast_to(lam_bar, bu.shape)

    def combine(e1, e2):
        a1, b1 = e1
        a2, b2 = e2
        return a1 * a2, a2 * b1 + b2

    _, h = lax.associative_scan(combine, (a, bu), axis=1)
    y = jnp.einsum('gcp,btgp->btgc', c_mat, h).real + d.astype(f32).reshape(S5_G, S5_CG) * uf
    act = jax.nn.gelu(y.reshape(B, T, GROUP_W))
    out = act * jax.nn.sigmoid(act @ w_glu.astype(f32))
    h_last = h[:, -1]
    return out.astype(u.dtype), h_last.real, h_last.imag


def swiglu(x, wg, wu, wd):
    return (jax.nn.silu(x @ wg) * (x @ wu)) @ wd


def moe_swiglu(x, w_router, b_router, wg, wu, wd):
    B, T, D = x.shape
    xt = x.reshape(B * T, D)
    NT = B * T
    logits = jnp.einsum('td,de->te', xt, w_router, preferred_element_type=jnp.float32) + b_router.astype(jnp.float32)
    top_val, top_idx = lax.top_k(logits, TOP_K)
    gates = jax.nn.softmax(top_val, axis=-1)
    e_flat = top_idx.reshape(-1)
    tok_flat = jnp.arange(NT * TOP_K, dtype=jnp.int32) // TOP_K
    g_flat = gates.reshape(-1)
    order = jnp.argsort(e_flat)
    e_s, tok_s, g_s = e_flat[order], tok_flat[order], g_flat[order]
    counts = jnp.bincount(e_flat, length=N_EXP)
    padded = (counts + MOE_BLK - 1) // MOE_BLK * MOE_BLK
    pad_end = jnp.cumsum(padded)
    pad_start = pad_end - padded
    grp_start = jnp.cumsum(counts) - counts
    dest = pad_start[e_s] + jnp.arange(NT * TOP_K, dtype=jnp.int32) - grp_start[e_s]
    n_blk = -(-(NT * TOP_K + N_EXP * (MOE_BLK - 1)) // MOE_BLK)
    rows = n_blk * MOE_BLK
    row_tok = jnp.full((rows,), NT, jnp.int32).at[dest].set(tok_s)
    row_gate = jnp.zeros((rows,), jnp.float32).at[dest].set(g_s)
    blk_exp = jnp.minimum(jnp.searchsorted(pad_end, jnp.arange(n_blk) * MOE_BLK, side='right'), N_EXP - 1)
    xpad = jnp.concatenate([xt, jnp.zeros((1, D), xt.dtype)], 0)
    xb = xpad[row_tok].reshape(n_blk, MOE_BLK, D)

    def expert_block(args):
        xblk, e = args
        return (jax.nn.silu(xblk @ wg[e]) * (xblk @ wu[e])) @ wd[e]

    yb = lax.map(expert_block, (xb, blk_exp)).reshape(rows, D)
    out = jnp.zeros((NT + 1, D), jnp.float32).at[row_tok].add(yb.astype(jnp.float32) * row_gate[:, None])
    return out[:NT].astype(x.dtype).reshape(B, T, D)


def run_trunk(x, states, P):
    lat_c, kr_c, sbk_c, sbv_c, conv_c, ssm_c, re_c, im_c = states
    B, T = x.shape[:2]
    past = lat_c.shape[2]
    q_pos = past + jnp.arange(T, dtype=jnp.int32)
    k_pos = jnp.arange(past + T, dtype=jnp.int32)
    x = layer_norm(x, P['ln_in_g'], P['ln_in_b'])
    outs = [[] for _ in range(8)]
    for l in range(DEPTH):
        proj = x @ P['w_in'][l]
        o_a, lat_n, kr_n = mla_mixer(proj[..., :OFF_SB], lat_c[l], kr_c[l], q_pos, k_pos,
                                     P['mla_q_norm'][l], P['mla_kv_norm'][l], P['mla_w_uq'][l], P['mla_w_ukv'][l])
        o_b, k_n, v_n = sb_mixer(proj[..., OFF_SB:OFF_GDN], sbk_c[l], sbv_c[l], q_pos, k_pos)
        o_c, conv_n, ssm_n = gdn_mixer(proj[..., OFF_GDN:OFF_S5], conv_c[l], ssm_c[l], P['gdn_conv_w'][l],
                                       P['gdn_a_log'][l], P['gdn_dt_bias'][l], P['gdn_norm'][l])
        o_d, re_n, im_n = s5_mixer(proj[..., OFF_S5:], re_c[l], im_c[l], P['s5_lam_re'][l], P['s5_lam_im'][l],
                                   P['s5_log_step'][l], P['s5_b_re'][l], P['s5_b_im'][l], P['s5_c_re'][l],
                                   P['s5_c_im'][l], P['s5_d'][l], P['s5_w_glu'][l])
        gn = P['grp_norm'][l]
        mix = jnp.concatenate([rms_norm(o_a, gn[0]), rms_norm(o_b, gn[1]), o_c, rms_norm(o_d, gn[2])], -1)
        x = layer_norm(DN_ALPHA * x + mix @ P['w_out'][l], P['ln1_g'][l], P['ln1_b'][l])
        i = l // 2
        if l % 2 == 0:
            f = swiglu(x, P['ffn_w_gate'][i], P['ffn_w_up'][i], P['ffn_w_down'][i])
        else:
            f = moe_swiglu(x, P['moe_w_router'][i], P['moe_b_router'][i], P['moe_w_gate'][i],
                           P['moe_w_up'][i], P['moe_w_down'][i])
        x = layer_norm(DN_ALPHA * x + f, P['ln2_g'][l], P['ln2_b'][l])
        for lst, a in zip(outs, (lat_n, kr_n, k_n, v_n, conv_n, ssm_n, re_n, im_n)):
            lst.append(a)
    return x, [jnp.stack(lst) for lst in outs]


def setup_inputs(seed: int = 0) -> dict:
    key = jax.random.key(seed)
    keys = iter(jax.random.split(key, 64))
    f32 = jnp.float32

    def nrm(shape, scale=1.0):
        return jax.random.normal(next(keys), shape, f32) * scale

    def unif(shape, lo, hi):
        return jax.random.uniform(next(keys), shape, f32, lo, hi)

    def gain(shape):
        return 1.0 + nrm(shape, 0.02)

    dt = jnp.exp(unif((DEPTH, HC), math.log(1e-3), math.log(1e-1)))
    lam_im = jnp.pi * jnp.arange(S5_P, dtype=f32) + nrm((DEPTH, S5_G, S5_P), 0.01)
    return {
        'x_prompt': nrm((BATCH, SEQ, D_MODEL)),
        'x_sample': nrm((DEC_BATCH, DEC_SEQ, D_MODEL)),
        'cache_mla_latent': nrm((DEPTH, DEC_BATCH, PAST_LEN, KV_LORA)),
        'cache_mla_krope': nrm((DEPTH, DEC_BATCH, PAST_LEN, ROPE)),
        'cache_sb_k': nrm((DEPTH, DEC_BATCH, PAST_LEN, HB, DB)),
        'cache_sb_v': nrm((DEPTH, DEC_BATCH, PAST_LEN, HB, DB)),
        'state_gdn_conv': nrm((DEPTH, DEC_BATCH, CONV_W - 1, N_GDN_QKV)),
        'state_gdn_ssm': nrm((DEPTH, DEC_BATCH, HC, DKC, DVC), 0.1),
        'state_s5_re': nrm((DEPTH, DEC_BATCH, S5_G, S5_P), 0.1),
        'state_s5_im': nrm((DEPTH, DEC_BATCH, S5_G, S5_P), 0.1),
        'ln_in_g': gain((D_MODEL,)),
        'ln_in_b': nrm((D_MODEL,), 0.02),
        'w_in': nrm((DEPTH, D_MODEL, N_IN), D_MODEL ** -0.5),
        'mla_q_norm': gain((DEPTH, Q_LORA)),
        'mla_kv_norm': gain((DEPTH, KV_LORA)),
        'mla_w_uq': nrm((DEPTH, Q_LORA, HA * (NOPE + ROPE)), Q_LORA ** -0.5),
        'mla_w_ukv': nrm((DEPTH, KV_LORA, HA * (NOPE + VDA)), KV_LORA ** -0.5),
        'gdn_conv_w': nrm((DEPTH, CONV_W, N_GDN_QKV), CONV_W ** -0.5),
        'gdn_a_log': jnp.log(unif((DEPTH, HC), 1.0, 16.0)),
        'gdn_dt_bias': dt + jnp.log(-jnp.expm1(-dt)),
        'gdn_norm': gain((DEPTH, DVC)),
        's5_lam_re': -0.5 + nrm((DEPTH, S5_G, S5_P), 0.01),
        's5_lam_im': lam_im,
        's5_log_step': jnp.log(unif((DEPTH, S5_G), 1e-3, 1e-1)),
        's5_b_re': nrm((DEPTH, S5_G, S5_P, S5_CG), (2 * S5_CG) ** -0.5),
        's5_b_im': nrm((DEPTH, S5_G, S5_P, S5_CG), (2 * S5_CG) ** -0.5),
        's5_c_re': nrm((DEPTH, S5_G, S5_CG, S5_P), (2 * S5_P) ** -0.5),
        's5_c_im': nrm((DEPTH, S5_G, S5_CG, S5_P), (2 * S5_P) ** -0.5),
        's5_d': nrm((DEPTH, GROUP_W)),
        's5_w_glu': nrm((DEPTH, GROUP_W, GROUP_W), GROUP_W ** -0.5),
        'grp_norm': gain((DEPTH, 3, GROUP_W)),
        'w_out': nrm((DEPTH, MIX_W, D_MODEL), MIX_W ** -0.5 * DN_BETA),
        'ln1_g': gain((DEPTH, D_MODEL)),
        'ln1_b': nrm((DEPTH, D_MODEL), 0.02),
        'ln2_g': gain((DEPTH, D_MODEL)),
        'ln2_b': nrm((DEPTH, D_MODEL), 0.02),
        'ffn_w_gate': nrm((N_DENSE, D_MODEL, D_FF), D_MODEL ** -0.5),
        'ffn_w_up': nrm((N_DENSE, D_MODEL, D_FF), D_MODEL ** -0.5),
        'ffn_w_down': nrm((N_DENSE, D_FF, D_MODEL), D_FF ** -0.5 * DN_BETA),
        'moe_w_router': nrm((N_MOE, D_MODEL, N_EXP), D_MODEL ** -0.5),
        'moe_b_router': nrm((N_MOE, N_EXP), 0.01),
        'moe_w_gate': nrm((N_MOE, N_EXP, D_MODEL, D_FF_EXP), D_MODEL ** -0.5),
        'moe_w_up': nrm((N_MOE, N_EXP, D_MODEL, D_FF_EXP), D_MODEL ** -0.5),
        'moe_w_down': nrm((N_MOE, N_EXP, D_FF_EXP, D_MODEL), D_FF_EXP ** -0.5 * DN_BETA),
    }


def reference(x_prompt, x_sample, cache_mla_latent, cache_mla_krope, cache_sb_k, cache_sb_v,
              state_gdn_conv, state_gdn_ssm, state_s5_re, state_s5_im,
              ln_in_g, ln_in_b, w_in, mla_q_norm, mla_kv_norm, mla_w_uq, mla_w_ukv,
              gdn_conv_w, gdn_a_log, gdn_dt_bias, gdn_norm,
              s5_lam_re, s5_lam_im, s5_log_step, s5_b_re, s5_b_im, s5_c_re, s5_c_im, s5_d, s5_w_glu,
              grp_norm, w_out, ln1_g, ln1_b, ln2_g, ln2_b,
              ffn_w_gate, ffn_w_up, ffn_w_down,
              moe_w_router, moe_b_router, moe_w_gate, moe_w_up, moe_w_down):
    P = dict(ln_in_g=ln_in_g, ln_in_b=ln_in_b, w_in=w_in, mla_q_norm=mla_q_norm, mla_kv_norm=mla_kv_norm,
             mla_w_uq=mla_w_uq, mla_w_ukv=mla_w_ukv, gdn_conv_w=gdn_conv_w, gdn_a_log=gdn_a_log,
             gdn_dt_bias=gdn_dt_bias, gdn_norm=gdn_norm, s5_lam_re=s5_lam_re, s5_lam_im=s5_lam_im,
             s5_log_step=s5_log_step, s5_b_re=s5_b_re, s5_b_im=s5_b_im, s5_c_re=s5_c_re, s5_c_im=s5_c_im,
             s5_d=s5_d, s5_w_glu=s5_w_glu, grp_norm=grp_norm, w_out=w_out, ln1_g=ln1_g, ln1_b=ln1_b,
             ln2_g=ln2_g, ln2_b=ln2_b, ffn_w_gate=ffn_w_gate, ffn_w_up=ffn_w_up, ffn_w_down=ffn_w_down,
             moe_w_router=moe_w_router, moe_b_router=moe_b_router, moe_w_gate=moe_w_gate,
             moe_w_up=moe_w_up, moe_w_down=moe_w_down)
    Bp = x_prompt.shape[0]
    dt = x_prompt.dtype
    prompt_states = (jnp.zeros((DEPTH, Bp, 0, KV_LORA), dt), jnp.zeros((DEPTH, Bp, 0, ROPE), dt),
                     jnp.zeros((DEPTH, Bp, 0, HB, DB), dt), jnp.zeros((DEPTH, Bp, 0, HB, DB), dt),
                     jnp.zeros((DEPTH, Bp, CONV_W - 1, N_GDN_QKV), dt),
                     jnp.zeros((DEPTH, Bp, HC, DKC, DVC), jnp.float32),
                     jnp.zeros((DEPTH, Bp, S5_G, S5_P), jnp.float32),
                     jnp.zeros((DEPTH, Bp, S5_G, S5_P), jnp.float32))
    y_prompt, st_p = run_trunk(x_prompt, prompt_states, P)
    sample_states = (cache_mla_latent, cache_mla_krope, cache_sb_k, cache_sb_v,
                     state_gdn_conv, state_gdn_ssm, state_s5_re, state_s5_im)
    y_sample, st_s = run_trunk(x_sample, sample_states, P)
    p_lat, p_kr, p_sbk, p_sbv, p_conv, p_ssm, p_re, p_im = st_p
    s_lat, s_kr, s_sbk, s_sbv, s_conv, s_ssm, s_re, s_im = st_s
    return (y_prompt, y_sample, p_lat, p_kr, p_sbk, p_sbv, p_conv, p_ssm, p_re, p_im,
            s_lat, s_kr, s_sbk, s_sbv, s_conv, s_ssm, s_re, s_im)
```

```python
import functools
import math

import jax
import jax.numpy as jnp
from jax import lax
from jax.experimental import pallas as pl
from jax.experimental.pallas import tpu as pltpu

F32 = jnp.float32
BF16 = jnp.bfloat16
HIGHEST = lax.Precision.HIGHEST

D_MODEL = 1024
CHUNK = 64
GROUP_W = 256
N_HEADS = 4
HEAD_D = 64
NOPE = 64
ROPE = 32
Q_LORA = 192
KV_LORA = 128
ROPE_THETA = 10000.0
MLA_SCALE = (NOPE + ROPE) ** -0.5
SB_SCALE = HEAD_D ** -0.5
CONV_W = 4
N_GDN_QKV = 768
S5_CG = 16
S5_G = 16
S5_P = 64
S5_STATE = S5_G * S5_P
D_FF = 2816
N_EXP = 8
TOP_K = 2
D_FF_EXP = 1792
MOE_BLK = 512
LN_EPS = 1e-5
RMS_EPS = 1e-6
NEG_INF = -1e30
OFF_SB = Q_LORA + KV_LORA + ROPE
OFF_GDN = OFF_SB + 3 * GROUP_W
OFF_S5 = OFF_GDN + N_GDN_QKV + 2 * N_HEADS + GROUP_W

LANES = 128
HEAD_PAD = LANES
VMEM_LIMIT = 48 * 1024 * 1024


def _params(sem):
    return pltpu.CompilerParams(dimension_semantics=sem, vmem_limit_bytes=VMEM_LIMIT)


def _dot(a, b, precision=None):
    return jnp.dot(a, b, preferred_element_type=F32, precision=precision)


def _dot_nt(a, b, precision=None):
    return lax.dot_general(a, b, (((1,), (1,)), ((), ())), preferred_element_type=F32, precision=precision)


def _dot_tn(a, b, precision=None):
    return lax.dot_general(a, b, (((0,), (0,)), ((), ())), preferred_element_type=F32, precision=precision)


def _ln_rows(x, g, b):
    mu = jnp.mean(x, -1, keepdims=True)
    xc = x - mu
    var = jnp.mean(xc * xc, -1, keepdims=True)
    return xc * lax.rsqrt(var + LN_EPS) * g + b


def _rms_rows(x, g, n=None):
    n = x.shape[-1] if n is None else n
    ms = jnp.sum(x * x, -1, keepdims=True) * (1.0 / n)
    return x * lax.rsqrt(ms + RMS_EPS) * g


def _full_spec(shape):
    nd = len(shape)
    return pl.BlockSpec(shape, lambda *_: (0,) * nd)


def _ln_kernel(x_ref, g_ref, b_ref, o_ref):
    o_ref[...] = _ln_rows(x_ref[...], g_ref[...], b_ref[...])


def _layer_norm(x, g, b, tm):
    B, T, D = x.shape
    row = pl.BlockSpec((None, tm, D), lambda bi, i: (bi, i, 0))
    return pl.pallas_call(
        _ln_kernel, grid=(B, T // tm),
        in_specs=[row, _full_spec((1, D)), _full_spec((1, D))],
        out_specs=row, out_shape=jax.ShapeDtypeStruct((B, T, D), F32),
        compiler_params=_params(("parallel", "parallel")), name="ln_in",
    )(x, g.reshape(1, D), b.reshape(1, D))


MLA_IN_PAD = 640
GZ_PAD = 384


def _inproj_kernel(x_ref, wm_ref, wsb_ref, wgq_ref, wgz_ref, ws5_ref,
                   pm_ref, sbq_ref, sbk_ref, sbv_ref, gq_ref, gz_ref, s5_ref):
    xb = x_ref[...].astype(BF16)
    pm_ref[...] = _dot(xb, wm_ref[...])
    sb = _dot(xb, wsb_ref[...])
    sbq_ref[...] = sb[:, :GROUP_W]
    sbk_ref[...] = sb[:, GROUP_W:2 * GROUP_W]
    sbv_ref[...] = sb[:, 2 * GROUP_W:]
    gq_ref[...] = _dot(xb, wgq_ref[...])
    gz_ref[...] = _dot(xb, wgz_ref[...])
    s5_ref[...] = _dot(xb, ws5_ref[...])


def _rotate_half_cols(w):
    half = w.shape[-1] // 2
    return jnp.concatenate([-w[..., half:], w[..., :half]], -1)


def _prep_inproj_weights(w_in):
    zeros = lambda n: jnp.zeros((D_MODEL, n), F32)
    w_cq = w_in[:, :Q_LORA]
    w_ckv = w_in[:, Q_LORA:Q_LORA + KV_LORA]
    w_kr = w_in[:, Q_LORA + KV_LORA:OFF_SB]
    wm = jnp.concatenate([w_ckv, w_cq, zeros(64), w_kr, zeros(96), _rotate_half_cols(w_kr), zeros(96)], 1)
    wsb = w_in[:, OFF_SB:OFF_GDN]
    wgq = w_in[:, OFF_GDN:OFF_GDN + N_GDN_QKV]
    o_a = OFF_GDN + N_GDN_QKV
    wgz = jnp.concatenate([w_in[:, o_a + 2 * N_HEADS:OFF_S5], w_in[:, o_a:o_a + 2 * N_HEADS],
                           zeros(GZ_PAD - GROUP_W - 2 * N_HEADS)], 1)
    ws5 = w_in[:, OFF_S5:]
    return tuple(w.astype(BF16) for w in (wm, wsb, wgq, wgz, ws5))


def _inproj(x, weights, tm):
    B, T, D = x.shape
    wm, wsb, wgq, wgz, ws5 = weights

    def row(f):
        return pl.BlockSpec((None, tm, f), lambda bi, i: (bi, i, 0))

    def out(f):
        return jax.ShapeDtypeStruct((B, T, f), F32)

    return pl.pallas_call(
        _inproj_kernel, grid=(B, T // tm),
        in_specs=[row(D)] + [_full_spec(w.shape) for w in weights],
        out_specs=[row(MLA_IN_PAD), row(GROUP_W), row(GROUP_W), row(GROUP_W), row(N_GDN_QKV), row(GZ_PAD),
                   pl.BlockSpec((tm, GROUP_W), lambda bi, i: (i, bi))],
        out_shape=[out(MLA_IN_PAD), out(GROUP_W), out(GROUP_W), out(GROUP_W), out(N_GDN_QKV), out(GZ_PAD),
                   jax.ShapeDtypeStruct((T, B * GROUP_W), F32)],
        compiler_params=_params(("parallel", "parallel")), name="inproj",
    )(x, *weights)


def _mla_prep_kernel(pm_ref, tab_ref, qn_ref, kvn_ref, wq_ref, wqr_ref, q_ref, lat_ref, kr_ref, krp_ref):
    pm = pm_ref[...]
    tab = tab_ref[...]
    c_kv = pm[:, :KV_LORA]
    c_q = pm[:, KV_LORA:KV_LORA + 2 * LANES]
    k_r = pm[:, 3 * LANES:4 * LANES]
    k_rr = pm[:, 4 * LANES:5 * LANES]
    lat_ref[...] = _rms_rows(c_kv, kvn_ref[...])
    nq = _rms_rows(c_q, qn_ref[...], n=Q_LORA).astype(BF16)
    qp = _dot(nq, wq_ref[...])
    qr = _dot(nq, wqr_ref[...])
    cq, sq = tab[:, :LANES], tab[:, LANES:2 * LANES]
    for h in range(N_HEADS):
        sl = slice(h * HEAD_PAD, (h + 1) * HEAD_PAD)
        q_ref[:, sl] = ((qp[:, sl] * cq + qr[:, sl] * sq) * MLA_SCALE).astype(BF16)
    kr_new = k_r * tab[:, 2 * LANES:3 * LANES] + k_rr * tab[:, 3 * LANES:]
    krp_ref[...] = kr_new
    kr_ref[...] = kr_new[:, :ROPE]


def _rope_table(past, T):
    half = ROPE // 2
    inv = ROPE_THETA ** (-jnp.arange(half, dtype=F32) / half)
    ang = (past + jnp.arange(T, dtype=jnp.int32)).astype(F32)[:, None] * inv
    c, s = jnp.cos(ang), jnp.sin(ang)
    one, zero = jnp.ones((T, NOPE), F32), jnp.zeros((T, NOPE), F32)
    z32, z96 = jnp.zeros((T, 32), F32), jnp.zeros((T, 96), F32)
    return jnp.concatenate([one, c, c, z32, zero, s, s, z32, c, c, z96, s, s, z96], 1)


def _prep_mla_weights(q_norm, kv_norm, w_uq, w_ukv):
    w3 = w_uq.reshape(Q_LORA, N_HEADS, NOPE + ROPE)
    zq = jnp.zeros((Q_LORA, N_HEADS, HEAD_PAD - NOPE - ROPE), F32)
    wq = jnp.concatenate([w3, zq], -1).reshape(Q_LORA, N_HEADS * HEAD_PAD)
    w3r = jnp.concatenate([jnp.zeros((Q_LORA, N_HEADS, NOPE), F32), _rotate_half_cols(w3[..., NOPE:]), zq], -1)
    wqr = w3r.reshape(Q_LORA, N_HEADS * HEAD_PAD)
    pad_rows = jnp.zeros((2 * LANES - Q_LORA, N_HEADS * HEAD_PAD), F32)
    wq = jnp.concatenate([wq, pad_rows], 0).astype(BF16)
    wqr = jnp.concatenate([wqr, pad_rows], 0).astype(BF16)
    qn = jnp.concatenate([q_norm, jnp.zeros((2 * LANES - Q_LORA,), F32)]).reshape(1, 2 * LANES)
    kvn = kv_norm.reshape(1, KV_LORA)
    kv3 = w_ukv.reshape(KV_LORA, N_HEADS, NOPE + HEAD_D)
    zk = jnp.zeros((KV_LORA, N_HEADS, HEAD_PAD - NOPE), F32)
    wk = jnp.concatenate([kv3[..., :NOPE], zk], -1).reshape(KV_LORA, N_HEADS * HEAD_PAD).astype(BF16)
    wv = jnp.concatenate([kv3[..., NOPE:], zk], -1).reshape(KV_LORA, N_HEADS * HEAD_PAD).astype(BF16)
    e = jnp.zeros((LANES, N_HEADS, HEAD_PAD), F32)
    e = e.at[jnp.arange(ROPE)[:, None], jnp.arange(N_HEADS)[None, :], NOPE + jnp.arange(ROPE)[:, None]].set(1.0)
    e = e.reshape(LANES, N_HEADS * HEAD_PAD).astype(BF16)
    return qn, kvn, wq, wqr, wk, wv, e


def _mla_prep(pm, tab, qn, kvn, wq, wqr, tm):
    B, T, _ = pm.shape

    def row(f):
        return pl.BlockSpec((None, tm, f), lambda bi, i: (bi, i, 0))

    return pl.pallas_call(
        _mla_prep_kernel, grid=(B, T // tm),
        in_specs=[row(MLA_IN_PAD), pl.BlockSpec((tm, 4 * LANES), lambda bi, i: (i, 0)),
                  _full_spec(qn.shape), _full_spec(kvn.shape), _full_spec(wq.shape), _full_spec(wqr.shape)],
        out_specs=[row(N_HEADS * HEAD_PAD), row(KV_LORA), row(ROPE), row(LANES)],
        out_shape=[jax.ShapeDtypeStruct((B, T, N_HEADS * HEAD_PAD), BF16),
                   jax.ShapeDtypeStruct((B, T, KV_LORA), F32),
                   jax.ShapeDtypeStruct((B, T, ROPE), F32),
                   jax.ShapeDtypeStruct((B, T, LANES), F32)],
        compiler_params=_params(("parallel", "parallel")), name="mla_prep",
    )(pm, tab, qn, kvn, wq, wqr)


def _mla_kv_kernel(lat_ref, krp_ref, wk_ref, wv_ref, e_ref, k_ref, v_ref):
    lb = lat_ref[...].astype(BF16)
    kb = krp_ref[...].astype(BF16)
    k_ref[...] = (_dot(lb, wk_ref[...]) + _dot(kb, e_ref[...])).astype(BF16)
    v_ref[...] = _dot(lb, wv_ref[...]).astype(BF16)


def _mla_kv(lat, krp, wk, wv, e, tm):
    B, S, _ = lat.shape

    def row(f):
        return pl.BlockSpec((None, tm, f), lambda bi, i: (bi, i, 0))

    return pl.pallas_call(
        _mla_kv_kernel, grid=(B, S // tm),
        in_specs=[row(KV_LORA), row(LANES), _full_spec(wk.shape), _full_spec(wv.shape), _full_spec(e.shape)],
        out_specs=[row(N_HEADS * HEAD_PAD), row(N_HEADS * HEAD_PAD)],
        out_shape=[jax.ShapeDtypeStruct((B, S, N_HEADS * HEAD_PAD), BF16)] * 2,
        compiler_params=_params(("parallel", "parallel")), name="mla_kv",
    )(lat, krp, wk, wv, e)


def _mla_attn_kernel(q_ref, k_ref, v_ref, o_ref, m_scr, l_scr, acc_scr, *, tq, tk, past, nkv):
    qi = pl.program_id(1)
    kj = pl.program_id(2)

    @pl.when(kj == 0)
    def _():
        m_scr[...] = jnp.full(m_scr.shape, NEG_INF, F32)
        l_scr[...] = jnp.zeros(l_scr.shape, F32)
        acc_scr[...] = jnp.zeros(acc_scr.shape, F32)

    @pl.when(kj * tk < past + (qi + 1) * tq)
    def _():
        q_chunk = (past + qi * tq + lax.broadcasted_iota(jnp.int32, (tq, tk), 0)) // CHUNK
        k_chunk = (kj * tk + lax.broadcasted_iota(jnp.int32, (tq, tk), 1)) // CHUNK
        allowed = k_chunk <= q_chunk
        for h in range(N_HEADS):
            sl = slice(h * HEAD_PAD, (h + 1) * HEAD_PAD)
            s = _dot_nt(q_ref[:, sl], k_ref[:, sl])
            s = jnp.where(allowed, s, NEG_INF)
            m_prev = m_scr[h]
            m_new = jnp.maximum(m_prev, jnp.max(s, -1, keepdims=True))
            alpha = jnp.exp(m_prev - m_new)
            p = jnp.exp(s - m_new)
            l_scr[h] = alpha * l_scr[h] + jnp.sum(p, -1, keepdims=True)
            acc_scr[:, sl] = alpha * acc_scr[:, sl] + _dot(p.astype(BF16), v_ref[:, sl])
            m_scr[h] = m_new

    @pl.when(kj == nkv - 1)
    def _():
        for h in range(N_HEADS):
            o_ref[:, h * HEAD_D:(h + 1) * HEAD_D] = acc_scr[:, h * HEAD_PAD:h * HEAD_PAD + HEAD_D] / l_scr[h]


def _mla_attn(q, k, v, past, tq, tk):
    B, T, W = q.shape
    S = k.shape[1]
    nq, nkv = T // tq, S // tk

    def kv_map(bi, i, j):
        last = (past + (i + 1) * tq - 1) // tk
        return (bi, jnp.minimum(j, last), 0)

    kern = functools.partial(_mla_attn_kernel, tq=tq, tk=tk, past=past, nkv=nkv)
    return pl.pallas_call(
        kern, grid=(B, nq, nkv),
        in_specs=[pl.BlockSpec((None, tq, W), lambda bi, i, j: (bi, i, 0)),
                  pl.BlockSpec((None, tk, W), kv_map), pl.BlockSpec((None, tk, W), kv_map)],
        out_specs=pl.BlockSpec((None, tq, GROUP_W), lambda bi, i, j: (bi, i, 0)),
        out_shape=jax.ShapeDtypeStruct((B, T, GROUP_W), F32),
        scratch_shapes=[pltpu.VMEM((N_HEADS, tq, 1), F32), pltpu.VMEM((N_HEADS, tq, 1), F32),
                        pltpu.VMEM((tq, W), F32)],
        compiler_params=_params(("parallel", "parallel", "arbitrary")), name="mla_attn",
    )(q, k, v)


def _sb_attn_kernel(q_ref, k_ref, v_ref, u_ref, o_ref, c_scr, acc_scr, *, tq, tk, sub, past, nkv):
    qi = pl.program_id(1)
    j = pl.program_id(2)
    last = (past + (qi + 1) * tq - 2) // tk
    jb = last - j

    @pl.when(j == 0)
    def _():
        c_scr[...] = jnp.zeros(c_scr.shape, F32)
        acc_scr[...] = jnp.zeros(acc_scr.shape, F32)

    @pl.when(jb >= 0)
    def _():
        q_pos = past + qi * tq + lax.broadcasted_iota(jnp.int32, (tq, sub), 0)
        k_iota = lax.broadcasted_iota(jnp.int32, (tq, sub), 1)
        u = u_ref[...]
        for h in range(N_HEADS):
            sl = slice(h * HEAD_PAD, (h + 1) * HEAD_PAD)
            qh = q_ref[:, sl]
            carry = c_scr[h]
            acc = acc_scr[:, sl]
            for c in reversed(range(tk // sub)):
                rows = slice(c * sub, (c + 1) * sub)
                z = _dot_nt(qh, k_ref[rows, sl])
                allowed = (jb * tk + c * sub + k_iota) < q_pos
                sp = jnp.log1p(jnp.exp(-jnp.abs(z)))
                log_beta = jnp.minimum(z, 0.0) - sp
                log_1m = jnp.where(allowed, -jnp.maximum(z, 0.0) - sp, 0.0)
                hi = log_1m.astype(BF16)
                lo = (log_1m - hi.astype(F32)).astype(BF16)
                after = carry + _dot(hi, u) + _dot(lo, u)
                w = jnp.where(allowed, jnp.exp(log_beta + after), 0.0)
                acc = acc + _dot(w.astype(BF16), v_ref[rows, sl])
                carry = carry + jnp.sum(log_1m, -1, keepdims=True)
            c_scr[h] = carry
            acc_scr[:, sl] = acc

    @pl.when(j == nkv - 1)
    def _():
        for h in range(N_HEADS):
            o_ref[:, h * HEAD_D:(h + 1) * HEAD_D] = acc_scr[:, h * HEAD_PAD:h * HEAD_PAD + HEAD_D]


def _sb_attn(q, k, v, past, tq, tk, sub):
    B, T, W = q.shape
    S = k.shape[1]
    nq, nkv = T // tq, S // tk
    u = (lax.broadcasted_iota(jnp.int32, (sub, sub), 0) > lax.broadcasted_iota(jnp.int32, (sub, sub), 1)).astype(BF16)

    def kv_map(bi, i, j):
        last = (past + (i + 1) * tq - 2) // tk
        return (bi, jnp.maximum(last - j, 0), 0)

    kern = functools.partial(_sb_attn_kernel, tq=tq, tk=tk, sub=sub, past=past, nkv=nkv)
    return pl.pallas_call(
        kern, grid=(B, nq, nkv),
        in_specs=[pl.BlockSpec((None, tq, W), lambda bi, i, j: (bi, i, 0)),
                  pl.BlockSpec((None, tk, W), kv_map), pl.BlockSpec((None, tk, W), kv_map),
                  _full_spec((sub, sub))],
        out_specs=pl.BlockSpec((None, tq, GROUP_W), lambda bi, i, j: (bi, i, 0)),
        out_shape=jax.ShapeDtypeStruct((B, T, GROUP_W), F32),
        scratch_shapes=[pltpu.VMEM((N_HEADS, tq, 1), F32), pltpu.VMEM((tq, W), F32)],
        compiler_params=_params(("parallel", "parallel", "arbitrary")), name="sb_attn",
    )(q, k, v, u)


def _pad_heads(x, scale=None):
    B, S, _ = x.shape
    x = x.reshape(B, S, N_HEADS, HEAD_D)
    if scale is not None:
        x = x * scale
    x = jnp.pad(x.astype(BF16), ((0, 0), (0, 0), (0, 0), (0, HEAD_PAD - HEAD_D)))
    return x.reshape(B, S, N_HEADS * HEAD_PAD)


def _softplus(x):
    return jnp.maximum(x, 0.0) + jnp.log1p(jnp.exp(-jnp.abs(x)))


def _gdn_kernel(qkv_ref, gz_ref, conv0_ref, s0_ref, cw_ref, alog_ref, dtb_ref, ng_ref, hsum_ref,
                o_ref, s_out_ref, xp_scr, s_scr, o_scr, *, L, nt):
    ti = pl.program_id(1)
    halo = CONV_W - 1

    @pl.when(ti == 0)
    def _():
        xp_scr[8 - halo:8, :] = conv0_ref[...]
        s_scr[...] = s0_ref[...]

    x = qkv_ref[...]
    xp_scr[8:8 + L, :] = x
    conv = jnp.zeros((L, N_GDN_QKV), F32)
    for i in range(CONV_W):
        conv = conv + xp_scr[8 - halo + i:8 - halo + i + L, :] * cw_ref[i:i + 1, :]
    xp_scr[8 - halo:8, :] = x[L - halo:, :]
    conv = jax.nn.silu(conv)
    qa = conv[:, :GROUP_W]
    ka = conv[:, GROUP_W:2 * GROUP_W]
    va = conv[:, 2 * GROUP_W:]
    hsum = hsum_ref[...]
    qa = qa * lax.rsqrt(_dot(qa * qa, hsum, HIGHEST) + 1e-6) * (HEAD_D ** -0.5)
    ka = ka * lax.rsqrt(_dot(ka * ka, hsum, HIGHEST) + 1e-6)

    gz = gz_ref[...]
    ab = gz[:, GROUP_W:]
    g_all = -jnp.exp(alog_ref[...]) * _softplus(ab + dtb_ref[...])
    beta_all = jax.nn.sigmoid(ab)

    ii = lax.broadcasted_iota(jnp.int32, (L, L), 0)
    jj = lax.broadcasted_iota(jnp.int32, (L, L), 1)
    incl = ii >= jj
    strict = ii > jj
    eye = (ii == jj).astype(F32)

    for h in range(N_HEADS):
        hs = slice(h * HEAD_D, (h + 1) * HEAD_D)
        q, k, v = qa[:, hs], ka[:, hs], va[:, hs]
        g_col = g_all[:, h:h + 1]
        beta = beta_all[:, N_HEADS + h:N_HEADS + h + 1]
        g_b = jnp.broadcast_to(g_col, (L, L))
        gc_row = jnp.sum(jnp.where(ii <= jj, g_b, 0.0), 0, keepdims=True)
        gc_col = jnp.sum(jnp.where(ii == jj, jnp.broadcast_to(gc_row, (L, L)), 0.0), 1, keepdims=True)
        decay = jnp.where(incl, jnp.exp(jnp.where(incl, gc_col - gc_row, 0.0)), 0.0)
        e_col = jnp.exp(gc_col)
        gc_last = gc_row[:, L - 1:L]
        kb = k * beta
        kbb, kbf = kb.astype(BF16), k.astype(BF16)
        m = jnp.where(strict, _dot_nt(kbb, kbf) * decay, 0.0)
        p = -m
        t = eye + p
        for _ in range(int(math.log2(L)) - 1):
            p = _dot(p, p, HIGHEST)
            t = t + _dot(t, p, HIGHEST)
        u = _dot(t, v * beta, HIGHEST)
        w = _dot(t, kb * e_col, HIGHEST)
        attn = jnp.where(incl, _dot_nt(q.astype(BF16), kbf) * decay, 0.0)
        q_dec = q * e_col
        k_dec = k * jnp.exp(gc_last - gc_col)
        S = s_scr[h]
        Sb = S.astype(BF16)
        v_new = u - _dot(w.astype(BF16), Sb)
        o_h = _dot(q_dec.astype(BF16), Sb) + _dot(attn.astype(BF16), v_new.astype(BF16))
        s_scr[h] = S * jnp.exp(gc_last) + _dot_tn(k_dec.astype(BF16), v_new.astype(BF16))
        o_scr[:, hs] = o_h

    o = o_scr[...]
    ms = _dot(o * o, hsum, HIGHEST) * (1.0 / HEAD_D)
    o_ref[...] = o * lax.rsqrt(ms + RMS_EPS) * ng_ref[...] * jax.nn.silu(gz[:, :GROUP_W])

    @pl.when(ti == nt - 1)
    def _():
        s_out_ref[...] = s_scr[...]


def _gdn(qkv, gz, conv0, s0, conv_w, a_log, dt_bias, norm_g):
    B, T, _ = qkv.shape
    L = min(CHUNK, T)
    nt = T // L
    pad = lambda a: jnp.concatenate([a, jnp.zeros((LANES - a.shape[0],), F32)]).reshape(1, LANES)
    alog = pad(a_log)
    dtb = pad(dt_bias)
    ng = jnp.tile(norm_g, N_HEADS).reshape(1, GROUP_W)
    hid = jnp.arange(GROUP_W) // HEAD_D
    hsum = (hid[:, None] == hid[None, :]).astype(F32)

    def row(f):
        return pl.BlockSpec((None, L, f), lambda bi, i: (bi, i, 0))

    kern = functools.partial(_gdn_kernel, L=L, nt=nt)
    return pl.pallas_call(
        kern, grid=(B, nt),
        in_specs=[row(N_GDN_QKV), row(GZ_PAD),
                  pl.BlockSpec((None, CONV_W - 1, N_GDN_QKV), lambda bi, i: (bi, 0, 0)),
                  pl.BlockSpec((None, N_HEADS, HEAD_D, HEAD_D), lambda bi, i: (bi, 0, 0, 0)),
                  _full_spec(conv_w.shape), _full_spec(alog.shape), _full_spec(dtb.shape),
                  _full_spec(ng.shape), _full_spec(hsum.shape)],
        out_specs=[row(GROUP_W), pl.BlockSpec((None, N_HEADS, HEAD_D, HEAD_D), lambda bi, i: (bi, 0, 0, 0))],
        out_shape=[jax.ShapeDtypeStruct((B, T, GROUP_W), F32),
                   jax.ShapeDtypeStruct((B, N_HEADS, HEAD_D, HEAD_D), F32)],
        scratch_shapes=[pltpu.VMEM((8 + L, N_GDN_QKV), F32), pltpu.VMEM((N_HEADS, HEAD_D, HEAD_D), F32),
                        pltpu.VMEM((L, GROUP_W), F32)],
        compiler_params=_params(("parallel", "arbitrary")), name="gdn",
    )(qkv, gz, conv0, s0, conv_w, alog, dtb, ng, hsum)


def _s5_kernel(u_ref, h0r_ref, h0i_ref, lr_ref, li_ref, br_ref, bi_ref, cr_ref, ci_ref, d_ref, wg_ref,
               o_ref, hr_out_ref, hi_out_ref, hr_scr, hi_scr, xr_scr, xi_scr, *, tt, nb, nt):
    ti = pl.program_id(0)

    @pl.when(ti == 0)
    def _():
        hr_scr[...] = h0r_ref[...]
        hi_scr[...] = h0i_ref[...]

    u = u_ref[...]
    ub = u.astype(BF16)
    xr_scr[...] = _dot(ub, br_ref[...])
    xi_scr[...] = _dot(ub, bi_ref[...])
    lr = jnp.broadcast_to(lr_ref[...], (nb, S5_STATE))
    li = jnp.broadcast_to(li_ref[...], (nb, S5_STATE))

    def step(t, carry):
        hr, hi = carry
        r0 = pl.multiple_of(t * nb, nb)
        nr = lr * hr - li * hi + xr_scr[pl.ds(r0, nb), :]
        ni = lr * hi + li * hr + xi_scr[pl.ds(r0, nb), :]
        xr_scr[pl.ds(r0, nb), :] = nr
        xi_scr[pl.ds(r0, nb), :] = ni
        return nr, ni

    hr, hi = lax.fori_loop(0, tt, step, (hr_scr[...], hi_scr[...]), unroll=8)
    hr_scr[...] = hr
    hi_scr[...] = hi
    y = _dot(xr_scr[...].astype(BF16), cr_ref[...]) - _dot(xi_scr[...].astype(BF16), ci_ref[...]) + d_ref[...] * u
    act = jax.nn.gelu(y)
    o_ref[...] = act * jax.nn.sigmoid(_dot(act.astype(BF16), wg_ref[...]))

    @pl.when(ti == nt - 1)
    def _():
        hr_out_ref[...] = hr
        hi_out_ref[...] = hi


def _prep_s5_weights(lam_re, lam_im, log_step, b_re, b_im, c_re, c_im, d, w_glu):
    dt = jnp.exp(log_step)[:, None]
    mag = jnp.exp(lam_re * dt)
    bar_re, bar_im = mag * jnp.cos(lam_im * dt), mag * jnp.sin(lam_im * dt)
    den = lam_re * lam_re + lam_im * lam_im
    f_re = ((bar_re - 1.0) * lam_re + bar_im * lam_im) / den
    f_im = (bar_im * lam_re - (bar_re - 1.0) * lam_im) / den
    bb_re = f_re[..., None] * b_re - f_im[..., None] * b_im
    bb_im = f_re[..., None] * b_im + f_im[..., None] * b_re
    eye = jnp.eye(S5_G, dtype=F32)

    def in_mat(b):
        return jnp.einsum('gpc,gh->gchp', b, eye).reshape(S5_G * S5_CG, S5_STATE).astype(BF16)

    def out_mat(c):
        return jnp.einsum('gcp,gh->gphc', c, eye).reshape(S5_STATE, S5_G * S5_CG).astype(BF16)

    return (bar_re.reshape(1, S5_STATE), bar_im.reshape(1, S5_STATE),
            in_mat(bb_re), in_mat(bb_im), out_mat(c_re), out_mat(c_im),
            d.reshape(1, GROUP_W), w_glu.astype(BF16))


def _s5(u_tm, h0_re, h0_im, weights, B, tt):
    T = u_tm.shape[0]
    u2 = u_tm.reshape(T * B, GROUP_W)
    nt = T // tt
    lr, li, br, bi, cr, ci, d, wg = weights
    rows = pl.BlockSpec((tt * B, GROUP_W), lambda i: (i, 0))
    st = _full_spec((B, S5_STATE))
    kern = functools.partial(_s5_kernel, tt=tt, nb=B, nt=nt)
    o, hr, hi = pl.pallas_call(
        kern, grid=(nt,),
        in_specs=[rows, st, st] + [_full_spec(w.shape) for w in weights],
        out_specs=[rows, st, st],
        out_shape=[jax.ShapeDtypeStruct((T * B, GROUP_W), F32), jax.ShapeDtypeStruct((B, S5_STATE), F32),
                   jax.ShapeDtypeStruct((B, S5_STATE), F32)],
        scratch_shapes=[pltpu.VMEM((B, S5_STATE), F32), pltpu.VMEM((B, S5_STATE), F32),
                        pltpu.VMEM((tt * B, S5_STATE), F32), pltpu.VMEM((tt * B, S5_STATE), F32)],
        compiler_params=_params(("arbitrary",)), name="s5",
    )(u2, h0_re.reshape(B, S5_STATE), h0_im.reshape(B, S5_STATE), *weights)
    return o.reshape(T, B * GROUP_W), hr.reshape(B, S5_G, S5_P), hi.reshape(B, S5_G, S5_P)


def _outproj_kernel(oa_ref, ob_ref, oc_ref, od_ref, x_ref, gn_ref, w_ref, g_ref, b_ref, o_ref, *, alpha):
    gn = gn_ref[...]
    mix = jnp.concatenate([
        _rms_rows(oa_ref[...], gn[0:1]).astype(BF16),
        _rms_rows(ob_ref[...], gn[1:2]).astype(BF16),
        oc_ref[...].astype(BF16),
        _rms_rows(od_ref[...], gn[2:3]).astype(BF16)], -1)
    y = alpha * x_ref[...] + _dot(mix, w_ref[...])
    o_ref[...] = _ln_rows(y, g_ref[...], b_ref[...])


def _outproj(oa, ob, oc, od_tm, x, gn, w_out, g, b, alpha, tm):
    B, T, D = x.shape

    def row(f):
        return pl.BlockSpec((None, tm, f), lambda bi, i: (bi, i, 0))

    kern = functools.partial(_outproj_kernel, alpha=alpha)
    return pl.pallas_call(
        kern, grid=(B, T // tm),
        in_specs=[row(GROUP_W), row(GROUP_W), row(GROUP_W), pl.BlockSpec((tm, GROUP_W), lambda bi, i: (i, bi)),
                  row(D), _full_spec(gn.shape), _full_spec(w_out.shape), _full_spec((1, D)), _full_spec((1, D))],
        out_specs=row(D), out_shape=jax.ShapeDtypeStruct((B, T, D), F32),
        compiler_params=_params(("parallel", "parallel")), name="outproj",
    )(oa, ob, oc, od_tm, x, gn, w_out, g.reshape(1, D), b.reshape(1, D))


def _ffn_kernel(x_ref, wg_ref, wu_ref, wd_ref, g_ref, b_ref, o_ref, *, alpha):
    x = x_ref[...]
    xb = x.astype(BF16)
    hid = (jax.nn.silu(_dot(xb, wg_ref[...])) * _dot(xb, wu_ref[...])).astype(BF16)
    y = alpha * x + _dot(hid, wd_ref[...])
    o_ref[...] = _ln_rows(y, g_ref[...], b_ref[...])


def _ffn(x, wg, wu, wd, g, b, alpha, tm):
    B, T, D = x.shape
    row = pl.BlockSpec((None, tm, D), lambda bi, i: (bi, i, 0))

    def resident(shape):
        return pl.BlockSpec(shape, lambda *_: (0,) * len(shape), pipeline_mode=pl.Buffered(1))

    kern = functools.partial(_ffn_kernel, alpha=alpha)
    return pl.pallas_call(
        kern, grid=(B, T // tm),
        in_specs=[row, resident(wg.shape), resident(wu.shape), resident(wd.shape),
                  _full_spec((1, D)), _full_spec((1, D))],
        out_specs=row, out_shape=jax.ShapeDtypeStruct((B, T, D), F32),
        compiler_params=_params(("parallel", "parallel")), name="ffn",
    )(x, wg, wu, wd, g.reshape(1, D), b.reshape(1, D))


def _router_kernel(x_ref, wr_ref, br_ref, idx_ref, gate_ref):
    logits = _dot_nt(wr_ref[...], x_ref[...], HIGHEST) + br_ref[...]
    eid = lax.broadcasted_iota(jnp.int32, logits.shape, 0)
    m1 = jnp.max(logits, 0, keepdims=True)
    i1 = jnp.min(jnp.where(logits == m1, eid, N_EXP), 0, keepdims=True)
    rest = jnp.where(eid == i1, -jnp.inf, logits)
    m2 = jnp.max(rest, 0, keepdims=True)
    i2 = jnp.min(jnp.where(rest == m2, eid, N_EXP), 0, keepdims=True)
    e2 = jnp.exp(m2 - m1)
    den = 1.0 + e2
    idx_ref[...] = jnp.concatenate([i1, i2], 0)
    gate_ref[...] = jnp.concatenate([1.0 / den, e2 / den], 0)


def _router(x2, w_router, b_router, tm):
    N, D = x2.shape
    return pl.pallas_call(
        _router_kernel, grid=(N // tm,),
        in_specs=[pl.BlockSpec((tm, D), lambda i: (i, 0)), _full_spec((N_EXP, D)), _full_spec((N_EXP, 1))],
        out_specs=[pl.BlockSpec((TOP_K, tm), lambda i: (0, i)), pl.BlockSpec((TOP_K, tm), lambda i: (0, i))],
        out_shape=[jax.ShapeDtypeStruct((TOP_K, N), jnp.int32), jax.ShapeDtypeStruct((TOP_K, N), F32)],
        compiler_params=_params(("parallel",)), name="moe_router",
    )(x2, w_router.T, b_router.reshape(N_EXP, 1))


def _expert_kernel(blk_exp_ref, src_ref, dst_ref, gate_ref, x_hbm, wg_ref, wu_ref, wd_ref, out_hbm,
                   xbuf, ybuf, gsem, ssem, *, n_slots):
    del blk_exp_ref

    def gather_copy(r):
        return pltpu.make_async_copy(x_hbm.at[pl.ds(src_ref[0, 0, r], 1)], xbuf.at[pl.ds(r, 1)], gsem)

    def scatter_copy(r):
        return pltpu.make_async_copy(ybuf.at[pl.ds(r, 1)], out_hbm.at[pl.ds(dst_ref[0, 0, r], 1)], ssem)

    def for_rows(fn):
        def body(r, c):
            fn(r)
            return c
        lax.fori_loop(0, MOE_BLK, body, 0, unroll=8)

    for_rows(lambda r: gather_copy(r).start())
    for_rows(lambda r: gather_copy(r).wait())
    xb = xbuf[...].astype(BF16)
    hid = (jax.nn.silu(_dot(xb, wg_ref[...])) * _dot(xb, wu_ref[...])).astype(BF16)
    ybuf[...] = _dot(hid, wd_ref[...]) * gate_ref[...]

    def when_real(r, fn):
        @pl.when(dst_ref[0, 0, r] < n_slots)
        def _():
            fn(r)

    for_rows(lambda r: when_real(r, lambda q: scatter_copy(q).start()))
    for_rows(lambda r: when_real(r, lambda q: scatter_copy(q).wait()))


def _combine_kernel(x_ref, y_ref, g_ref, b_ref, o_ref, *, alpha):
    D = x_ref.shape[-1]
    f = y_ref[:, :D] + y_ref[:, D:]
    o_ref[...] = _ln_rows(alpha * x_ref[...] + f, g_ref[...], b_ref[...])


def _moe(x, w_router, b_router, wg, wu, wd, g, b, alpha):
    B, T, D = x.shape
    N = B * T
    x2 = x.reshape(N, D)
    tm = math.gcd(N, MOE_BLK)
    idx, gates = _router(x2, w_router, b_router, tm)

    n_slots = N * TOP_K
    e_flat = idx.T.reshape(-1)
    g_flat = gates.T.reshape(-1)
    onehot = (e_flat[:, None] == jnp.arange(N_EXP, dtype=jnp.int32)[None, :]).astype(jnp.int32)
    csum = jnp.cumsum(onehot, 0)
    counts = csum[-1]
    rank = jnp.sum((csum - onehot) * onehot, 1)
    padded = (counts + MOE_BLK - 1) // MOE_BLK * MOE_BLK
    pad_end = jnp.cumsum(padded)
    pad_start = pad_end - padded
    dest = pad_start[e_flat] + rank
    n_blk = -(-(n_slots + N_EXP * (MOE_BLK - 1)) // MOE_BLK)
    rows = n_blk * MOE_BLK
    row_slot = jnp.full((rows,), n_slots, jnp.int32).at[dest].set(jnp.arange(n_slots, dtype=jnp.int32))
    row_gate = jnp.zeros((rows,), F32).at[dest].set(g_flat)
    row_src = jnp.minimum(row_slot // TOP_K, N - 1)
    blk_start = jnp.arange(n_blk, dtype=jnp.int32) * MOE_BLK
    blk_exp = jnp.minimum(jnp.sum((pad_end[None, :] <= blk_start[:, None]).astype(jnp.int32), 1), N_EXP - 1)

    idx_spec = pl.BlockSpec((1, 1, MOE_BLK), lambda i, be: (i, 0, 0), memory_space=pltpu.SMEM)
    kern = functools.partial(_expert_kernel, n_slots=n_slots)
    y = pl.pallas_call(
        kern,
        grid_spec=pltpu.PrefetchScalarGridSpec(
            num_scalar_prefetch=1, grid=(n_blk,),
            in_specs=[idx_spec, idx_spec,
                      pl.BlockSpec((MOE_BLK, 1), lambda i, be: (i, 0)),
                      pl.BlockSpec(memory_space=pl.ANY),
                      pl.BlockSpec((None, D, D_FF_EXP), lambda i, be: (be[i], 0, 0)),
                      pl.BlockSpec((None, D, D_FF_EXP), lambda i, be: (be[i], 0, 0)),
                      pl.BlockSpec((None, D_FF_EXP, D), lambda i, be: (be[i], 0, 0))],
            out_specs=pl.BlockSpec(memory_space=pl.ANY),
            scratch_shapes=[pltpu.VMEM((MOE_BLK, D), F32), pltpu.VMEM((MOE_BLK, D), F32),
                            pltpu.SemaphoreType.DMA(()), pltpu.SemaphoreType.DMA(())]),
        out_shape=jax.ShapeDtypeStruct((n_slots, D), F32),
        compiler_params=_params(("arbitrary",)), name="moe_experts",
    )(blk_exp, row_src.reshape(n_blk, 1, MOE_BLK), row_slot.reshape(n_blk, 1, MOE_BLK),
      row_gate.reshape(rows, 1), x2, wg, wu, wd)

    y2 = y.reshape(N, TOP_K * D)
    kern = functools.partial(_combine_kernel, alpha=alpha)
    out = pl.pallas_call(
        kern, grid=(N // tm,),
        in_specs=[pl.BlockSpec((tm, D), lambda i: (i, 0)), pl.BlockSpec((tm, TOP_K * D), lambda i: (i, 0)),
                  _full_spec((1, D)), _full_spec((1, D))],
        out_specs=pl.BlockSpec((tm, D), lambda i: (i, 0)), out_shape=jax.ShapeDtypeStruct((N, D), F32),
        compiler_params=_params(("parallel",)), name="moe_combine",
    )(x2, y2, g.reshape(1, D), b.reshape(1, D))
    return out.reshape(B, T, D)


def _round_up(n, m):
    return -(-n // m) * m


def _run_trunk(x, states, P, depth):
    lat_c, kr_c, sbk_c, sbv_c, conv_c, ssm_c, re_c, im_c = states
    B, T, D = x.shape
    past = lat_c.shape[2]
    alpha = (2 * depth) ** 0.25
    long_seq = T >= 512
    tm = 512 if long_seq else T
    tq = 512 if long_seq else T
    tk = 512 if long_seq else 128
    sub = 256 if long_seq else 128
    S = past + T
    S_pad = _round_up(S, tk)

    tab = _rope_table(past, T)
    x = _layer_norm(x, P['ln_in_g'], P['ln_in_b'], tm)
    outs = [[] for _ in range(8)]
    for l in range(depth):
        pm, sbq, sbk, sbv, gq, gz, s5u = _inproj(x, _prep_inproj_weights(P['w_in'][l]), tm)

        qn, kvn, wq, wqr, wk, wv, e = _prep_mla_weights(P['mla_q_norm'][l], P['mla_kv_norm'][l],
                                                        P['mla_w_uq'][l], P['mla_w_ukv'][l])
        q_a, lat_n, kr_n, krp_n = _mla_prep(pm, tab, qn, kvn, wq, wqr, tm)
        kr_cache = jnp.pad(kr_c[l], ((0, 0), (0, 0), (0, LANES - ROPE)))
        lat_all = jnp.pad(jnp.concatenate([lat_c[l], lat_n], 1), ((0, 0), (0, S_pad - S), (0, 0)))
        krp_all = jnp.pad(jnp.concatenate([kr_cache, krp_n], 1), ((0, 0), (0, S_pad - S), (0, 0)))
        k_a, v_a = _mla_kv(lat_all, krp_all, wk, wv, e, tk)
        o_a = _mla_attn(q_a, k_a, v_a, past, tq, tk)

        k_all = jnp.pad(jnp.concatenate([sbk_c[l].reshape(B, past, GROUP_W), sbk], 1), ((0, 0), (0, S_pad - S), (0, 0)))
        v_all = jnp.pad(jnp.concatenate([sbv_c[l].reshape(B, past, GROUP_W), sbv], 1), ((0, 0), (0, S_pad - S), (0, 0)))
        o_b = _sb_attn(_pad_heads(sbq, SB_SCALE), _pad_heads(k_all), _pad_heads(v_all), past, tq, tk, sub)

        o_c, ssm_n = _gdn(gq, gz, conv_c[l], ssm_c[l], P['gdn_conv_w'][l], P['gdn_a_log'][l],
                          P['gdn_dt_bias'][l], P['gdn_norm'][l])
        conv_n = jnp.concatenate([conv_c[l], gq], 1)[:, T:]

        s5w = _prep_s5_weights(P['s5_lam_re'][l], P['s5_lam_im'][l], P['s5_log_step'][l], P['s5_b_re'][l],
                               P['s5_b_im'][l], P['s5_c_re'][l], P['s5_c_im'][l], P['s5_d'][l], P['s5_w_glu'][l])
        o_d, re_n, im_n = _s5(s5u, re_c[l], im_c[l], s5w, B, min(64, T))

        x = _outproj(o_a, o_b, o_c, o_d, x, P['grp_norm'][l], P['w_out'][l].astype(BF16),
                     P['ln1_g'][l], P['ln1_b'][l], alpha, tm)
        i = l // 2
        if l % 2 == 0:
            x = _ffn(x, P['ffn_w_gate'][i].astype(BF16), P['ffn_w_up'][i].astype(BF16),
                     P['ffn_w_down'][i].astype(BF16), P['ln2_g'][l], P['ln2_b'][l], alpha, min(tm, 256))
        else:
            x = _moe(x, P['moe_w_router'][i], P['moe_b_router'][i], P['moe_w_gate'][i].astype(BF16),
                     P['moe_w_up'][i].astype(BF16), P['moe_w_down'][i].astype(BF16),
                     P['ln2_g'][l], P['ln2_b'][l], alpha)
        news = (lat_n, kr_n, sbk.reshape(B, T, N_HEADS, HEAD_D), sbv.reshape(B, T, N_HEADS, HEAD_D),
                conv_n, ssm_n, re_n, im_n)
        for lst, a in zip(outs, news):
            lst.append(a)
    return x, [jnp.stack(lst) for lst in outs]


def kernel(x_prompt, x_sample, cache_mla_latent, cache_mla_krope, cache_sb_k, cache_sb_v,
           state_gdn_conv, state_gdn_ssm, state_s5_re, state_s5_im,
           ln_in_g, ln_in_b, w_in, mla_q_norm, mla_kv_norm, mla_w_uq, mla_w_ukv,
           gdn_conv_w, gdn_a_log, gdn_dt_bias, gdn_norm,
           s5_lam_re, s5_lam_im, s5_log_step, s5_b_re, s5_b_im, s5_c_re, s5_c_im, s5_d, s5_w_glu,
           grp_norm, w_out, ln1_g, ln1_b, ln2_g, ln2_b,
           ffn_w_gate, ffn_w_up, ffn_w_down,
           moe_w_router, moe_b_router, moe_w_gate, moe_w_up, moe_w_down):
    P = dict(ln_in_g=ln_in_g, ln_in_b=ln_in_b, w_in=w_in, mla_q_norm=mla_q_norm, mla_kv_norm=mla_kv_norm,
             mla_w_uq=mla_w_uq, mla_w_ukv=mla_w_ukv, gdn_conv_w=gdn_conv_w, gdn_a_log=gdn_a_log,
             gdn_dt_bias=gdn_dt_bias, gdn_norm=gdn_norm, s5_lam_re=s5_lam_re, s5_lam_im=s5_lam_im,
             s5_log_step=s5_log_step, s5_b_re=s5_b_re, s5_b_im=s5_b_im, s5_c_re=s5_c_re, s5_c_im=s5_c_im,
             s5_d=s5_d, s5_w_glu=s5_w_glu, grp_norm=grp_norm, w_out=w_out, ln1_g=ln1_g, ln1_b=ln1_b,
             ln2_g=ln2_g, ln2_b=ln2_b, ffn_w_gate=ffn_w_gate, ffn_w_up=ffn_w_up, ffn_w_down=ffn_w_down,
             moe_w_router=moe_w_router, moe_b_router=moe_b_router, moe_w_gate=moe_w_gate,
             moe_w_up=moe_w_up, moe_w_down=moe_w_down)
    depth = w_in.shape[0]
    Bp = x_prompt.shape[0]
    dt = x_prompt.dtype
    prompt_states = (jnp.zeros((depth, Bp, 0, KV_LORA), dt), jnp.zeros((depth, Bp, 0, ROPE), dt),
                     jnp.zeros((depth, Bp, 0, N_HEADS, HEAD_D), dt), jnp.zeros((depth, Bp, 0, N_HEADS, HEAD_D), dt),
                     jnp.zeros((depth, Bp, CONV_W - 1, N_GDN_QKV), dt),
                     jnp.zeros((depth, Bp, N_HEADS, HEAD_D, HEAD_D), F32),
                     jnp.zeros((depth, Bp, S5_G, S5_P), F32), jnp.zeros((depth, Bp, S5_G, S5_P), F32))
    y_prompt, st_p = _run_trunk(x_prompt, prompt_states, P, depth)
    sample_states = (cache_mla_latent, cache_mla_krope, cache_sb_k, cache_sb_v,
                     state_gdn_conv, state_gdn_ssm, state_s5_re, state_s5_im)
    y_sample, st_s = _run_trunk(x_sample, sample_states, P, depth)
    return (y_prompt, y_sample, *st_p, *st_s)
```

```python
import functools
import math

import jax
import jax.numpy as jnp
from jax import lax
from jax.experimental import pallas as pl
from jax.experimental.pallas import tpu as pltpu

F32 = jnp.float32
BF16 = jnp.bfloat16
HIGHEST = lax.Precision.HIGHEST

D_MODEL = 1024
CHUNK = 64
GROUP_W = 256
N_HEADS = 4
HEAD_D = 64
NOPE = 64
ROPE = 32
Q_LORA = 192
KV_LORA = 128
ROPE_THETA = 10000.0
MLA_SCALE = (NOPE + ROPE) ** -0.5
SB_SCALE = HEAD_D ** -0.5
CONV_W = 4
N_GDN_QKV = 768
S5_CG = 16
S5_G = 16
S5_P = 64
S5_STATE = S5_G * S5_P
D_FF = 2816
N_EXP = 8
TOP_K = 2
D_FF_EXP = 1792
MOE_BLK = 512
LN_EPS = 1e-5
RMS_EPS = 1e-6
NEG_INF = -1e30
OFF_SB = Q_LORA + KV_LORA + ROPE
OFF_GDN = OFF_SB + 3 * GROUP_W
OFF_S5 = OFF_GDN + N_GDN_QKV + 2 * N_HEADS + GROUP_W

LANES = 128
HEAD_PAD = LANES
VMEM_LIMIT = 48 * 1024 * 1024
GDN_BATCH_PER_STEP = 4
MOE_DMA_UNROLL = 16
ATTN_TQ = 512
ATTN_TK = 512
ATTN_RQ = 128
ATTN_CK = 256


def _params(sem):
    return pltpu.CompilerParams(dimension_semantics=sem, vmem_limit_bytes=VMEM_LIMIT)


def _dot(a, b, precision=None):
    return jnp.dot(a, b, preferred_element_type=F32, precision=precision)


def _dot_nt(a, b, precision=None):
    return lax.dot_general(a, b, (((1,), (1,)), ((), ())), preferred_element_type=F32, precision=precision)


def _dot_tn(a, b, precision=None):
    return lax.dot_general(a, b, (((0,), (0,)), ((), ())), preferred_element_type=F32, precision=precision)


def _ln_rows(x, g, b):
    mu = jnp.mean(x, -1, keepdims=True)
    xc = x - mu
    var = jnp.mean(xc * xc, -1, keepdims=True)
    return xc * lax.rsqrt(var + LN_EPS) * g + b


def _rms_rows(x, g, n=None):
    n = x.shape[-1] if n is None else n
    ms = jnp.sum(x * x, -1, keepdims=True) * (1.0 / n)
    return x * lax.rsqrt(ms + RMS_EPS) * g


def _full_spec(shape):
    nd = len(shape)
    return pl.BlockSpec(shape, lambda *_: (0,) * nd)


def _ln_kernel(x_ref, g_ref, b_ref, o_ref):
    o_ref[...] = _ln_rows(x_ref[...], g_ref[...], b_ref[...])


def _layer_norm(x, g, b, tm):
    B, T, D = x.shape
    row = pl.BlockSpec((None, tm, D), lambda bi, i: (bi, i, 0))
    return pl.pallas_call(
        _ln_kernel, grid=(B, T // tm),
        in_specs=[row, _full_spec((1, D)), _full_spec((1, D))],
        out_specs=row, out_shape=jax.ShapeDtypeStruct((B, T, D), F32),
        compiler_params=_params(("parallel", "parallel")), name="ln_in",
    )(x, g.reshape(1, D), b.reshape(1, D))


MLA_IN_PAD = 640
GZ_PAD = 384


def _inproj_kernel(x_ref, wm_ref, wsb_ref, wgq_ref, wgz_ref, ws5_ref,
                   pm_ref, sbq_ref, sbk_ref, sbv_ref, sbkb_ref, sbvb_ref, gq_ref, gz_ref, s5_ref):
    xb = x_ref[...].astype(BF16)
    pm_ref[...] = _dot(xb, wm_ref[...])
    sb = _dot(xb, wsb_ref[...])
    sbq_ref[...] = (sb[:, :GROUP_W] * SB_SCALE).astype(BF16)
    sbk_ref[...] = sb[:, GROUP_W:2 * GROUP_W]
    sbv_ref[...] = sb[:, 2 * GROUP_W:]
    sbkb_ref[...] = sb[:, GROUP_W:2 * GROUP_W].astype(BF16)
    sbvb_ref[...] = sb[:, 2 * GROUP_W:].astype(BF16)
    gq_ref[...] = _dot(xb, wgq_ref[...])
    gz_ref[...] = _dot(xb, wgz_ref[...])
    s5_ref[...] = _dot(xb, ws5_ref[...])


def _rotate_half_cols(w):
    half = w.shape[-1] // 2
    return jnp.concatenate([-w[..., half:], w[..., :half]], -1)


def _prep_inproj_weights(w_in):
    zeros = lambda n: jnp.zeros((D_MODEL, n), F32)
    w_cq = w_in[:, :Q_LORA]
    w_ckv = w_in[:, Q_LORA:Q_LORA + KV_LORA]
    w_kr = w_in[:, Q_LORA + KV_LORA:OFF_SB]
    wm = jnp.concatenate([w_ckv, w_cq, zeros(64), w_kr, zeros(96), _rotate_half_cols(w_kr), zeros(96)], 1)
    wsb = w_in[:, OFF_SB:OFF_GDN]
    wgq = w_in[:, OFF_GDN:OFF_GDN + N_GDN_QKV]
    o_a = OFF_GDN + N_GDN_QKV
    wgz = jnp.concatenate([w_in[:, o_a + 2 * N_HEADS:OFF_S5], w_in[:, o_a:o_a + 2 * N_HEADS],
                           zeros(GZ_PAD - GROUP_W - 2 * N_HEADS)], 1)
    ws5 = w_in[:, OFF_S5:]
    return tuple(w.astype(BF16) for w in (wm, wsb, wgq, wgz, ws5))


def _inproj(x, weights, tm):
    B, T, D = x.shape
    wm, wsb, wgq, wgz, ws5 = weights

    def row(f):
        return pl.BlockSpec((None, tm, f), lambda bi, i: (bi, i, 0))

    def out(f, dt=F32):
        return jax.ShapeDtypeStruct((B, T, f), dt)

    return pl.pallas_call(
        _inproj_kernel, grid=(B, T // tm),
        in_specs=[row(D)] + [_full_spec(w.shape) for w in weights],
        out_specs=[row(MLA_IN_PAD), row(GROUP_W), row(GROUP_W), row(GROUP_W), row(GROUP_W), row(GROUP_W),
                   row(N_GDN_QKV), row(GZ_PAD), pl.BlockSpec((tm, GROUP_W), lambda bi, i: (i, bi))],
        out_shape=[out(MLA_IN_PAD), out(GROUP_W, BF16), out(GROUP_W), out(GROUP_W), out(GROUP_W, BF16),
                   out(GROUP_W, BF16), out(N_GDN_QKV), out(GZ_PAD),
                   jax.ShapeDtypeStruct((T, B * GROUP_W), F32)],
        compiler_params=_params(("parallel", "parallel")), name="inproj",
    )(x, *weights)


def _mla_prep_kernel(pm_ref, tab_ref, qn_ref, kvn_ref, wq_ref, wqr_ref, q_ref, lat_ref, kr_ref, krp_ref):
    pm = pm_ref[...]
    tab = tab_ref[...]
    c_kv = pm[:, :KV_LORA]
    c_q = pm[:, KV_LORA:KV_LORA + 2 * LANES]
    k_r = pm[:, 3 * LANES:4 * LANES]
    k_rr = pm[:, 4 * LANES:5 * LANES]
    lat_ref[...] = _rms_rows(c_kv, kvn_ref[...])
    nq = _rms_rows(c_q, qn_ref[...], n=Q_LORA).astype(BF16)
    qp = _dot(nq, wq_ref[...])
    qr = _dot(nq, wqr_ref[...])
    cq, sq = tab[:, :LANES], tab[:, LANES:2 * LANES]
    for h in range(N_HEADS):
        sl = slice(h * HEAD_PAD, (h + 1) * HEAD_PAD)
        q_ref[:, sl] = ((qp[:, sl] * cq + qr[:, sl] * sq) * MLA_SCALE).astype(BF16)
    kr_new = k_r * tab[:, 2 * LANES:3 * LANES] + k_rr * tab[:, 3 * LANES:]
    krp_ref[...] = kr_new
    kr_ref[...] = kr_new[:, :ROPE]


def _rope_table(past, T):
    half = ROPE // 2
    inv = ROPE_THETA ** (-jnp.arange(half, dtype=F32) / half)
    ang = (past + jnp.arange(T, dtype=jnp.int32)).astype(F32)[:, None] * inv
    c, s = jnp.cos(ang), jnp.sin(ang)
    one, zero = jnp.ones((T, NOPE), F32), jnp.zeros((T, NOPE), F32)
    z32, z96 = jnp.zeros((T, 32), F32), jnp.zeros((T, 96), F32)
    return jnp.concatenate([one, c, c, z32, zero, s, s, z32, c, c, z96, s, s, z96], 1)


def _prep_mla_weights(q_norm, kv_norm, w_uq, w_ukv):
    w3 = w_uq.reshape(Q_LORA, N_HEADS, NOPE + ROPE)
    zq = jnp.zeros((Q_LORA, N_HEADS, HEAD_PAD - NOPE - ROPE), F32)
    wq = jnp.concatenate([w3, zq], -1).reshape(Q_LORA, N_HEADS * HEAD_PAD)
    w3r = jnp.concatenate([jnp.zeros((Q_LORA, N_HEADS, NOPE), F32), _rotate_half_cols(w3[..., NOPE:]), zq], -1)
    wqr = w3r.reshape(Q_LORA, N_HEADS * HEAD_PAD)
    pad_rows = jnp.zeros((2 * LANES - Q_LORA, N_HEADS * HEAD_PAD), F32)
    wq = jnp.concatenate([wq, pad_rows], 0).astype(BF16)
    wqr = jnp.concatenate([wqr, pad_rows], 0).astype(BF16)
    qn = jnp.concatenate([q_norm, jnp.zeros((2 * LANES - Q_LORA,), F32)]).reshape(1, 2 * LANES)
    kvn = kv_norm.reshape(1, KV_LORA)
    kv3 = w_ukv.reshape(KV_LORA, N_HEADS, NOPE + HEAD_D)
    zk = jnp.zeros((KV_LORA, N_HEADS, HEAD_PAD - NOPE), F32)
    wk = jnp.concatenate([kv3[..., :NOPE], zk], -1).reshape(KV_LORA, N_HEADS * HEAD_PAD).astype(BF16)
    wv = kv3[..., NOPE:].reshape(KV_LORA, GROUP_W).astype(BF16)
    e = jnp.zeros((LANES, N_HEADS, HEAD_PAD), F32)
    e = e.at[jnp.arange(ROPE)[:, None], jnp.arange(N_HEADS)[None, :], NOPE + jnp.arange(ROPE)[:, None]].set(1.0)
    e = e.reshape(LANES, N_HEADS * HEAD_PAD).astype(BF16)
    return qn, kvn, wq, wqr, wk, wv, e


def _mla_prep(pm, tab, qn, kvn, wq, wqr, tm):
    B, T, _ = pm.shape

    def row(f):
        return pl.BlockSpec((None, tm, f), lambda bi, i: (bi, i, 0))

    return pl.pallas_call(
        _mla_prep_kernel, grid=(B, T // tm),
        in_specs=[row(MLA_IN_PAD), pl.BlockSpec((tm, 4 * LANES), lambda bi, i: (i, 0)),
                  _full_spec(qn.shape), _full_spec(kvn.shape), _full_spec(wq.shape), _full_spec(wqr.shape)],
        out_specs=[row(N_HEADS * HEAD_PAD), row(KV_LORA), row(ROPE), row(LANES)],
        out_shape=[jax.ShapeDtypeStruct((B, T, N_HEADS * HEAD_PAD), BF16),
                   jax.ShapeDtypeStruct((B, T, KV_LORA), F32),
                   jax.ShapeDtypeStruct((B, T, ROPE), F32),
                   jax.ShapeDtypeStruct((B, T, LANES), F32)],
        compiler_params=_params(("parallel", "parallel")), name="mla_prep",
    )(pm, tab, qn, kvn, wq, wqr)


def _mla_kv_kernel(lat_ref, krp_ref, wk_ref, wv_ref, e_ref, k_ref, v_ref):
    lb = lat_ref[...].astype(BF16)
    kb = krp_ref[...].astype(BF16)
    k_ref[...] = (_dot(lb, wk_ref[...]) + _dot(kb, e_ref[...])).astype(BF16)
    v_ref[...] = _dot(lb, wv_ref[...]).astype(BF16)


def _mla_kv(lat, krp, wk, wv, e, tm):
    B, S, _ = lat.shape

    def row(f):
        return pl.BlockSpec((None, tm, f), lambda bi, i: (bi, i, 0))

    return pl.pallas_call(
        _mla_kv_kernel, grid=(B, S // tm),
        in_specs=[row(KV_LORA), row(LANES), _full_spec(wk.shape), _full_spec(wv.shape), _full_spec(e.shape)],
        out_specs=[row(N_HEADS * HEAD_PAD), row(GROUP_W)],
        out_shape=[jax.ShapeDtypeStruct((B, S, N_HEADS * HEAD_PAD), BF16),
                   jax.ShapeDtypeStruct((B, S, GROUP_W), BF16)],
        compiler_params=_params(("parallel", "parallel")), name="mla_kv",
    )(lat, krp, wk, wv, e)


def _lane_tile(a, n):
    return a if n == 1 else jnp.concatenate([a] * n, axis=1)


def _per_head_lanes(stats):
    low = lax.broadcasted_iota(jnp.int32, stats[0].shape, 1) < HEAD_D
    return jnp.concatenate([jnp.where(low, stats[0], stats[1]), jnp.where(low, stats[2], stats[3])], axis=1)


def _block_diag_values(vc):
    head = lax.broadcasted_iota(jnp.int32, vc.shape, 1) // HEAD_D
    return jnp.concatenate([jnp.where(head == h, vc, jnp.zeros_like(vc)) for h in range(N_HEADS)], axis=0)


def _mla_attn_kernel(q_ref, k_ref, v_ref, o_ref, m_scr, l_scr, acc_scr, *, tq, tk, rq, ck, past, nkv):
    qi = pl.program_id(1)
    kj = pl.program_id(2)
    delta = past + qi * tq - kj * tk

    @pl.when(kj == 0)
    def _():
        m_scr[...] = jnp.full(m_scr.shape, NEG_INF, F32)
        l_scr[...] = jnp.zeros(l_scr.shape, F32)
        acc_scr[...] = jnp.zeros(acc_scr.shape, F32)

    def tiles(masked):
        v_bd = [_block_diag_values(v_ref[c * ck:(c + 1) * ck, :]) for c in range(tk // ck)]
        for r in range(tq // rq):
            rows = slice(r * rq, (r + 1) * rq)
            m = [m_scr[h, rows, :] for h in range(N_HEADS)]
            l = [l_scr[h, rows, :] for h in range(N_HEADS)]
            acc = acc_scr[rows, :]
            for c in range(tk // ck):
                keys = slice(c * ck, (c + 1) * ck)
                if masked:
                    q_chunk = (past + qi * tq + r * rq + lax.broadcasted_iota(jnp.int32, (rq, ck), 0)) // CHUNK
                    k_chunk = (kj * tk + c * ck + lax.broadcasted_iota(jnp.int32, (rq, ck), 1)) // CHUNK
                    allowed = k_chunk <= q_chunk
                ps, alphas = [], []
                for h in range(N_HEADS):
                    sl = slice(h * HEAD_PAD, (h + 1) * HEAD_PAD)
                    s = _dot_nt(q_ref[rows, sl], k_ref[keys, sl])
                    if masked:
                        s = jnp.where(allowed, s, NEG_INF)
                    m_new = jnp.maximum(m[h], jnp.max(s, -1, keepdims=True))
                    alpha = jnp.exp(m[h] - m_new)
                    p = jnp.exp(s - _lane_tile(m_new, ck // LANES))
                    l[h] = alpha * l[h] + jnp.sum(p, -1, keepdims=True)
                    m[h] = m_new
                    ps.append(p.astype(BF16))
                    alphas.append(alpha)
                acc = _per_head_lanes(alphas) * acc + _dot(jnp.concatenate(ps, axis=1), v_bd[c])
            for h in range(N_HEADS):
                m_scr[h, rows, :] = m[h]
                l_scr[h, rows, :] = l[h]
            acc_scr[rows, :] = acc

    @pl.when(delta >= tk)
    def _():
        tiles(False)

    @pl.when((delta < tk) & (delta + tq > 0))
    def _():
        tiles(True)

    @pl.when(kj == nkv - 1)
    def _():
        o_ref[...] = acc_scr[...] / _per_head_lanes([l_scr[h] for h in range(N_HEADS)])


def _mla_attn(q, k, v, past, tq, tk):
    B, T, W = q.shape
    S = k.shape[1]
    nq, nkv = T // tq, S // tk
    rq, ck = min(ATTN_RQ, tq), min(ATTN_CK, tk)

    def kv_map(bi, i, j):
        last = (past + (i + 1) * tq - 1) // tk
        return (bi, jnp.minimum(j, last), 0)

    kern = functools.partial(_mla_attn_kernel, tq=tq, tk=tk, rq=rq, ck=ck, past=past, nkv=nkv)
    return pl.pallas_call(
        kern, grid=(B, nq, nkv),
        in_specs=[pl.BlockSpec((None, tq, W), lambda bi, i, j: (bi, i, 0)),
                  pl.BlockSpec((None, tk, W), kv_map), pl.BlockSpec((None, tk, GROUP_W), kv_map)],
        out_specs=pl.BlockSpec((None, tq, GROUP_W), lambda bi, i, j: (bi, i, 0)),
        out_shape=jax.ShapeDtypeStruct((B, T, GROUP_W), F32),
        scratch_shapes=[pltpu.VMEM((N_HEADS, tq, LANES), F32), pltpu.VMEM((N_HEADS, tq, LANES), F32),
                        pltpu.VMEM((tq, GROUP_W), F32)],
        compiler_params=_params(("parallel", "parallel", "arbitrary")), name="mla_attn",
    )(q, k, v)


def _sb_attn_kernel(q_ref, k_ref, v_ref, u_ref, o_ref, c_scr, acc_scr, *, tq, tk, rq, ck, past, nkv):
    qi = pl.program_id(1)
    j = pl.program_id(2)
    last = (past + (qi + 1) * tq - 2) // tk
    jb = last - j
    delta = past + qi * tq - jb * tk

    @pl.when(j == 0)
    def _():
        c_scr[...] = jnp.zeros(c_scr.shape, F32)
        acc_scr[...] = jnp.zeros(acc_scr.shape, F32)

    def tiles(masked):
        u2 = u_ref[...]
        v_bd = [_block_diag_values(v_ref[c * ck:(c + 1) * ck, :]) for c in range(tk // ck)]
        head = lax.broadcasted_iota(jnp.int32, (rq, GROUP_W), 1) // HEAD_D
        for r in range(tq // rq):
            rows = slice(r * rq, (r + 1) * rq)
            q_r = q_ref[rows, :]
            q_h = [jnp.where(head == h, q_r, jnp.zeros_like(q_r)) for h in range(N_HEADS)]
            carry = [c_scr[h, rows, :] for h in range(N_HEADS)]
            acc = acc_scr[rows, :]
            for c in reversed(range(tk // ck)):
                k_c = k_ref[c * ck:(c + 1) * ck, :]
                if masked:
                    q_pos = delta + r * rq + lax.broadcasted_iota(jnp.int32, (rq, ck), 0)
                    allowed = (c * ck + lax.broadcasted_iota(jnp.int32, (rq, ck), 1)) < q_pos
                ws = []
                for h in range(N_HEADS):
                    z = _dot_nt(q_h[h], k_c)
                    sp = jnp.log(1.0 + jnp.exp(-jnp.abs(z)))
                    log_beta = jnp.minimum(z, 0.0) - sp
                    log_1m = log_beta - z
                    if masked:
                        log_1m = jnp.where(allowed, log_1m, 0.0)
                    hi = log_1m.astype(BF16)
                    lo = (log_1m - hi.astype(F32)).astype(BF16)
                    after = _dot(jnp.concatenate([hi, lo], axis=1), u2) + _lane_tile(carry[h], ck // LANES)
                    w = jnp.exp(log_beta + after)
                    if masked:
                        w = jnp.where(allowed, w, 0.0)
                    ws.append(w.astype(BF16))
                    carry[h] = carry[h] + jnp.sum(log_1m, -1, keepdims=True)
                acc = acc + _dot(jnp.concatenate(ws, axis=1), v_bd[c])
            for h in range(N_HEADS):
                c_scr[h, rows, :] = carry[h]
            acc_scr[rows, :] = acc

    @pl.when((delta >= tk) & (jb >= 0))
    def _():
        tiles(False)

    @pl.when((delta < tk) & (jb >= 0))
    def _():
        tiles(True)

    @pl.when(j == nkv - 1)
    def _():
        o_ref[...] = acc_scr[...]


def _sb_attn(q, k, v, past, tq, tk):
    B, T, W = q.shape
    S = k.shape[1]
    nq, nkv = T // tq, S // tk
    rq, ck = min(ATTN_RQ, tq), min(ATTN_CK, tk)
    later = lax.broadcasted_iota(jnp.int32, (ck, ck), 0) > lax.broadcasted_iota(jnp.int32, (ck, ck), 1)
    u2 = jnp.concatenate([later, later], 0).astype(BF16)

    def kv_map(bi, i, j):
        last = (past + (i + 1) * tq - 2) // tk
        return (bi, jnp.maximum(last - j, 0), 0)

    kern = functools.partial(_sb_attn_kernel, tq=tq, tk=tk, rq=rq, ck=ck, past=past, nkv=nkv)
    return pl.pallas_call(
        kern, grid=(B, nq, nkv),
        in_specs=[pl.BlockSpec((None, tq, W), lambda bi, i, j: (bi, i, 0)),
                  pl.BlockSpec((None, tk, W), kv_map), pl.BlockSpec((None, tk, W), kv_map),
                  _full_spec((2 * ck, ck))],
        out_specs=pl.BlockSpec((None, tq, GROUP_W), lambda bi, i, j: (bi, i, 0)),
        out_shape=jax.ShapeDtypeStruct((B, T, GROUP_W), F32),
        scratch_shapes=[pltpu.VMEM((N_HEADS, tq, LANES), F32), pltpu.VMEM((tq, GROUP_W), F32)],
        compiler_params=_params(("parallel", "parallel", "arbitrary")), name="sb_attn",
    )(q, k, v, u2)


def _softplus(x):
    return jnp.maximum(x, 0.0) + jnp.log1p(jnp.exp(-jnp.abs(x)))


def _split_bf16(a):
    hi = a.astype(BF16)
    return hi, (a - hi.astype(F32)).astype(BF16)


def _dot_split(a, b):
    a_hi, a_lo = _split_bf16(a)
    b_hi, b_lo = _split_bf16(b)
    m = a.shape[0]
    top = _dot(jnp.concatenate([a_hi, a_lo], 0), b_hi)
    return top[:m] + top[m:] + _dot(a_hi, b_lo)


def _head_sumsq(x, hsum):
    hi, lo = _split_bf16(x * x)
    m = x.shape[0]
    both = _dot(jnp.concatenate([hi, lo], 0), hsum)
    return both[:m] + both[m:]


def _gdn_kernel(qkv_ref, gz_ref, conv0_ref, s0_ref, cw_ref, alog_ref, dtb_ref, ng_ref, hsum_ref,
                o_ref, s_out_ref, xp_scr, s_scr, o_scr, *, L, nt, bb):
    ti = pl.program_id(1)
    halo = CONV_W - 1

    @pl.when(ti == 0)
    def _():
        xp_scr[:, 8 - halo:8, :] = conv0_ref[...]
        s_scr[...] = s0_ref[...]

    hsum = hsum_ref[...]
    ii = lax.broadcasted_iota(jnp.int32, (L, L), 0)
    jj = lax.broadcasted_iota(jnp.int32, (L, L), 1)
    incl = ii >= jj
    strict = ii > jj
    eye = (ii == jj).astype(F32)

    units = []
    for b in range(bb):
        x = qkv_ref[b]
        xp_scr[b, 8:8 + L, :] = x
        conv = jnp.zeros((L, N_GDN_QKV), F32)
        for i in range(CONV_W):
            conv = conv + xp_scr[b, 8 - halo + i:8 - halo + i + L, :] * cw_ref[i:i + 1, :]
        xp_scr[b, 8 - halo:8, :] = x[L - halo:, :]
        conv = jax.nn.silu(conv)
        qa = conv[:, :GROUP_W]
        ka = conv[:, GROUP_W:2 * GROUP_W]
        va = conv[:, 2 * GROUP_W:]
        qa = qa * lax.rsqrt(_head_sumsq(qa, hsum) + 1e-6) * (HEAD_D ** -0.5)
        ka = ka * lax.rsqrt(_head_sumsq(ka, hsum) + 1e-6)
        ab = gz_ref[b, :, GROUP_W:]
        g_all = -jnp.exp(alog_ref[...]) * _softplus(ab + dtb_ref[...])
        beta_all = jax.nn.sigmoid(ab)
        for h in range(N_HEADS):
            hs = slice(h * HEAD_D, (h + 1) * HEAD_D)
            units.append(dict(b=b, h=h, q=qa[:, hs], k=ka[:, hs], v=va[:, hs], g=g_all[:, h:h + 1],
                              beta=beta_all[:, N_HEADS + h:N_HEADS + h + 1]))

    for un in units:
        g_b = jnp.broadcast_to(un['g'], (L, L))
        gc_row = jnp.sum(jnp.where(ii <= jj, g_b, 0.0), 0, keepdims=True)
        gc_col = jnp.sum(jnp.where(ii == jj, jnp.broadcast_to(gc_row, (L, L)), 0.0), 1, keepdims=True)
        un['decay'] = jnp.where(incl, jnp.exp(jnp.where(incl, gc_col - gc_row, 0.0)), 0.0)
        un['e_col'] = jnp.exp(gc_col)
        gc_last = gc_row[:, L - 1:L]
        un['g_last'] = jnp.exp(gc_last)
        un['k_tail'] = jnp.exp(gc_last - gc_col)
        un['kb'] = un['k'] * un['beta']
        un['kbf'] = un['k'].astype(BF16)
    for un in units:
        m = jnp.where(strict, _dot_nt(un['kb'].astype(BF16), un['kbf']) * un['decay'], 0.0)
        un['p'] = -m
        un['t'] = eye - m

    n_sq = int(math.log2(L)) - 1
    for r in range(n_sq + 1):
        for un in units:
            p_hi, p_lo = _split_bf16(un['p'])
            if r == 0:
                lhs_hi, lhs_lo = p_hi, p_lo
            else:
                t_hi, t_lo = _split_bf16(un['t'])
                lhs_hi = jnp.concatenate([p_hi, t_hi], 0) if r < n_sq else t_hi
                lhs_lo = jnp.concatenate([p_lo, t_lo], 0) if r < n_sq else t_lo
            rows = lhs_hi.shape[0]
            top = _dot(jnp.concatenate([lhs_hi, lhs_lo], 0), p_hi)
            prod = top[:rows] + top[rows:] + _dot(lhs_hi, p_lo)
            if r == 0:
                un['p'] = prod
            elif r < n_sq:
                un['p'] = prod[:L]
                un['t'] = un['t'] + prod[L:]
            else:
                un['t'] = un['t'] + prod
    for un in units:
        un['u'] = _dot_split(un['t'], un['v'] * un['beta'])
        un['w'] = _dot_split(un['t'], un['kb'] * un['e_col'])
        un['attn'] = jnp.where(incl, _dot_nt(un['q'].astype(BF16), un['kbf']) * un['decay'], 0.0).astype(BF16)

    for un in units:
        un['S'] = s_scr[un['b'], un['h']]
        un['Sb'] = un['S'].astype(BF16)
    for un in units:
        un['v_new'] = (un['u'] - _dot(un['w'].astype(BF16), un['Sb'])).astype(BF16)
    for un in units:
        hs = slice(un['h'] * HEAD_D, (un['h'] + 1) * HEAD_D)
        o_h = _dot((un['q'] * un['e_col']).astype(BF16), un['Sb']) + _dot(un['attn'], un['v_new'])
        o_scr[un['b'], :, hs] = o_h
        k_dec = (un['k'] * un['k_tail']).astype(BF16)
        s_scr[un['b'], un['h']] = un['S'] * un['g_last'] + _dot_tn(k_dec, un['v_new'])

    for b in range(bb):
        o = o_scr[b]
        ms = _head_sumsq(o, hsum) * (1.0 / HEAD_D)
        o_ref[b] = o * lax.rsqrt(ms + RMS_EPS) * ng_ref[...] * jax.nn.silu(gz_ref[b, :, :GROUP_W])

    @pl.when(ti == nt - 1)
    def _():
        s_out_ref[...] = s_scr[...]


def _gdn(qkv, gz, conv0, s0, conv_w, a_log, dt_bias, norm_g):
    B, T, _ = qkv.shape
    L = min(CHUNK, T)
    nt = T // L
    pad = lambda a: jnp.concatenate([a, jnp.zeros((LANES - a.shape[0],), F32)]).reshape(1, LANES)
    alog = pad(a_log)
    dtb = pad(dt_bias)
    ng = jnp.tile(norm_g, N_HEADS).reshape(1, GROUP_W)
    hid = jnp.arange(GROUP_W) // HEAD_D
    hsum = (hid[:, None] == hid[None, :]).astype(BF16)
    bb = GDN_BATCH_PER_STEP

    def row(f):
        return pl.BlockSpec((bb, L, f), lambda bi, i: (bi, i, 0))

    state = pl.BlockSpec((bb, N_HEADS, HEAD_D, HEAD_D), lambda bi, i: (bi, 0, 0, 0))
    kern = functools.partial(_gdn_kernel, L=L, nt=nt, bb=bb)
    return pl.pallas_call(
        kern, grid=(B // bb, nt),
        in_specs=[row(N_GDN_QKV), row(GZ_PAD),
                  pl.BlockSpec((bb, CONV_W - 1, N_GDN_QKV), lambda bi, i: (bi, 0, 0)), state,
                  _full_spec(conv_w.shape), _full_spec(alog.shape), _full_spec(dtb.shape),
                  _full_spec(ng.shape), _full_spec(hsum.shape)],
        out_specs=[row(GROUP_W), state],
        out_shape=[jax.ShapeDtypeStruct((B, T, GROUP_W), F32),
                   jax.ShapeDtypeStruct((B, N_HEADS, HEAD_D, HEAD_D), F32)],
        scratch_shapes=[pltpu.VMEM((bb, 8 + L, N_GDN_QKV), F32), pltpu.VMEM((bb, N_HEADS, HEAD_D, HEAD_D), F32),
                        pltpu.VMEM((bb, L, GROUP_W), F32)],
        compiler_params=_params(("parallel", "arbitrary")), name="gdn",
    )(qkv, gz, conv0, s0, conv_w, alog, dtb, ng, hsum)


def _s5_kernel(u_ref, h0r_ref, h0i_ref, lr_ref, li_ref, br_ref, bi_ref, cr_ref, ci_ref, d_ref, wg_ref,
               o_ref, hr_out_ref, hi_out_ref, hr_scr, hi_scr, xr_scr, xi_scr, *, tt, nb, nt):
    ti = pl.program_id(0)

    @pl.when(ti == 0)
    def _():
        hr_scr[...] = h0r_ref[...]
        hi_scr[...] = h0i_ref[...]

    u = u_ref[...]
    ub = u.astype(BF16)
    xr_scr[...] = _dot(ub, br_ref[...])
    xi_scr[...] = _dot(ub, bi_ref[...])
    lr = jnp.broadcast_to(lr_ref[...], (nb, S5_STATE))
    li = jnp.broadcast_to(li_ref[...], (nb, S5_STATE))

    def step(t, carry):
        hr, hi = carry
        r0 = pl.multiple_of(t * nb, nb)
        nr = lr * hr - li * hi + xr_scr[pl.ds(r0, nb), :]
        ni = lr * hi + li * hr + xi_scr[pl.ds(r0, nb), :]
        xr_scr[pl.ds(r0, nb), :] = nr
        xi_scr[pl.ds(r0, nb), :] = ni
        return nr, ni

    hr, hi = lax.fori_loop(0, tt, step, (hr_scr[...], hi_scr[...]), unroll=8)
    hr_scr[...] = hr
    hi_scr[...] = hi
    y = _dot(xr_scr[...].astype(BF16), cr_ref[...]) - _dot(xi_scr[...].astype(BF16), ci_ref[...]) + d_ref[...] * u
    act = jax.nn.gelu(y)
    o_ref[...] = act * jax.nn.sigmoid(_dot(act.astype(BF16), wg_ref[...]))

    @pl.when(ti == nt - 1)
    def _():
        hr_out_ref[...] = hr
        hi_out_ref[...] = hi


def _prep_s5_weights(lam_re, lam_im, log_step, b_re, b_im, c_re, c_im, d, w_glu):
    dt = jnp.exp(log_step)[:, None]
    mag = jnp.exp(lam_re * dt)
    bar_re, bar_im = mag * jnp.cos(lam_im * dt), mag * jnp.sin(lam_im * dt)
    den = lam_re * lam_re + lam_im * lam_im
    f_re = ((bar_re - 1.0) * lam_re + bar_im * lam_im) / den
    f_im = (bar_im * lam_re - (bar_re - 1.0) * lam_im) / den
    bb_re = f_re[..., None] * b_re - f_im[..., None] * b_im
    bb_im = f_re[..., None] * b_im + f_im[..., None] * b_re
    eye = jnp.eye(S5_G, dtype=F32)

    def in_mat(b):
        return jnp.einsum('gpc,gh->gchp', b, eye).reshape(S5_G * S5_CG, S5_STATE).astype(BF16)

    def out_mat(c):
        return jnp.einsum('gcp,gh->gphc', c, eye).reshape(S5_STATE, S5_G * S5_CG).astype(BF16)

    return (bar_re.reshape(1, S5_STATE), bar_im.reshape(1, S5_STATE),
            in_mat(bb_re), in_mat(bb_im), out_mat(c_re), out_mat(c_im),
            d.reshape(1, GROUP_W), w_glu.astype(BF16))


def _s5(u_tm, h0_re, h0_im, weights, B, tt):
    T = u_tm.shape[0]
    u2 = u_tm.reshape(T * B, GROUP_W)
    nt = T // tt
    lr, li, br, bi, cr, ci, d, wg = weights
    rows = pl.BlockSpec((tt * B, GROUP_W), lambda i: (i, 0))
    st = _full_spec((B, S5_STATE))
    kern = functools.partial(_s5_kernel, tt=tt, nb=B, nt=nt)
    o, hr, hi = pl.pallas_call(
        kern, grid=(nt,),
        in_specs=[rows, st, st] + [_full_spec(w.shape) for w in weights],
        out_specs=[rows, st, st],
        out_shape=[jax.ShapeDtypeStruct((T * B, GROUP_W), F32), jax.ShapeDtypeStruct((B, S5_STATE), F32),
                   jax.ShapeDtypeStruct((B, S5_STATE), F32)],
        scratch_shapes=[pltpu.VMEM((B, S5_STATE), F32), pltpu.VMEM((B, S5_STATE), F32),
                        pltpu.VMEM((tt * B, S5_STATE), F32), pltpu.VMEM((tt * B, S5_STATE), F32)],
        compiler_params=_params(("arbitrary",)), name="s5",
    )(u2, h0_re.reshape(B, S5_STATE), h0_im.reshape(B, S5_STATE), *weights)
    return o.reshape(T, B * GROUP_W), hr.reshape(B, S5_G, S5_P), hi.reshape(B, S5_G, S5_P)


def _outproj_kernel(oa_ref, ob_ref, oc_ref, od_ref, x_ref, gn_ref, w_ref, g_ref, b_ref, o_ref, *, alpha):
    gn = gn_ref[...]
    mix = jnp.concatenate([
        _rms_rows(oa_ref[...], gn[0:1]).astype(BF16),
        _rms_rows(ob_ref[...], gn[1:2]).astype(BF16),
        oc_ref[...].astype(BF16),
        _rms_rows(od_ref[...], gn[2:3]).astype(BF16)], -1)
    y = alpha * x_ref[...] + _dot(mix, w_ref[...])
    o_ref[...] = _ln_rows(y, g_ref[...], b_ref[...])


def _outproj(oa, ob, oc, od_tm, x, gn, w_out, g, b, alpha, tm):
    B, T, D = x.shape

    def row(f):
        return pl.BlockSpec((None, tm, f), lambda bi, i: (bi, i, 0))

    kern = functools.partial(_outproj_kernel, alpha=alpha)
    return pl.pallas_call(
        kern, grid=(B, T // tm),
        in_specs=[row(GROUP_W), row(GROUP_W), row(GROUP_W), pl.BlockSpec((tm, GROUP_W), lambda bi, i: (i, bi)),
                  row(D), _full_spec(gn.shape), _full_spec(w_out.shape), _full_spec((1, D)), _full_spec((1, D))],
        out_specs=row(D), out_shape=jax.ShapeDtypeStruct((B, T, D), F32),
        compiler_params=_params(("parallel", "parallel")), name="outproj",
    )(oa, ob, oc, od_tm, x, gn, w_out, g.reshape(1, D), b.reshape(1, D))


def _ffn_kernel(x_ref, wg_ref, wu_ref, wd_ref, g_ref, b_ref, o_ref, *, alpha):
    x = x_ref[...]
    xb = x.astype(BF16)
    hid = (jax.nn.silu(_dot(xb, wg_ref[...])) * _dot(xb, wu_ref[...])).astype(BF16)
    y = alpha * x + _dot(hid, wd_ref[...])
    o_ref[...] = _ln_rows(y, g_ref[...], b_ref[...])


def _ffn(x, wg, wu, wd, g, b, alpha, tm):
    B, T, D = x.shape
    row = pl.BlockSpec((None, tm, D), lambda bi, i: (bi, i, 0))

    def resident(shape):
        return pl.BlockSpec(shape, lambda *_: (0,) * len(shape), pipeline_mode=pl.Buffered(1))

    kern = functools.partial(_ffn_kernel, alpha=alpha)
    return pl.pallas_call(
        kern, grid=(B, T // tm),
        in_specs=[row, resident(wg.shape), resident(wu.shape), resident(wd.shape),
                  _full_spec((1, D)), _full_spec((1, D))],
        out_specs=row, out_shape=jax.ShapeDtypeStruct((B, T, D), F32),
        compiler_params=_params(("parallel", "parallel")), name="ffn",
    )(x, wg, wu, wd, g.reshape(1, D), b.reshape(1, D))


def _router_kernel(x_ref, wr_ref, br_ref, idx_ref, gate_ref):
    logits = _dot_nt(wr_ref[...], x_ref[...], HIGHEST) + br_ref[...]
    eid = lax.broadcasted_iota(jnp.int32, logits.shape, 0)
    m1 = jnp.max(logits, 0, keepdims=True)
    i1 = jnp.min(jnp.where(logits == m1, eid, N_EXP), 0, keepdims=True)
    rest = jnp.where(eid == i1, -jnp.inf, logits)
    m2 = jnp.max(rest, 0, keepdims=True)
    i2 = jnp.min(jnp.where(rest == m2, eid, N_EXP), 0, keepdims=True)
    e2 = jnp.exp(m2 - m1)
    den = 1.0 + e2
    idx_ref[...] = jnp.concatenate([i1, i2], 0)
    gate_ref[...] = jnp.concatenate([1.0 / den, e2 / den], 0)


def _router(x2, w_router, b_router, tm):
    N, D = x2.shape
    return pl.pallas_call(
        _router_kernel, grid=(N // tm,),
        in_specs=[pl.BlockSpec((tm, D), lambda i: (i, 0)), _full_spec((N_EXP, D)), _full_spec((N_EXP, 1))],
        out_specs=[pl.BlockSpec((TOP_K, tm), lambda i: (0, i)), pl.BlockSpec((TOP_K, tm), lambda i: (0, i))],
        out_shape=[jax.ShapeDtypeStruct((TOP_K, N), jnp.int32), jax.ShapeDtypeStruct((TOP_K, N), F32)],
        compiler_params=_params(("parallel",)), name="moe_router",
    )(x2, w_router.T, b_router.reshape(N_EXP, 1))


def _expert_kernel(blk_exp_ref, src_ref, dst_ref, gate_ref, x_hbm, wg_ref, wu_ref, wd_ref, out_hbm,
                   xbuf, ybuf, gsem, ssem):
    del blk_exp_ref

    def for_rows(fn):
        def body(r, c):
            fn(r)
            return c
        lax.fori_loop(0, MOE_BLK, body, 0, unroll=MOE_DMA_UNROLL)

    for_rows(lambda r: pltpu.make_async_copy(x_hbm.at[pl.ds(src_ref[0, 0, r], 1)], xbuf.at[pl.ds(r, 1)],
                                             gsem).start())
    pltpu.make_async_copy(x_hbm.at[pl.ds(0, MOE_BLK)], xbuf, gsem).wait()
    xb = xbuf[...].astype(BF16)
    hid = (jax.nn.silu(_dot(xb, wg_ref[...])) * _dot(xb, wu_ref[...])).astype(BF16)
    ybuf[...] = _dot(hid, wd_ref[...]) * gate_ref[...]
    for_rows(lambda r: pltpu.make_async_copy(ybuf.at[pl.ds(r, 1)], out_hbm.at[pl.ds(dst_ref[0, 0, r], 1)],
                                             ssem).start())
    pltpu.make_async_copy(ybuf, out_hbm.at[pl.ds(0, MOE_BLK)], ssem).wait()


def _combine_kernel(x_ref, y_ref, g_ref, b_ref, o_ref, *, alpha):
    D = x_ref.shape[-1]
    f = y_ref[:, :D] + y_ref[:, D:]
    o_ref[...] = _ln_rows(alpha * x_ref[...] + f, g_ref[...], b_ref[...])


def _moe(x, w_router, b_router, wg, wu, wd, g, b, alpha):
    B, T, D = x.shape
    N = B * T
    x2 = x.reshape(N, D)
    tm = math.gcd(N, MOE_BLK)
    idx, gates = _router(x2, w_router, b_router, tm)

    n_slots = N * TOP_K
    e_flat = idx.T.reshape(-1)
    g_flat = gates.T.reshape(-1)
    onehot = (e_flat[:, None] == jnp.arange(N_EXP, dtype=jnp.int32)[None, :]).astype(jnp.int32)
    csum = jnp.cumsum(onehot, 0)
    counts = csum[-1]
    rank = jnp.sum((csum - onehot) * onehot, 1)
    padded = (counts + MOE_BLK - 1) // MOE_BLK * MOE_BLK
    pad_end = jnp.cumsum(padded)
    pad_start = pad_end - padded
    dest = pad_start[e_flat] + rank
    n_blk = -(-(n_slots + N_EXP * (MOE_BLK - 1)) // MOE_BLK)
    rows = n_blk * MOE_BLK
    row_slot = jnp.full((rows,), n_slots, jnp.int32).at[dest].set(jnp.arange(n_slots, dtype=jnp.int32))
    row_gate = jnp.zeros((rows,), F32).at[dest].set(g_flat)
    row_src = jnp.minimum(row_slot // TOP_K, N - 1)
    row_dst = jnp.where(row_slot < n_slots, row_slot, n_slots + jnp.arange(rows, dtype=jnp.int32) % MOE_BLK)
    blk_start = jnp.arange(n_blk, dtype=jnp.int32) * MOE_BLK
    blk_exp = jnp.minimum(jnp.sum((pad_end[None, :] <= blk_start[:, None]).astype(jnp.int32), 1), N_EXP - 1)

    idx_spec = pl.BlockSpec((1, 1, MOE_BLK), lambda i, be: (i, 0, 0), memory_space=pltpu.SMEM)
    y = pl.pallas_call(
        _expert_kernel,
        grid_spec=pltpu.PrefetchScalarGridSpec(
            num_scalar_prefetch=1, grid=(n_blk,),
            in_specs=[idx_spec, idx_spec,
                      pl.BlockSpec((MOE_BLK, 1), lambda i, be: (i, 0)),
                      pl.BlockSpec(memory_space=pl.ANY),
                      pl.BlockSpec((None, D, D_FF_EXP), lambda i, be: (be[i], 0, 0)),
                      pl.BlockSpec((None, D, D_FF_EXP), lambda i, be: (be[i], 0, 0)),
                      pl.BlockSpec((None, D_FF_EXP, D), lambda i, be: (be[i], 0, 0))],
            out_specs=pl.BlockSpec(memory_space=pl.ANY),
            scratch_shapes=[pltpu.VMEM((MOE_BLK, D), F32), pltpu.VMEM((MOE_BLK, D), F32),
                            pltpu.SemaphoreType.DMA(()), pltpu.SemaphoreType.DMA(())]),
        out_shape=jax.ShapeDtypeStruct((n_slots + MOE_BLK, D), F32),
        compiler_params=_params(("arbitrary",)), name="moe_experts",
    )(blk_exp, row_src.reshape(n_blk, 1, MOE_BLK), row_dst.reshape(n_blk, 1, MOE_BLK),
      row_gate.reshape(rows, 1), x2, wg, wu, wd)

    y2 = y.reshape(N + MOE_BLK // TOP_K, TOP_K * D)
    kern = functools.partial(_combine_kernel, alpha=alpha)
    out = pl.pallas_call(
        kern, grid=(N // tm,),
        in_specs=[pl.BlockSpec((tm, D), lambda i: (i, 0)), pl.BlockSpec((tm, TOP_K * D), lambda i: (i, 0)),
                  _full_spec((1, D)), _full_spec((1, D))],
        out_specs=pl.BlockSpec((tm, D), lambda i: (i, 0)), out_shape=jax.ShapeDtypeStruct((N, D), F32),
        compiler_params=_params(("parallel",)), name="moe_combine",
    )(x2, y2, g.reshape(1, D), b.reshape(1, D))
    return out.reshape(B, T, D)


def _round_up(n, m):
    return -(-n // m) * m


def _run_trunk(x, states, P, depth):
    lat_c, kr_c, sbk_c, sbv_c, conv_c, ssm_c, re_c, im_c = states
    B, T, D = x.shape
    past = lat_c.shape[2]
    alpha = (2 * depth) ** 0.25
    long_seq = T >= 512
    tm = 512 if long_seq else T
    tq = ATTN_TQ if long_seq else T
    tk = ATTN_TK if long_seq else LANES
    S = past + T
    S_pad = _round_up(S, tk)

    tab = _rope_table(past, T)
    x = _layer_norm(x, P['ln_in_g'], P['ln_in_b'], tm)
    outs = [[] for _ in range(8)]
    for l in range(depth):
        pm, sbq, sbk, sbv, sbk_b, sbv_b, gq, gz, s5u = _inproj(x, _prep_inproj_weights(P['w_in'][l]), tm)

        qn, kvn, wq, wqr, wk, wv, e = _prep_mla_weights(P['mla_q_norm'][l], P['mla_kv_norm'][l],
                                                        P['mla_w_uq'][l], P['mla_w_ukv'][l])
        q_a, lat_n, kr_n, krp_n = _mla_prep(pm, tab, qn, kvn, wq, wqr, tm)
        kr_cache = jnp.pad(kr_c[l], ((0, 0), (0, 0), (0, LANES - ROPE)))
        lat_all = jnp.pad(jnp.concatenate([lat_c[l], lat_n], 1), ((0, 0), (0, S_pad - S), (0, 0)))
        krp_all = jnp.pad(jnp.concatenate([kr_cache, krp_n], 1), ((0, 0), (0, S_pad - S), (0, 0)))
        k_a, v_a = _mla_kv(lat_all, krp_all, wk, wv, e, tk)
        o_a = _mla_attn(q_a, k_a, v_a, past, tq, tk)

        if past or S_pad != S:
            k_all = jnp.pad(jnp.concatenate([sbk_c[l].reshape(B, past, GROUP_W).astype(BF16), sbk_b], 1),
                            ((0, 0), (0, S_pad - S), (0, 0)))
            v_all = jnp.pad(jnp.concatenate([sbv_c[l].reshape(B, past, GROUP_W).astype(BF16), sbv_b], 1),
                            ((0, 0), (0, S_pad - S), (0, 0)))
        else:
            k_all, v_all = sbk_b, sbv_b
        o_b = _sb_attn(sbq, k_all, v_all, past, tq, tk)

        o_c, ssm_n = _gdn(gq, gz, conv_c[l], ssm_c[l], P['gdn_conv_w'][l], P['gdn_a_log'][l],
                          P['gdn_dt_bias'][l], P['gdn_norm'][l])
        conv_n = jnp.concatenate([conv_c[l], gq], 1)[:, T:]

        s5w = _prep_s5_weights(P['s5_lam_re'][l], P['s5_lam_im'][l], P['s5_log_step'][l], P['s5_b_re'][l],
                               P['s5_b_im'][l], P['s5_c_re'][l], P['s5_c_im'][l], P['s5_d'][l], P['s5_w_glu'][l])
        o_d, re_n, im_n = _s5(s5u, re_c[l], im_c[l], s5w, B, min(64, T))

        x = _outproj(o_a, o_b, o_c, o_d, x, P['grp_norm'][l], P['w_out'][l].astype(BF16),
                     P['ln1_g'][l], P['ln1_b'][l], alpha, tm)
        i = l // 2
        if l % 2 == 0:
            x = _ffn(x, P['ffn_w_gate'][i].astype(BF16), P['ffn_w_up'][i].astype(BF16),
                     P['ffn_w_down'][i].astype(BF16), P['ln2_g'][l], P['ln2_b'][l], alpha, min(tm, 256))
        else:
            x = _moe(x, P['moe_w_router'][i], P['moe_b_router'][i], P['moe_w_gate'][i].astype(BF16),
                     P['moe_w_up'][i].astype(BF16), P['moe_w_down'][i].astype(BF16),
                     P['ln2_g'][l], P['ln2_b'][l], alpha)
        news = (lat_n, kr_n, sbk.reshape(B, T, N_HEADS, HEAD_D), sbv.reshape(B, T, N_HEADS, HEAD_D),
                conv_n, ssm_n, re_n, im_n)
        for lst, a in zip(outs, news):
            lst.append(a)
    return x, [jnp.stack(lst) for lst in outs]


def kernel(x_prompt, x_sample, cache_mla_latent, cache_mla_krope, cache_sb_k, cache_sb_v,
           state_gdn_conv, state_gdn_ssm, state_s5_re, state_s5_im,
           ln_in_g, ln_in_b, w_in, mla_q_norm, mla_kv_norm, mla_w_uq, mla_w_ukv,
           gdn_conv_w, gdn_a_log, gdn_dt_bias, gdn_norm,
           s5_lam_re, s5_lam_im, s5_log_step, s5_b_re, s5_b_im, s5_c_re, s5_c_im, s5_d, s5_w_glu,
           grp_norm, w_out, ln1_g, ln1_b, ln2_g, ln2_b,
           ffn_w_gate, ffn_w_up, ffn_w_down,
           moe_w_router, moe_b_router, moe_w_gate, moe_w_up, moe_w_down):
    P = dict(ln_in_g=ln_in_g, ln_in_b=ln_in_b, w_in=w_in, mla_q_norm=mla_q_norm, mla_kv_norm=mla_kv_norm,
             mla_w_uq=mla_w_uq, mla_w_ukv=mla_w_ukv, gdn_conv_w=gdn_conv_w, gdn_a_log=gdn_a_log,
             gdn_dt_bias=gdn_dt_bias, gdn_norm=gdn_norm, s5_lam_re=s5_lam_re, s5_lam_im=s5_lam_im,
             s5_log_step=s5_log_step, s5_b_re=s5_b_re, s5_b_im=s5_b_im, s5_c_re=s5_c_re, s5_c_im=s5_c_im,
             s5_d=s5_d, s5_w_glu=s5_w_glu, grp_norm=grp_norm, w_out=w_out, ln1_g=ln1_g, ln1_b=ln1_b,
             ln2_g=ln2_g, ln2_b=ln2_b, ffn_w_gate=ffn_w_gate, ffn_w_up=ffn_w_up, ffn_w_down=ffn_w_down,
             moe_w_router=moe_w_router, moe_b_router=moe_b_router, moe_w_gate=moe_w_gate,
             moe_w_up=moe_w_up, moe_w_down=moe_w_down)
    depth = w_in.shape[0]
    Bp = x_prompt.shape[0]
    dt = x_prompt.dtype
    prompt_states = (jnp.zeros((depth, Bp, 0, KV_LORA), dt), jnp.zeros((depth, Bp, 0, ROPE), dt),
                     jnp.zeros((depth, Bp, 0, N_HEADS, HEAD_D), dt), jnp.zeros((depth, Bp, 0, N_HEADS, HEAD_D), dt),
                     jnp.zeros((depth, Bp, CONV_W - 1, N_GDN_QKV), dt),
                     jnp.zeros((depth, Bp, N_HEADS, HEAD_D, HEAD_D), F32),
                     jnp.zeros((depth, Bp, S5_G, S5_P), F32), jnp.zeros((depth, Bp, S5_G, S5_P), F32))
    y_prompt, st_p = _run_trunk(x_prompt, prompt_states, P, depth)
    sample_states = (cache_mla_latent, cache_mla_krope, cache_sb_k, cache_sb_v,
                     state_gdn_conv, state_gdn_ssm, state_s5_re, state_s5_im)
    y_sample, st_s = _run_trunk(x_sample, sample_states, P, depth)
    return (y_prompt, y_sample, *st_p, *st_s)
```

```python
import functools
import math

import jax
import jax.numpy as jnp
from jax import lax
from jax.experimental import pallas as pl
from jax.experimental.pallas import tpu as pltpu

F32 = jnp.float32
BF16 = jnp.bfloat16
HIGHEST = lax.Precision.HIGHEST

D_MODEL = 1024
CHUNK = 64
GROUP_W = 256
N_HEADS = 4
HEAD_D = 64
NOPE = 64
ROPE = 32
Q_LORA = 192
KV_LORA = 128
ROPE_THETA = 10000.0
MLA_SCALE = (NOPE + ROPE) ** -0.5
SB_SCALE = HEAD_D ** -0.5
LOG2E = math.log2(math.e)
CONV_W = 4
N_GDN_QKV = 768
S5_CG = 16
S5_G = 16
S5_P = 64
S5_STATE = S5_G * S5_P
D_FF = 2816
N_EXP = 8
TOP_K = 2
D_FF_EXP = 1792
MOE_BLK = 512
LN_EPS = 1e-5
RMS_EPS = 1e-6
NEG_INF = -1e30
OFF_SB = Q_LORA + KV_LORA + ROPE
OFF_GDN = OFF_SB + 3 * GROUP_W
OFF_S5 = OFF_GDN + N_GDN_QKV + 2 * N_HEADS + GROUP_W

LANES = 128
HEAD_PAD = LANES
VMEM_LIMIT = 48 * 1024 * 1024
GDN_BATCH_PER_STEP = 4
MOE_DMA_UNROLL = 16
ATTN_TQ = 512
ATTN_TK = 512
MLA_TILE = (256, 512)
SB_TILE = (512, 256)


def _params(sem):
    return pltpu.CompilerParams(dimension_semantics=sem, vmem_limit_bytes=VMEM_LIMIT)


def _dot(a, b, precision=None):
    return jnp.dot(a, b, preferred_element_type=F32, precision=precision)


def _dot_nt(a, b, precision=None):
    return lax.dot_general(a, b, (((1,), (1,)), ((), ())), preferred_element_type=F32, precision=precision)


def _dot_tn(a, b, precision=None):
    return lax.dot_general(a, b, (((0,), (0,)), ((), ())), preferred_element_type=F32, precision=precision)


def _ln_rows(x, g, b):
    mu = jnp.mean(x, -1, keepdims=True)
    xc = x - mu
    var = jnp.mean(xc * xc, -1, keepdims=True)
    return xc * lax.rsqrt(var + LN_EPS) * g + b


def _rms_rows(x, g, n=None):
    n = x.shape[-1] if n is None else n
    ms = jnp.sum(x * x, -1, keepdims=True) * (1.0 / n)
    return x * lax.rsqrt(ms + RMS_EPS) * g


def _full_spec(shape):
    nd = len(shape)
    return pl.BlockSpec(shape, lambda *_: (0,) * nd)


def _ln_kernel(x_ref, g_ref, b_ref, o_ref):
    o_ref[...] = _ln_rows(x_ref[...], g_ref[...], b_ref[...])


def _layer_norm(x, g, b, tm):
    B, T, D = x.shape
    row = pl.BlockSpec((None, tm, D), lambda bi, i: (bi, i, 0))
    return pl.pallas_call(
        _ln_kernel, grid=(B, T // tm),
        in_specs=[row, _full_spec((1, D)), _full_spec((1, D))],
        out_specs=row, out_shape=jax.ShapeDtypeStruct((B, T, D), F32),
        compiler_params=_params(("parallel", "parallel")), name="ln_in",
    )(x, g.reshape(1, D), b.reshape(1, D))


MLA_IN_PAD = 640
GZ_PAD = 384


def _inproj_kernel(x_ref, wm_ref, wsb_ref, wgq_ref, wgz_ref, ws5_ref,
                   pm_ref, sbq_ref, sbk_ref, sbv_ref, sbkb_ref, sbvb_ref, gq_ref, gz_ref, s5_ref):
    xb = x_ref[...].astype(BF16)
    pm_ref[...] = _dot(xb, wm_ref[...])
    sb = _dot(xb, wsb_ref[...])
    sbq_ref[...] = (sb[:, :GROUP_W] * (SB_SCALE * LOG2E)).astype(BF16)
    sbk_ref[...] = sb[:, GROUP_W:2 * GROUP_W]
    sbv_ref[...] = sb[:, 2 * GROUP_W:]
    sbkb_ref[...] = sb[:, GROUP_W:2 * GROUP_W].astype(BF16)
    sbvb_ref[...] = sb[:, 2 * GROUP_W:].astype(BF16)
    gq_ref[...] = _dot(xb, wgq_ref[...])
    gz_ref[...] = _dot(xb, wgz_ref[...])
    s5_ref[...] = _dot(xb, ws5_ref[...])


def _rotate_half_cols(w):
    half = w.shape[-1] // 2
    return jnp.concatenate([-w[..., half:], w[..., :half]], -1)


def _prep_inproj_weights(w_in):
    zeros = lambda n: jnp.zeros((D_MODEL, n), F32)
    w_cq = w_in[:, :Q_LORA]
    w_ckv = w_in[:, Q_LORA:Q_LORA + KV_LORA]
    w_kr = w_in[:, Q_LORA + KV_LORA:OFF_SB]
    wm = jnp.concatenate([w_ckv, w_cq, zeros(64), w_kr, zeros(96), _rotate_half_cols(w_kr), zeros(96)], 1)
    wsb = w_in[:, OFF_SB:OFF_GDN]
    wgq = w_in[:, OFF_GDN:OFF_GDN + N_GDN_QKV]
    o_a = OFF_GDN + N_GDN_QKV
    wgz = jnp.concatenate([w_in[:, o_a + 2 * N_HEADS:OFF_S5], w_in[:, o_a:o_a + 2 * N_HEADS],
                           zeros(GZ_PAD - GROUP_W - 2 * N_HEADS)], 1)
    ws5 = w_in[:, OFF_S5:]
    return tuple(w.astype(BF16) for w in (wm, wsb, wgq, wgz, ws5))


def _inproj(x, weights, tm):
    B, T, D = x.shape
    wm, wsb, wgq, wgz, ws5 = weights

    def row(f):
        return pl.BlockSpec((None, tm, f), lambda bi, i: (bi, i, 0))

    def out(f, dt=F32):
        return jax.ShapeDtypeStruct((B, T, f), dt)

    return pl.pallas_call(
        _inproj_kernel, grid=(B, T // tm),
        in_specs=[row(D)] + [_full_spec(w.shape) for w in weights],
        out_specs=[row(MLA_IN_PAD), row(GROUP_W), row(GROUP_W), row(GROUP_W), row(GROUP_W), row(GROUP_W),
                   row(N_GDN_QKV), row(GZ_PAD), pl.BlockSpec((tm, GROUP_W), lambda bi, i: (i, bi))],
        out_shape=[out(MLA_IN_PAD), out(GROUP_W, BF16), out(GROUP_W), out(GROUP_W), out(GROUP_W, BF16),
                   out(GROUP_W, BF16), out(N_GDN_QKV), out(GZ_PAD),
                   jax.ShapeDtypeStruct((T, B * GROUP_W), F32)],
        compiler_params=_params(("parallel", "parallel")), name="inproj",
    )(x, *weights)


def _mla_prep_kernel(pm_ref, tab_ref, qn_ref, kvn_ref, wq_ref, wqr_ref, q_ref, lat_ref, kr_ref, krp_ref):
    pm = pm_ref[...]
    tab = tab_ref[...]
    c_kv = pm[:, :KV_LORA]
    c_q = pm[:, KV_LORA:KV_LORA + 2 * LANES]
    k_r = pm[:, 3 * LANES:4 * LANES]
    k_rr = pm[:, 4 * LANES:5 * LANES]
    lat_ref[...] = _rms_rows(c_kv, kvn_ref[...])
    nq = _rms_rows(c_q, qn_ref[...], n=Q_LORA).astype(BF16)
    qp = _dot(nq, wq_ref[...])
    qr = _dot(nq, wqr_ref[...])
    cq, sq = tab[:, :LANES], tab[:, LANES:2 * LANES]
    for h in range(N_HEADS):
        sl = slice(h * HEAD_PAD, (h + 1) * HEAD_PAD)
        q_ref[:, sl] = ((qp[:, sl] * cq + qr[:, sl] * sq) * (MLA_SCALE * LOG2E)).astype(BF16)
    kr_new = k_r * tab[:, 2 * LANES:3 * LANES] + k_rr * tab[:, 3 * LANES:]
    krp_ref[...] = kr_new
    kr_ref[...] = kr_new[:, :ROPE]


def _rope_table(past, T):
    half = ROPE // 2
    inv = ROPE_THETA ** (-jnp.arange(half, dtype=F32) / half)
    ang = (past + jnp.arange(T, dtype=jnp.int32)).astype(F32)[:, None] * inv
    c, s = jnp.cos(ang), jnp.sin(ang)
    one, zero = jnp.ones((T, NOPE), F32), jnp.zeros((T, NOPE), F32)
    z32, z96 = jnp.zeros((T, 32), F32), jnp.zeros((T, 96), F32)
    return jnp.concatenate([one, c, c, z32, zero, s, s, z32, c, c, z96, s, s, z96], 1)


def _prep_mla_weights(q_norm, kv_norm, w_uq, w_ukv):
    w3 = w_uq.reshape(Q_LORA, N_HEADS, NOPE + ROPE)
    zq = jnp.zeros((Q_LORA, N_HEADS, HEAD_PAD - NOPE - ROPE), F32)
    wq = jnp.concatenate([w3, zq], -1).reshape(Q_LORA, N_HEADS * HEAD_PAD)
    w3r = jnp.concatenate([jnp.zeros((Q_LORA, N_HEADS, NOPE), F32), _rotate_half_cols(w3[..., NOPE:]), zq], -1)
    wqr = w3r.reshape(Q_LORA, N_HEADS * HEAD_PAD)
    pad_rows = jnp.zeros((2 * LANES - Q_LORA, N_HEADS * HEAD_PAD), F32)
    wq = jnp.concatenate([wq, pad_rows], 0).astype(BF16)
    wqr = jnp.concatenate([wqr, pad_rows], 0).astype(BF16)
    qn = jnp.concatenate([q_norm, jnp.zeros((2 * LANES - Q_LORA,), F32)]).reshape(1, 2 * LANES)
    kvn = kv_norm.reshape(1, KV_LORA)
    kv3 = w_ukv.reshape(KV_LORA, N_HEADS, NOPE + HEAD_D)
    zk = jnp.zeros((KV_LORA, N_HEADS, HEAD_PAD - NOPE), F32)
    wk = jnp.concatenate([kv3[..., :NOPE], zk], -1).reshape(KV_LORA, N_HEADS * HEAD_PAD).astype(BF16)
    wv = kv3[..., NOPE:].reshape(KV_LORA, GROUP_W).astype(BF16)
    e = jnp.zeros((LANES, N_HEADS, HEAD_PAD), F32)
    e = e.at[jnp.arange(ROPE)[:, None], jnp.arange(N_HEADS)[None, :], NOPE + jnp.arange(ROPE)[:, None]].set(1.0)
    e = e.reshape(LANES, N_HEADS * HEAD_PAD).astype(BF16)
    return qn, kvn, wq, wqr, wk, wv, e


def _mla_prep(pm, tab, qn, kvn, wq, wqr, tm):
    B, T, _ = pm.shape

    def row(f):
        return pl.BlockSpec((None, tm, f), lambda bi, i: (bi, i, 0))

    return pl.pallas_call(
        _mla_prep_kernel, grid=(B, T // tm),
        in_specs=[row(MLA_IN_PAD), pl.BlockSpec((tm, 4 * LANES), lambda bi, i: (i, 0)),
                  _full_spec(qn.shape), _full_spec(kvn.shape), _full_spec(wq.shape), _full_spec(wqr.shape)],
        out_specs=[row(N_HEADS * HEAD_PAD), row(KV_LORA), row(ROPE), row(LANES)],
        out_shape=[jax.ShapeDtypeStruct((B, T, N_HEADS * HEAD_PAD), BF16),
                   jax.ShapeDtypeStruct((B, T, KV_LORA), F32),
                   jax.ShapeDtypeStruct((B, T, ROPE), F32),
                   jax.ShapeDtypeStruct((B, T, LANES), F32)],
        compiler_params=_params(("parallel", "parallel")), name="mla_prep",
    )(pm, tab, qn, kvn, wq, wqr)


def _mla_kv_kernel(lat_ref, krp_ref, wk_ref, wv_ref, e_ref, k_ref, v_ref):
    lb = lat_ref[...].astype(BF16)
    kb = krp_ref[...].astype(BF16)
    k_ref[...] = (_dot(lb, wk_ref[...]) + _dot(kb, e_ref[...])).astype(BF16)
    v_ref[...] = _dot(lb, wv_ref[...]).astype(BF16)


def _mla_kv(lat, krp, wk, wv, e, tm):
    B, S, _ = lat.shape

    def row(f):
        return pl.BlockSpec((None, tm, f), lambda bi, i: (bi, i, 0))

    return pl.pallas_call(
        _mla_kv_kernel, grid=(B, S // tm),
        in_specs=[row(KV_LORA), row(LANES), _full_spec(wk.shape), _full_spec(wv.shape), _full_spec(e.shape)],
        out_specs=[row(N_HEADS * HEAD_PAD), row(GROUP_W)],
        out_shape=[jax.ShapeDtypeStruct((B, S, N_HEADS * HEAD_PAD), BF16),
                   jax.ShapeDtypeStruct((B, S, GROUP_W), BF16)],
        compiler_params=_params(("parallel", "parallel")), name="mla_kv",
    )(lat, krp, wk, wv, e)


def _lane_tile(a, n):
    return a if n == 1 else jnp.concatenate([a] * n, axis=1)


def _per_head_lanes(stats):
    low = lax.broadcasted_iota(jnp.int32, stats[0].shape, 1) < HEAD_D
    return jnp.concatenate([jnp.where(low, stats[0], stats[1]), jnp.where(low, stats[2], stats[3])], axis=1)


def _block_diag_values(vc):
    head = lax.broadcasted_iota(jnp.int32, vc.shape, 1) // HEAD_D
    return jnp.concatenate([jnp.where(head == h, vc, jnp.zeros_like(vc)) for h in range(N_HEADS)], axis=0)


def _mla_attn_kernel(q_ref, k_ref, v_ref, o_ref, m_scr, l_scr, acc_scr, *, tq, tk, rq, ck, past, nkv):
    qi = pl.program_id(1)
    kj = pl.program_id(2)
    delta = past + qi * tq - kj * tk

    @pl.when(kj == 0)
    def _():
        m_scr[...] = jnp.full(m_scr.shape, NEG_INF, F32)
        l_scr[...] = jnp.zeros(l_scr.shape, F32)
        acc_scr[...] = jnp.zeros(acc_scr.shape, F32)

    def tiles(masked):
        v_bd = [_block_diag_values(v_ref[c * ck:(c + 1) * ck, :]) for c in range(tk // ck)]
        for r in range(tq // rq):
            rows = slice(r * rq, (r + 1) * rq)
            m = [m_scr[h, rows, :] for h in range(N_HEADS)]
            l = [l_scr[h, rows, :] for h in range(N_HEADS)]
            acc = acc_scr[rows, :]
            for c in range(tk // ck):
                keys = slice(c * ck, (c + 1) * ck)
                if masked:
                    q_chunk = (past + qi * tq + r * rq + lax.broadcasted_iota(jnp.int32, (rq, ck), 0)) // CHUNK
                    k_chunk = (kj * tk + c * ck + lax.broadcasted_iota(jnp.int32, (rq, ck), 1)) // CHUNK
                    allowed = k_chunk <= q_chunk
                ps, alphas = [], []
                for h in range(N_HEADS):
                    sl = slice(h * HEAD_PAD, (h + 1) * HEAD_PAD)
                    s = _dot_nt(q_ref[rows, sl], k_ref[keys, sl])
                    if masked:
                        s = jnp.where(allowed, s, NEG_INF)
                    m_new = jnp.maximum(m[h], jnp.max(s, -1, keepdims=True))
                    alpha = jnp.exp2(m[h] - m_new)
                    p = jnp.exp2(s - _lane_tile(m_new, ck // LANES))
                    l[h] = alpha * l[h] + jnp.sum(p, -1, keepdims=True)
                    m[h] = m_new
                    ps.append(p.astype(BF16))
                    alphas.append(alpha)
                acc = _per_head_lanes(alphas) * acc + _dot(jnp.concatenate(ps, axis=1), v_bd[c])
            for h in range(N_HEADS):
                m_scr[h, rows, :] = m[h]
                l_scr[h, rows, :] = l[h]
            acc_scr[rows, :] = acc

    @pl.when(delta >= tk)
    def _():
        tiles(False)

    @pl.when((delta < tk) & (delta + tq > 0))
    def _():
        tiles(True)

    @pl.when(kj == nkv - 1)
    def _():
        o_ref[...] = acc_scr[...] / _per_head_lanes([l_scr[h] for h in range(N_HEADS)])


def _mla_attn(q, k, v, past, tq, tk):
    B, T, W = q.shape
    S = k.shape[1]
    nq, nkv = T // tq, S // tk
    rq, ck = min(MLA_TILE[0], tq), min(MLA_TILE[1], tk)

    def kv_map(bi, i, j):
        last = (past + (i + 1) * tq - 1) // tk
        return (bi, jnp.minimum(j, last), 0)

    kern = functools.partial(_mla_attn_kernel, tq=tq, tk=tk, rq=rq, ck=ck, past=past, nkv=nkv)
    return pl.pallas_call(
        kern, grid=(B, nq, nkv),
        in_specs=[pl.BlockSpec((None, tq, W), lambda bi, i, j: (bi, i, 0)),
                  pl.BlockSpec((None, tk, W), kv_map), pl.BlockSpec((None, tk, GROUP_W), kv_map)],
        out_specs=pl.BlockSpec((None, tq, GROUP_W), lambda bi, i, j: (bi, i, 0)),
        out_shape=jax.ShapeDtypeStruct((B, T, GROUP_W), F32),
        scratch_shapes=[pltpu.VMEM((N_HEADS, tq, LANES), F32), pltpu.VMEM((N_HEADS, tq, LANES), F32),
                        pltpu.VMEM((tq, GROUP_W), F32)],
        compiler_params=_params(("parallel", "parallel", "arbitrary")), name="mla_attn",
    )(q, k, v)


def _sb_attn_kernel(q_ref, k_ref, v_ref, u_ref, o_ref, c_scr, acc_scr, *, tq, tk, rq, ck, past, nkv):
    qi = pl.program_id(1)
    j = pl.program_id(2)
    last = (past + (qi + 1) * tq - 2) // tk
    jb = last - j
    delta = past + qi * tq - jb * tk

    @pl.when(j == 0)
    def _():
        c_scr[...] = jnp.zeros(c_scr.shape, F32)
        acc_scr[...] = jnp.zeros(acc_scr.shape, F32)

    def tiles(masked):
        u2 = u_ref[...]
        v_bd = [_block_diag_values(v_ref[c * ck:(c + 1) * ck, :]) for c in range(tk // ck)]
        head = lax.broadcasted_iota(jnp.int32, (rq, GROUP_W), 1) // HEAD_D
        for r in range(tq // rq):
            rows = slice(r * rq, (r + 1) * rq)
            q_r = q_ref[rows, :]
            q_h = [jnp.where(head == h, q_r, jnp.zeros_like(q_r)) for h in range(N_HEADS)]
            carry = [c_scr[h, rows, :] for h in range(N_HEADS)]
            acc = acc_scr[rows, :]
            for c in reversed(range(tk // ck)):
                k_c = k_ref[c * ck:(c + 1) * ck, :]
                if masked:
                    q_pos = delta + r * rq + lax.broadcasted_iota(jnp.int32, (rq, ck), 0)
                    allowed = (c * ck + lax.broadcasted_iota(jnp.int32, (rq, ck), 1)) < q_pos
                ws = []
                for h in range(N_HEADS):
                    z = _dot_nt(q_h[h], k_c)
                    sp = jnp.log(1.0 + jnp.exp2(-jnp.abs(z))) * LOG2E
                    log_beta = jnp.minimum(z, 0.0) - sp
                    log_1m = log_beta - z
                    if masked:
                        log_1m = jnp.where(allowed, log_1m, 0.0)
                    after = _dot(log_1m.astype(BF16), u2) + _lane_tile(carry[h], ck // LANES)
                    w = jnp.exp2(log_beta + after)
                    if masked:
                        w = jnp.where(allowed, w, 0.0)
                    ws.append(w.astype(BF16))
                    carry[h] = carry[h] + jnp.sum(log_1m, -1, keepdims=True)
                acc = acc + _dot(jnp.concatenate(ws, axis=1), v_bd[c])
            for h in range(N_HEADS):
                c_scr[h, rows, :] = carry[h]
            acc_scr[rows, :] = acc

    @pl.when((delta >= tk) & (jb >= 0))
    def _():
        tiles(False)

    @pl.when((delta < tk) & (jb >= 0))
    def _():
        tiles(True)

    @pl.when(j == nkv - 1)
    def _():
        o_ref[...] = acc_scr[...]


def _sb_attn(q, k, v, past, tq, tk):
    B, T, W = q.shape
    S = k.shape[1]
    nq, nkv = T // tq, S // tk
    rq, ck = min(SB_TILE[0], tq), min(SB_TILE[1], tk)
    later =lax.broadcasted_iota(jnp.int32, (ck, ck), 0) > lax.broadcasted_iota(jnp.int32, (ck, ck), 1)
    u2 = later.astype(BF16)

    def kv_map(bi, i, j):
        last = (past + (i + 1) * tq - 2) // tk
        return (bi, jnp.maximum(last - j, 0), 0)

    kern = functools.partial(_sb_attn_kernel, tq=tq, tk=tk, rq=rq, ck=ck, past=past, nkv=nkv)
    return pl.pallas_call(
        kern, grid=(B, nq, nkv),
        in_specs=[pl.BlockSpec((None, tq, W), lambda bi, i, j: (bi, i, 0)),
                  pl.BlockSpec((None, tk, W), kv_map), pl.BlockSpec((None, tk, W), kv_map),
                  _full_spec((ck, ck))],
        out_specs=pl.BlockSpec((None, tq, GROUP_W), lambda bi, i, j: (bi, i, 0)),
        out_shape=jax.ShapeDtypeStruct((B, T, GROUP_W), F32),
        scratch_shapes=[pltpu.VMEM((N_HEADS, tq, LANES), F32), pltpu.VMEM((tq, GROUP_W), F32)],
        compiler_params=_params(("parallel", "parallel", "arbitrary")), name="sb_attn",
    )(q, k, v, u2)


def _softplus(x):
    return jnp.maximum(x, 0.0) + jnp.log1p(jnp.exp(-jnp.abs(x)))


def _split_bf16(a):
    hi = a.astype(BF16)
    return hi, (a - hi.astype(F32)).astype(BF16)


def _dot_split(a, b):
    a_hi, a_lo = _split_bf16(a)
    b_hi, b_lo = _split_bf16(b)
    m = a.shape[0]
    top = _dot(jnp.concatenate([a_hi, a_lo], 0), b_hi)
    return top[:m] + top[m:] + _dot(a_hi, b_lo)


def _head_sumsq(x, hsum):
    hi, lo = _split_bf16(x * x)
    m = x.shape[0]
    both = _dot(jnp.concatenate([hi, lo], 0), hsum)
    return both[:m] + both[m:]


def _gdn_kernel(qkv_ref, gz_ref, conv0_ref, s0_ref, cw_ref, alog_ref, dtb_ref, ng_ref, hsum_ref,
                o_ref, s_out_ref, xp_scr, s_scr, o_scr, *, L, nt, bb):
    ti = pl.program_id(1)
    halo = CONV_W - 1

    @pl.when(ti == 0)
    def _():
        xp_scr[:, 8 - halo:8, :] = conv0_ref[...]
        s_scr[...] = s0_ref[...]

    hsum = hsum_ref[...]
    ii = lax.broadcasted_iota(jnp.int32, (L, L), 0)
    jj = lax.broadcasted_iota(jnp.int32, (L, L), 1)
    incl = ii >= jj
    strict = ii > jj
    eye = (ii == jj).astype(F32)

    units = []
    for b in range(bb):
        x = qkv_ref[b]
        xp_scr[b, 8:8 + L, :] = x
        conv = jnp.zeros((L, N_GDN_QKV), F32)
        for i in range(CONV_W):
            conv = conv + xp_scr[b, 8 - halo + i:8 - halo + i + L, :] * cw_ref[i:i + 1, :]
        xp_scr[b, 8 - halo:8, :] = x[L - halo:, :]
        conv = jax.nn.silu(conv)
        qa = conv[:, :GROUP_W]
        ka = conv[:, GROUP_W:2 * GROUP_W]
        va = conv[:, 2 * GROUP_W:]
        qa = qa * lax.rsqrt(_head_sumsq(qa, hsum) + 1e-6) * (HEAD_D ** -0.5)
        ka = ka * lax.rsqrt(_head_sumsq(ka, hsum) + 1e-6)
        ab = gz_ref[b, :, GROUP_W:]
        g_all = -jnp.exp(alog_ref[...]) * _softplus(ab + dtb_ref[...])
        beta_all = jax.nn.sigmoid(ab)
        for h in range(N_HEADS):
            hs = slice(h * HEAD_D, (h + 1) * HEAD_D)
            units.append(dict(b=b, h=h, q=qa[:, hs], k=ka[:, hs], v=va[:, hs], g=g_all[:, h:h + 1],
                              beta=beta_all[:, N_HEADS + h:N_HEADS + h + 1]))

    for un in units:
        g_b = jnp.broadcast_to(un['g'], (L, L))
        gc_row = jnp.sum(jnp.where(ii <= jj, g_b, 0.0), 0, keepdims=True)
        gc_col = jnp.sum(jnp.where(ii == jj, jnp.broadcast_to(gc_row, (L, L)), 0.0), 1, keepdims=True)
        un['decay'] = jnp.where(incl, jnp.exp(jnp.where(incl, gc_col - gc_row, 0.0)), 0.0)
        un['e_col'] = jnp.exp(gc_col)
        gc_last = gc_row[:, L - 1:L]
        un['g_last'] = jnp.exp(gc_last)
        un['k_tail'] = jnp.exp(gc_last - gc_col)
        un['kb'] = un['k'] * un['beta']
        un['kbf'] = un['k'].astype(BF16)
    for un in units:
        m = jnp.where(strict, _dot_nt(un['kb'].astype(BF16), un['kbf']) * un['decay'], 0.0)
        un['p'] = -m
        un['t'] = eye - m

    n_sq = int(math.log2(L)) - 1
    for r in range(n_sq + 1):
        for un in units:
            p_hi, p_lo = _split_bf16(un['p'])
            if r == 0:
                lhs_hi, lhs_lo = p_hi, p_lo
            else:
                t_hi, t_lo = _split_bf16(un['t'])
                lhs_hi = jnp.concatenate([p_hi, t_hi], 0) if r < n_sq else t_hi
                lhs_lo = jnp.concatenate([p_lo, t_lo], 0) if r < n_sq else t_lo
            rows = lhs_hi.shape[0]
            top = _dot(jnp.concatenate([lhs_hi, lhs_lo], 0), p_hi)
            prod = top[:rows] + top[rows:] + _dot(lhs_hi, p_lo)
            if r == 0:
                un['p'] = prod
            elif r < n_sq:
                un['p'] = prod[:L]
                un['t'] = un['t'] + prod[L:]
            else:
                un['t'] = un['t'] + prod
    for un in units:
        un['u'] = _dot_split(un['t'], un['v'] * un['beta'])
        un['w'] = _dot_split(un['t'], un['kb'] * un['e_col'])
        un['attn'] = jnp.where(incl, _dot_nt(un['q'].astype(BF16), un['kbf']) * un['decay'], 0.0).astype(BF16)

    for un in units:
        un['S'] = s_scr[un['b'], un['h']]
        un['Sb'] = un['S'].astype(BF16)
    for un in units:
        un['v_new'] = (un['u'] - _dot(un['w'].astype(BF16), un['Sb'])).astype(BF16)
    for un in units:
        hs = slice(un['h'] * HEAD_D, (un['h'] + 1) * HEAD_D)
        o_h = _dot((un['q'] * un['e_col']).astype(BF16), un['Sb']) + _dot(un['attn'], un['v_new'])
        o_scr[un['b'], :, hs] = o_h
        k_dec = (un['k'] * un['k_tail']).astype(BF16)
        s_scr[un['b'], un['h']] = un['S'] * un['g_last'] + _dot_tn(k_dec, un['v_new'])

    for b in range(bb):
        o = o_scr[b]
        ms = _head_sumsq(o, hsum) * (1.0 / HEAD_D)
        o_ref[b] = o * lax.rsqrt(ms + RMS_EPS) * ng_ref[...] * jax.nn.silu(gz_ref[b, :, :GROUP_W])

    @pl.when(ti == nt - 1)
    def _():
        s_out_ref[...] = s_scr[...]


def _gdn(qkv, gz, conv0, s0, conv_w, a_log, dt_bias, norm_g):
    B, T, _ = qkv.shape
    L = min(CHUNK, T)
    nt = T // L
    pad = lambda a: jnp.concatenate([a, jnp.zeros((LANES - a.shape[0],), F32)]).reshape(1, LANES)
    alog = pad(a_log)
    dtb = pad(dt_bias)
    ng = jnp.tile(norm_g, N_HEADS).reshape(1, GROUP_W)
    hid = jnp.arange(GROUP_W) // HEAD_D
    hsum = (hid[:, None] == hid[None, :]).astype(BF16)
    bb = GDN_BATCH_PER_STEP

    def row(f):
        return pl.BlockSpec((bb, L, f), lambda bi, i: (bi, i, 0))

    state = pl.BlockSpec((bb, N_HEADS, HEAD_D, HEAD_D), lambda bi, i: (bi, 0, 0, 0))
    kern = functools.partial(_gdn_kernel, L=L, nt=nt, bb=bb)
    return pl.pallas_call(
        kern, grid=(B // bb, nt),
        in_specs=[row(N_GDN_QKV), row(GZ_PAD),
                  pl.BlockSpec((bb, CONV_W - 1, N_GDN_QKV), lambda bi, i: (bi, 0, 0)), state,
                  _full_spec(conv_w.shape), _full_spec(alog.shape), _full_spec(dtb.shape),
                  _full_spec(ng.shape), _full_spec(hsum.shape)],
        out_specs=[row(GROUP_W), state],
        out_shape=[jax.ShapeDtypeStruct((B, T, GROUP_W), F32),
                   jax.ShapeDtypeStruct((B, N_HEADS, HEAD_D, HEAD_D), F32)],
        scratch_shapes=[pltpu.VMEM((bb, 8 + L, N_GDN_QKV), F32), pltpu.VMEM((bb, N_HEADS, HEAD_D, HEAD_D), F32),
                        pltpu.VMEM((bb, L, GROUP_W), F32)],
        compiler_params=_params(("parallel", "arbitrary")), name="gdn",
    )(qkv, gz, conv0, s0, conv_w, alog, dtb, ng, hsum)


def _s5_kernel(u_ref, h0r_ref, h0i_ref, lr_ref, li_ref, br_ref, bi_ref, cr_ref, ci_ref, d_ref, wg_ref,
               o_ref, hr_out_ref, hi_out_ref, hr_scr, hi_scr, xr_scr, xi_scr, *, tt, nb, nt):
    ti = pl.program_id(0)

    @pl.when(ti == 0)
    def _():
        hr_scr[...] = h0r_ref[...]
        hi_scr[...] = h0i_ref[...]

    u = u_ref[...]
    ub = u.astype(BF16)
    xr_scr[...] = _dot(ub, br_ref[...])
    xi_scr[...] = _dot(ub, bi_ref[...])
    lr = jnp.broadcast_to(lr_ref[...], (nb, S5_STATE))
    li = jnp.broadcast_to(li_ref[...], (nb, S5_STATE))

    def step(t, carry):
        hr, hi = carry
        r0 = pl.multiple_of(t * nb, nb)
        nr = lr * hr - li * hi + xr_scr[pl.ds(r0, nb), :]
        ni = lr * hi + li * hr + xi_scr[pl.ds(r0, nb), :]
        xr_scr[pl.ds(r0, nb), :] = nr
        xi_scr[pl.ds(r0, nb), :] = ni
        return nr, ni

    hr, hi = lax.fori_loop(0, tt, step, (hr_scr[...], hi_scr[...]), unroll=8)
    hr_scr[...] = hr
    hi_scr[...] = hi
    y = _dot(xr_scr[...].astype(BF16), cr_ref[...]) - _dot(xi_scr[...].astype(BF16), ci_ref[...]) + d_ref[...] * u
    act = jax.nn.gelu(y)
    o_ref[...] = act * jax.nn.sigmoid(_dot(act.astype(BF16), wg_ref[...]))

    @pl.when(ti == nt - 1)
    def _():
        hr_out_ref[...] = hr
        hi_out_ref[...] = hi


def _prep_s5_weights(lam_re, lam_im, log_step, b_re, b_im, c_re, c_im, d, w_glu):
    dt = jnp.exp(log_step)[:, None]
    mag = jnp.exp(lam_re * dt)
    bar_re, bar_im = mag * jnp.cos(lam_im * dt), mag * jnp.sin(lam_im * dt)
    den = lam_re * lam_re + lam_im * lam_im
    f_re = ((bar_re - 1.0) * lam_re + bar_im * lam_im) / den
    f_im = (bar_im * lam_re - (bar_re - 1.0) * lam_im) / den
    bb_re = f_re[..., None] * b_re - f_im[..., None] * b_im
    bb_im = f_re[..., None] * b_im + f_im[..., None] * b_re
    eye = jnp.eye(S5_G, dtype=F32)

    def in_mat(b):
        return jnp.einsum('gpc,gh->gchp', b, eye).reshape(S5_G * S5_CG, S5_STATE).astype(BF16)

    def out_mat(c):
        return jnp.einsum('gcp,gh->gphc', c, eye).reshape(S5_STATE, S5_G * S5_CG).astype(BF16)

    return (bar_re.reshape(1, S5_STATE), bar_im.reshape(1, S5_STATE),
            in_mat(bb_re), in_mat(bb_im), out_mat(c_re), out_mat(c_im),
            d.reshape(1, GROUP_W), w_glu.astype(BF16))


def _s5(u_tm, h0_re, h0_im, weights, B, tt):
    T = u_tm.shape[0]
    u2 = u_tm.reshape(T * B, GROUP_W)
    nt = T // tt
    lr, li, br, bi, cr, ci, d, wg = weights
    rows = pl.BlockSpec((tt * B, GROUP_W), lambda i: (i, 0))
    st = _full_spec((B, S5_STATE))
    kern = functools.partial(_s5_kernel, tt=tt, nb=B, nt=nt)
    o, hr, hi = pl.pallas_call(
        kern, grid=(nt,),
        in_specs=[rows, st, st] + [_full_spec(w.shape) for w in weights],
        out_specs=[rows, st, st],
        out_shape=[jax.ShapeDtypeStruct((T * B, GROUP_W), F32), jax.ShapeDtypeStruct((B, S5_STATE), F32),
                   jax.ShapeDtypeStruct((B, S5_STATE), F32)],
        scratch_shapes=[pltpu.VMEM((B, S5_STATE), F32), pltpu.VMEM((B, S5_STATE), F32),
                        pltpu.VMEM((tt * B, S5_STATE), F32), pltpu.VMEM((tt * B, S5_STATE), F32)],
        compiler_params=_params(("arbitrary",)), name="s5",
    )(u2, h0_re.reshape(B, S5_STATE), h0_im.reshape(B, S5_STATE), *weights)
    return o.reshape(T, B * GROUP_W), hr.reshape(B, S5_G, S5_P), hi.reshape(B, S5_G, S5_P)


def _outproj_kernel(oa_ref, ob_ref, oc_ref, od_ref, x_ref, gn_ref, w_ref, g_ref, b_ref, o_ref, *, alpha):
    gn = gn_ref[...]
    mix = jnp.concatenate([
        _rms_rows(oa_ref[...], gn[0:1]).astype(BF16),
        _rms_rows(ob_ref[...], gn[1:2]).astype(BF16),
        oc_ref[...].astype(BF16),
        _rms_rows(od_ref[...], gn[2:3]).astype(BF16)], -1)
    y = alpha * x_ref[...] + _dot(mix, w_ref[...])
    o_ref[...] = _ln_rows(y, g_ref[...], b_ref[...])


def _outproj(oa, ob, oc, od_tm, x, gn, w_out, g, b, alpha, tm):
    B, T, D = x.shape

    def row(f):
        return pl.BlockSpec((None, tm, f), lambda bi, i: (bi, i, 0))

    kern = functools.partial(_outproj_kernel, alpha=alpha)
    return pl.pallas_call(
        kern, grid=(B, T // tm),
        in_specs=[row(GROUP_W), row(GROUP_W), row(GROUP_W), pl.BlockSpec((tm, GROUP_W), lambda bi, i: (i, bi)),
                  row(D), _full_spec(gn.shape), _full_spec(w_out.shape), _full_spec((1, D)), _full_spec((1, D))],
        out_specs=row(D), out_shape=jax.ShapeDtypeStruct((B, T, D), F32),
        compiler_params=_params(("parallel", "parallel")), name="outproj",
    )(oa, ob, oc, od_tm, x, gn, w_out, g.reshape(1, D), b.reshape(1, D))


def _ffn_kernel(x_ref, wg_ref, wu_ref, wd_ref, g_ref, b_ref, o_ref, *, alpha):
    x = x_ref[...]
    xb = x.astype(BF16)
    hid = (jax.nn.silu(_dot(xb, wg_ref[...])) * _dot(xb, wu_ref[...])).astype(BF16)
    y = alpha * x + _dot(hid, wd_ref[...])
    o_ref[...] = _ln_rows(y, g_ref[...], b_ref[...])


def _ffn(x, wg, wu, wd, g, b, alpha, tm):
    B, T, D = x.shape
    row = pl.BlockSpec((None, tm, D), lambda bi, i: (bi, i, 0))

    def resident(shape):
        return pl.BlockSpec(shape, lambda *_: (0,) * len(shape), pipeline_mode=pl.Buffered(1))

    kern = functools.partial(_ffn_kernel, alpha=alpha)
    return pl.pallas_call(
        kern, grid=(B, T // tm),
        in_specs=[row, resident(wg.shape), resident(wu.shape), resident(wd.shape),
                  _full_spec((1, D)), _full_spec((1, D))],
        out_specs=row, out_shape=jax.ShapeDtypeStruct((B, T, D), F32),
        compiler_params=_params(("parallel", "parallel")), name="ffn",
    )(x, wg, wu, wd, g.reshape(1, D), b.reshape(1, D))


def _router_kernel(x_ref, wr_ref, br_ref, idx_ref, gate_ref):
    logits = _dot_nt(wr_ref[...], x_ref[...], HIGHEST) + br_ref[...]
    eid = lax.broadcasted_iota(jnp.int32, logits.shape, 0)
    m1 = jnp.max(logits, 0, keepdims=True)
    i1 = jnp.min(jnp.where(logits == m1, eid, N_EXP), 0, keepdims=True)
    rest = jnp.where(eid == i1, -jnp.inf, logits)
    m2 = jnp.max(rest, 0, keepdims=True)
    i2 = jnp.min(jnp.where(rest == m2, eid, N_EXP), 0, keepdims=True)
    e2 = jnp.exp(m2 - m1)
    den = 1.0 + e2
    idx_ref[...] = jnp.concatenate([i1, i2], 0)
    gate_ref[...] = jnp.concatenate([1.0 / den, e2 / den], 0)


def _router(x2, w_router, b_router, tm):
    N, D = x2.shape
    return pl.pallas_call(
        _router_kernel, grid=(N // tm,),
        in_specs=[pl.BlockSpec((tm, D), lambda i: (i, 0)), _full_spec((N_EXP, D)), _full_spec((N_EXP, 1))],
        out_specs=[pl.BlockSpec((TOP_K, tm), lambda i: (0, i)), pl.BlockSpec((TOP_K, tm), lambda i: (0, i))],
        out_shape=[jax.ShapeDtypeStruct((TOP_K, N), jnp.int32), jax.ShapeDtypeStruct((TOP_K, N), F32)],
        compiler_params=_params(("parallel",)), name="moe_router",
    )(x2, w_router.T, b_router.reshape(N_EXP, 1))


def _expert_kernel(blk_exp_ref, src_ref, dst_ref, x_hbm, wg_ref, wu_ref, wd_ref, out_hbm,
                   xbuf, ybuf, gsem, ssem):
    del blk_exp_ref

    def for_rows(fn):
        def body(r, c):
            fn(r)
            return c
        lax.fori_loop(0, MOE_BLK, body, 0, unroll=MOE_DMA_UNROLL)

    for_rows(lambda r: pltpu.make_async_copy(x_hbm.at[pl.ds(src_ref[0, 0, r], 1)], xbuf.at[pl.ds(r, 1)],
                                             gsem).start())
    pltpu.make_async_copy(x_hbm.at[pl.ds(0, MOE_BLK)], xbuf, gsem).wait()
    xb = xbuf[...].astype(BF16)
    hid = (jax.nn.silu(_dot(xb, wg_ref[...])) * _dot(xb, wu_ref[...])).astype(BF16)
    ybuf[...] = _dot(hid, wd_ref[...])
    for_rows(lambda r: pltpu.make_async_copy(ybuf.at[pl.ds(r, 1)], out_hbm.at[pl.ds(dst_ref[0, 0, r], 1)],
                                             ssem).start())
    pltpu.make_async_copy(ybuf, out_hbm.at[pl.ds(0, MOE_BLK)], ssem).wait()


def _combine_kernel(x_ref, y0_ref, y1_ref, gate_ref, g_ref, b_ref, o_ref, *, alpha):
    gate = gate_ref[...]
    f = y0_ref[...] * gate[:, 0:1] + y1_ref[...] * gate[:, 1:2]
    o_ref[...] = _ln_rows(alpha * x_ref[...] + f, g_ref[...], b_ref[...])


def _moe(x, w_router, b_router, wg, wu, wd, g, b, alpha):
    B, T, D = x.shape
    N = B * T
    x2 = x.reshape(N, D)
    tm = math.gcd(N, MOE_BLK)
    idx, gates = _router(x2, w_router, b_router, tm)

    n_slots = N * TOP_K
    e_flat = idx.T.reshape(-1)
    onehot = (e_flat[:, None] == jnp.arange(N_EXP, dtype=jnp.int32)[None, :]).astype(jnp.int32)
    csum = jnp.cumsum(onehot, 0)
    counts = csum[-1]
    rank = jnp.sum((csum - onehot) * onehot, 1)
    padded = (counts + MOE_BLK - 1) // MOE_BLK * MOE_BLK
    pad_end = jnp.cumsum(padded)
    pad_start = pad_end - padded
    dest = pad_start[e_flat] + rank
    n_blk = -(-(n_slots + N_EXP * (MOE_BLK - 1)) // MOE_BLK)
    rows = n_blk * MOE_BLK
    row_f = jnp.full((rows,), n_slots, jnp.int32).at[dest].set(jnp.arange(n_slots, dtype=jnp.int32))
    row_src = jnp.minimum(row_f // TOP_K, N - 1)
    row_dst = jnp.where(row_f < n_slots, (row_f % TOP_K) * N + row_f // TOP_K,
                        n_slots + jnp.arange(rows, dtype=jnp.int32) % MOE_BLK)
    blk_start = jnp.arange(n_blk, dtype=jnp.int32) * MOE_BLK
    blk_exp = jnp.minimum(jnp.sum((pad_end[None, :] <= blk_start[:, None]).astype(jnp.int32), 1), N_EXP - 1)

    idx_spec = pl.BlockSpec((1, 1, MOE_BLK), lambda i, be: (i, 0, 0), memory_space=pltpu.SMEM)
    y = pl.pallas_call(
        _expert_kernel,
        grid_spec=pltpu.PrefetchScalarGridSpec(
            num_scalar_prefetch=1, grid=(n_blk,),
            in_specs=[idx_spec, idx_spec,
                      pl.BlockSpec(memory_space=pl.ANY),
                      pl.BlockSpec((None, D, D_FF_EXP), lambda i, be: (be[i], 0, 0)),
                      pl.BlockSpec((None, D, D_FF_EXP), lambda i, be: (be[i], 0, 0)),
                      pl.BlockSpec((None, D_FF_EXP, D), lambda i, be: (be[i], 0, 0))],
            out_specs=pl.BlockSpec(memory_space=pl.ANY),
            scratch_shapes=[pltpu.VMEM((MOE_BLK, D), F32), pltpu.VMEM((MOE_BLK, D), F32),
                            pltpu.SemaphoreType.DMA(()), pltpu.SemaphoreType.DMA(())]),
        out_shape=jax.ShapeDtypeStruct((n_slots + MOE_BLK, D), F32),
        compiler_params=_params(("arbitrary",)), name="moe_experts",
    )(blk_exp, row_src.reshape(n_blk, 1, MOE_BLK), row_dst.reshape(n_blk, 1, MOE_BLK), x2, wg, wu, wd)

    kern = functools.partial(_combine_kernel, alpha=alpha)
    rows_spec = pl.BlockSpec((tm, D), lambda i: (i, 0))
    out = pl.pallas_call(
        kern, grid=(N // tm,),
        in_specs=[rows_spec, rows_spec, pl.BlockSpec((tm, D), lambda i: (i + N // tm, 0)),
                  pl.BlockSpec((tm, TOP_K), lambda i: (i, 0)), _full_spec((1, D)), _full_spec((1, D))],
        out_specs=rows_spec, out_shape=jax.ShapeDtypeStruct((N, D), F32),
        compiler_params=_params(("parallel",)), name="moe_combine",
    )(x2, y, y, gates.T, g.reshape(1, D), b.reshape(1, D))
    return out.reshape(B, T, D)


def _round_up(n, m):
    return -(-n // m) * m


def _run_trunk(x, states, P, depth):
    lat_c, kr_c, sbk_c, sbv_c, conv_c, ssm_c, re_c, im_c = states
    B, T, D = x.shape
    past = lat_c.shape[2]
    alpha = (2 * depth) ** 0.25
    long_seq = T >= 512
    tm = 512 if long_seq else T
    tq = ATTN_TQ if long_seq else T
    tk = ATTN_TK if long_seq else LANES
    S = past + T
    S_pad = _round_up(S, tk)

    tab = _rope_table(past, T)
    x = _layer_norm(x, P['ln_in_g'], P['ln_in_b'], tm)
    outs = [[] for _ in range(8)]
    for l in range(depth):
        pm, sbq, sbk, sbv, sbk_b, sbv_b, gq, gz, s5u = _inproj(x, _prep_inproj_weights(P['w_in'][l]), tm)

        qn, kvn, wq, wqr, wk, wv, e = _prep_mla_weights(P['mla_q_norm'][l], P['mla_kv_norm'][l],
                                                        P['mla_w_uq'][l], P['mla_w_ukv'][l])
        q_a, lat_n, kr_n, krp_n = _mla_prep(pm, tab, qn, kvn, wq, wqr, tm)
        kr_cache = jnp.pad(kr_c[l], ((0, 0), (0, 0), (0, LANES - ROPE)))
        lat_all = jnp.pad(jnp.concatenate([lat_c[l], lat_n], 1), ((0, 0), (0, S_pad - S), (0, 0)))
        krp_all = jnp.pad(jnp.concatenate([kr_cache, krp_n], 1), ((0, 0), (0, S_pad - S), (0, 0)))
        k_a, v_a = _mla_kv(lat_all, krp_all, wk, wv, e, tk)
        o_a = _mla_attn(q_a, k_a, v_a, past, tq, tk)

        if past or S_pad != S:
            k_all = jnp.pad(jnp.concatenate([sbk_c[l].reshape(B, past, GROUP_W).astype(BF16), sbk_b], 1),
                            ((0, 0), (0, S_pad - S), (0, 0)))
            v_all = jnp.pad(jnp.concatenate([sbv_c[l].reshape(B, past, GROUP_W).astype(BF16), sbv_b], 1),
                            ((0, 0), (0, S_pad - S), (0, 0)))
        else:
            k_all, v_all = sbk_b, sbv_b
        o_b = _sb_attn(sbq, k_all, v_all, past, tq, tk)

        o_c, ssm_n = _gdn(gq, gz, conv_c[l], ssm_c[l], P['gdn_conv_w'][l], P['gdn_a_log'][l],
                          P['gdn_dt_bias'][l], P['gdn_norm'][l])
        conv_n = jnp.concatenate([conv_c[l], gq], 1)[:, T:]

        s5w = _prep_s5_weights(P['s5_lam_re'][l], P['s5_lam_im'][l], P['s5_log_step'][l], P['s5_b_re'][l],
                               P['s5_b_im'][l], P['s5_c_re'][l], P['s5_c_im'][l], P['s5_d'][l], P['s5_w_glu'][l])
        o_d, re_n, im_n = _s5(s5u, re_c[l], im_c[l], s5w, B, min(64, T))

        x = _outproj(o_a, o_b, o_c, o_d, x, P['grp_norm'][l], P['w_out'][l].astype(BF16),
                     P['ln1_g'][l], P['ln1_b'][l], alpha, tm)
        i = l // 2
        if l % 2 == 0:
            x = _ffn(x, P['ffn_w_gate'][i].astype(BF16), P['ffn_w_up'][i].astype(BF16),
                     P['ffn_w_down'][i].astype(BF16), P['ln2_g'][l], P['ln2_b'][l], alpha, min(tm, 256))
        else:
            x = _moe(x, P['moe_w_router'][i], P['moe_b_router'][i], P['moe_w_gate'][i].astype(BF16),
                     P['moe_w_up'][i].astype(BF16), P['moe_w_down'][i].astype(BF16),
                     P['ln2_g'][l], P['ln2_b'][l], alpha)
        news = (lat_n, kr_n, sbk.reshape(B, T, N_HEADS, HEAD_D), sbv.reshape(B, T, N_HEADS, HEAD_D),
                conv_n, ssm_n, re_n, im_n)
        for lst, a in zip(outs, news):
            lst.append(a)
    return x, [jnp.stack(lst) for lst in outs]


def kernel(x_prompt, x_sample, cache_mla_latent, cache_mla_krope, cache_sb_k, cache_sb_v,
           state_gdn_conv, state_gdn_ssm, state_s5_re, state_s5_im,
           ln_in_g, ln_in_b, w_in, mla_q_norm, mla_kv_norm, mla_w_uq, mla_w_ukv,
           gdn_conv_w, gdn_a_log, gdn_dt_bias, gdn_norm,
           s5_lam_re, s5_lam_im, s5_log_step, s5_b_re, s5_b_im, s5_c_re, s5_c_im, s5_d, s5_w_glu,
           grp_norm, w_out, ln1_g, ln1_b, ln2_g, ln2_b,
           ffn_w_gate, ffn_w_up, ffn_w_down,
           moe_w_router, moe_b_router, moe_w_gate, moe_w_up, moe_w_down):
    P = dict(ln_in_g=ln_in_g, ln_in_b=ln_in_b, w_in=w_in, mla_q_norm=mla_q_norm, mla_kv_norm=mla_kv_norm,
             mla_w_uq=mla_w_uq, mla_w_ukv=mla_w_ukv, gdn_conv_w=gdn_conv_w, gdn_a_log=gdn_a_log,
             gdn_dt_bias=gdn_dt_bias, gdn_norm=gdn_norm, s5_lam_re=s5_lam_re, s5_lam_im=s5_lam_im,
             s5_log_step=s5_log_step, s5_b_re=s5_b_re, s5_b_im=s5_b_im, s5_c_re=s5_c_re, s5_c_im=s5_c_im,
             s5_d=s5_d, s5_w_glu=s5_w_glu, grp_norm=grp_norm, w_out=w_out, ln1_g=ln1_g, ln1_b=ln1_b,
             ln2_g=ln2_g, ln2_b=ln2_b, ffn_w_gate=ffn_w_gate, ffn_w_up=ffn_w_up, ffn_w_down=ffn_w_down,
             moe_w_router=moe_w_router, moe_b_router=moe_b_router, moe_w_gate=moe_w_gate,
             moe_w_up=moe_w_up, moe_w_down=moe_w_down)
    depth = w_in.shape[0]
    Bp = x_prompt.shape[0]
    dt = x_prompt.dtype
    prompt_states = (jnp.zeros((depth, Bp, 0, KV_LORA), dt), jnp.zeros((depth, Bp, 0, ROPE), dt),
                     jnp.zeros((depth, Bp, 0, N_HEADS, HEAD_D), dt), jnp.zeros((depth, Bp, 0, N_HEADS, HEAD_D), dt),
                     jnp.zeros((depth, Bp, CONV_W - 1, N_GDN_QKV), dt),
                     jnp.zeros((depth, Bp, N_HEADS, HEAD_D, HEAD_D), F32),
                     jnp.zeros((depth, Bp, S5_G, S5_P), F32), jnp.zeros((depth, Bp, S5_G, S5_P), F32))
    y_prompt, st_p = _run_trunk(x_prompt, prompt_states, P, depth)
    sample_states = (cache_mla_latent, cache_mla_krope, cache_sb_k, cache_sb_v,
                     state_gdn_conv, state_gdn_ssm, state_s5_re, state_s5_im)
    y_sample, st_s = _run_trunk(x_sample, sample_states, P, depth)
    return (y_prompt, y_sample, *st_p, *st_s)
```

```python
import functools
import math

import jax
import jax.numpy as jnp
from jax import lax
from jax.experimental import pallas as pl
from jax.experimental.pallas import tpu as pltpu

F32 = jnp.float32
BF16 = jnp.bfloat16
HIGHEST = lax.Precision.HIGHEST

D_MODEL = 1024
CHUNK = 64
GROUP_W = 256
N_HEADS = 4
HEAD_D = 64
NOPE = 64
ROPE = 32
Q_LORA = 192
KV_LORA = 128
ROPE_THETA = 10000.0
MLA_SCALE = (NOPE + ROPE) ** -0.5
SB_SCALE = HEAD_D ** -0.5
LOG2E = math.log2(math.e)
CONV_W = 4
N_GDN_QKV = 768
S5_CG = 16
S5_G = 16
S5_P = 64
S5_STATE = S5_G * S5_P
D_FF = 2816
N_EXP = 8
TOP_K = 2
D_FF_EXP = 1792
MOE_BLK = 512
LN_EPS = 1e-5
RMS_EPS = 1e-6
NEG_INF = -1e30
OFF_SB = Q_LORA + KV_LORA + ROPE
OFF_GDN = OFF_SB + 3 * GROUP_W
OFF_S5 = OFF_GDN + N_GDN_QKV + 2 * N_HEADS + GROUP_W

LANES = 128
HEAD_PAD = LANES
VMEM_LIMIT = 48 * 1024 * 1024
GDN_BATCH_PER_STEP = 8
MOE_SUB = 128
ATTN_TQ = 512
ATTN_TK = 512
MLA_TILE = (256, 512)
SB_TILE = (512, 256)


def _params(sem):
    return pltpu.CompilerParams(dimension_semantics=sem, vmem_limit_bytes=VMEM_LIMIT)


def _dot(a, b, precision=None):
    return jnp.dot(a, b, preferred_element_type=F32, precision=precision)


def _dot_nt(a, b, precision=None):
    return lax.dot_general(a, b, (((1,), (1,)), ((), ())), preferred_element_type=F32, precision=precision)


def _dot_tn(a, b, precision=None):
    return lax.dot_general(a, b, (((0,), (0,)), ((), ())), preferred_element_type=F32, precision=precision)


def _ln_rows(x, g, b):
    mu = jnp.mean(x, -1, keepdims=True)
    xc = x - mu
    var = jnp.mean(xc * xc, -1, keepdims=True)
    return xc * lax.rsqrt(var + LN_EPS) * g + b


def _rms_rows(x, g, n=None):
    n = x.shape[-1] if n is None else n
    ms = jnp.sum(x * x, -1, keepdims=True) * (1.0 / n)
    return x * lax.rsqrt(ms + RMS_EPS) * g


def _full_spec(shape):
    nd = len(shape)
    return pl.BlockSpec(shape, lambda *_: (0,) * nd)


def _ln_kernel(x_ref, g_ref, b_ref, o_ref):
    o_ref[...] = _ln_rows(x_ref[...], g_ref[...], b_ref[...])


def _layer_norm(x, g, b, tm):
    B, T, D = x.shape
    row = pl.BlockSpec((None, tm, D), lambda bi, i: (bi, i, 0))
    return pl.pallas_call(
        _ln_kernel, grid=(B, T // tm),
        in_specs=[row, _full_spec((1, D)), _full_spec((1, D))],
        out_specs=row, out_shape=jax.ShapeDtypeStruct((B, T, D), F32),
        compiler_params=_params(("parallel", "parallel")), name="ln_in",
    )(x, g.reshape(1, D), b.reshape(1, D))


MLA_IN_PAD = 640
GZ_PAD = 384


def _inproj_kernel(x_ref, wm_ref, wsb_ref, wgq_ref, wgz_ref, ws5_ref,
                   pm_ref, sbq_ref, sbk_ref, sbv_ref, sbkb_ref, sbvb_ref, gq_ref, gz_ref, s5_ref):
    xb = x_ref[...].astype(BF16)
    pm_ref[...] = _dot(xb, wm_ref[...])
    sb = _dot(xb, wsb_ref[...])
    sbq_ref[...] = (sb[:, :GROUP_W] * (SB_SCALE * LOG2E)).astype(BF16)
    sbk_ref[...] = sb[:, GROUP_W:2 * GROUP_W]
    sbv_ref[...] = sb[:, 2 * GROUP_W:]
    sbkb_ref[...] = sb[:, GROUP_W:2 * GROUP_W].astype(BF16)
    sbvb_ref[...] = sb[:, 2 * GROUP_W:].astype(BF16)
    gq_ref[...] = _dot(xb, wgq_ref[...])
    gz_ref[...] = _dot(xb, wgz_ref[...])
    s5_ref[...] = _dot(xb, ws5_ref[...])


def _rotate_half_cols(w):
    half = w.shape[-1] // 2
    return jnp.concatenate([-w[..., half:], w[..., :half]], -1)


def _prep_inproj_weights(w_in):
    zeros = lambda n: jnp.zeros((D_MODEL, n), F32)
    w_cq = w_in[:, :Q_LORA]
    w_ckv = w_in[:, Q_LORA:Q_LORA + KV_LORA]
    w_kr = w_in[:, Q_LORA + KV_LORA:OFF_SB]
    wm = jnp.concatenate([w_ckv, w_cq, zeros(64), w_kr, zeros(96), _rotate_half_cols(w_kr), zeros(96)], 1)
    wsb = w_in[:, OFF_SB:OFF_GDN]
    wgq = w_in[:, OFF_GDN:OFF_GDN + N_GDN_QKV]
    o_a = OFF_GDN + N_GDN_QKV
    wgz = jnp.concatenate([w_in[:, o_a + 2 * N_HEADS:OFF_S5], w_in[:, o_a:o_a + 2 * N_HEADS],
                           zeros(GZ_PAD - GROUP_W - 2 * N_HEADS)], 1)
    ws5 = w_in[:, OFF_S5:]
    return tuple(w.astype(BF16) for w in (wm, wsb, wgq, wgz, ws5))


def _inproj(x, weights, tm):
    B, T, D = x.shape
    wm, wsb, wgq, wgz, ws5 = weights

    def row(f):
        return pl.BlockSpec((None, tm, f), lambda bi, i: (bi, i, 0))

    def out(f, dt=F32):
        return jax.ShapeDtypeStruct((B, T, f), dt)

    return pl.pallas_call(
        _inproj_kernel, grid=(B, T // tm),
        in_specs=[row(D)] + [_full_spec(w.shape) for w in weights],
        out_specs=[row(MLA_IN_PAD), row(GROUP_W), row(GROUP_W), row(GROUP_W), row(GROUP_W), row(GROUP_W),
                   row(N_GDN_QKV), row(GZ_PAD), pl.BlockSpec((tm, GROUP_W), lambda bi, i: (i, bi))],
        out_shape=[out(MLA_IN_PAD), out(GROUP_W, BF16), out(GROUP_W), out(GROUP_W), out(GROUP_W, BF16),
                   out(GROUP_W, BF16), out(N_GDN_QKV), out(GZ_PAD),
                   jax.ShapeDtypeStruct((T, B * GROUP_W), F32)],
        compiler_params=_params(("parallel", "parallel")), name="inproj",
    )(x, *weights)


def _mla_prep_kernel(pm_ref, tab_ref, qn_ref, kvn_ref, wq_ref, wqr_ref, q_ref, lat_ref, kr_ref, krp_ref):
    pm = pm_ref[...]
    tab = tab_ref[...]
    c_kv = pm[:, :KV_LORA]
    c_q = pm[:, KV_LORA:KV_LORA + 2 * LANES]
    k_r = pm[:, 3 * LANES:4 * LANES]
    k_rr = pm[:, 4 * LANES:5 * LANES]
    lat_ref[...] = _rms_rows(c_kv, kvn_ref[...])
    nq = _rms_rows(c_q, qn_ref[...], n=Q_LORA).astype(BF16)
    qp = _dot(nq, wq_ref[...])
    qr = _dot(nq, wqr_ref[...])
    cq, sq = tab[:, :LANES], tab[:, LANES:2 * LANES]
    for h in range(N_HEADS):
        sl = slice(h * HEAD_PAD, (h + 1) * HEAD_PAD)
        q_ref[:, sl] = ((qp[:, sl] * cq + qr[:, sl] * sq) * (MLA_SCALE * LOG2E)).astype(BF16)
    kr_new = k_r * tab[:, 2 * LANES:3 * LANES] + k_rr * tab[:, 3 * LANES:]
    krp_ref[...] = kr_new
    kr_ref[...] = kr_new[:, :ROPE]


def _rope_table(past, T):
    half = ROPE // 2
    inv = ROPE_THETA ** (-jnp.arange(half, dtype=F32) / half)
    ang = (past + jnp.arange(T, dtype=jnp.int32)).astype(F32)[:, None] * inv
    c, s = jnp.cos(ang), jnp.sin(ang)
    one, zero = jnp.ones((T, NOPE), F32), jnp.zeros((T, NOPE), F32)
    z32, z96 = jnp.zeros((T, 32), F32), jnp.zeros((T, 96), F32)
    return jnp.concatenate([one, c, c, z32, zero, s, s, z32, c, c, z96, s, s, z96], 1)


def _prep_mla_weights(q_norm, kv_norm, w_uq, w_ukv):
    w3 = w_uq.reshape(Q_LORA, N_HEADS, NOPE + ROPE)
    zq = jnp.zeros((Q_LORA, N_HEADS, HEAD_PAD - NOPE - ROPE), F32)
    wq = jnp.concatenate([w3, zq], -1).reshape(Q_LORA, N_HEADS * HEAD_PAD)
    w3r = jnp.concatenate([jnp.zeros((Q_LORA, N_HEADS, NOPE), F32), _rotate_half_cols(w3[..., NOPE:]), zq], -1)
    wqr = w3r.reshape(Q_LORA, N_HEADS * HEAD_PAD)
    pad_rows = jnp.zeros((2 * LANES - Q_LORA, N_HEADS * HEAD_PAD), F32)
    wq = jnp.concatenate([wq, pad_rows], 0).astype(BF16)
    wqr = jnp.concatenate([wqr, pad_rows], 0).astype(BF16)
    qn = jnp.concatenate([q_norm, jnp.zeros((2 * LANES - Q_LORA,), F32)]).reshape(1, 2 * LANES)
    kvn = kv_norm.reshape(1, KV_LORA)
    kv3 = w_ukv.reshape(KV_LORA, N_HEADS, NOPE + HEAD_D)
    zk = jnp.zeros((KV_LORA, N_HEADS, HEAD_PAD - NOPE), F32)
    wk = jnp.concatenate([kv3[..., :NOPE], zk], -1).reshape(KV_LORA, N_HEADS * HEAD_PAD).astype(BF16)
    wv = kv3[..., NOPE:].reshape(KV_LORA, GROUP_W).astype(BF16)
    e = jnp.zeros((LANES, N_HEADS, HEAD_PAD), F32)
    e = e.at[jnp.arange(ROPE)[:, None], jnp.arange(N_HEADS)[None, :], NOPE + jnp.arange(ROPE)[:, None]].set(1.0)
    e = e.reshape(LANES, N_HEADS * HEAD_PAD).astype(BF16)
    return qn, kvn, wq, wqr, wk, wv, e


def _mla_prep(pm, tab, qn, kvn, wq, wqr, tm):
    B, T, _ = pm.shape

    def row(f):
        return pl.BlockSpec((None, tm, f), lambda bi, i: (bi, i, 0))

    return pl.pallas_call(
        _mla_prep_kernel, grid=(B, T // tm),
        in_specs=[row(MLA_IN_PAD), pl.BlockSpec((tm, 4 * LANES), lambda bi, i: (i, 0)),
                  _full_spec(qn.shape), _full_spec(kvn.shape), _full_spec(wq.shape), _full_spec(wqr.shape)],
        out_specs=[row(N_HEADS * HEAD_PAD), row(KV_LORA), row(ROPE), row(LANES)],
        out_shape=[jax.ShapeDtypeStruct((B, T, N_HEADS * HEAD_PAD), BF16),
                   jax.ShapeDtypeStruct((B, T, KV_LORA), F32),
                   jax.ShapeDtypeStruct((B, T, ROPE), F32),
                   jax.ShapeDtypeStruct((B, T, LANES), F32)],
        compiler_params=_params(("parallel", "parallel")), name="mla_prep",
    )(pm, tab, qn, kvn, wq, wqr)


def _mla_kv_kernel(lat_ref, krp_ref, wk_ref, wv_ref, e_ref, k_ref, v_ref):
    lb = lat_ref[...].astype(BF16)
    kb = krp_ref[...].astype(BF16)
    k_ref[...] = (_dot(lb, wk_ref[...]) + _dot(kb, e_ref[...])).astype(BF16)
    v_ref[...] = _dot(lb, wv_ref[...]).astype(BF16)


def _mla_kv(lat, krp, wk, wv, e, tm):
    B, S, _ = lat.shape

    def row(f):
        return pl.BlockSpec((None, tm, f), lambda bi, i: (bi, i, 0))

    return pl.pallas_call(
        _mla_kv_kernel, grid=(B, S // tm),
        in_specs=[row(KV_LORA), row(LANES), _full_spec(wk.shape), _full_spec(wv.shape), _full_spec(e.shape)],
        out_specs=[row(N_HEADS * HEAD_PAD), row(GROUP_W)],
        out_shape=[jax.ShapeDtypeStruct((B, S, N_HEADS * HEAD_PAD), BF16),
                   jax.ShapeDtypeStruct((B, S, GROUP_W), BF16)],
        compiler_params=_params(("parallel", "parallel")), name="mla_kv",
    )(lat, krp, wk, wv, e)


def _lane_tile(a, n):
    return a if n == 1 else jnp.concatenate([a] * n, axis=1)


def _per_head_lanes(stats):
    low = lax.broadcasted_iota(jnp.int32, stats[0].shape, 1) < HEAD_D
    return jnp.concatenate([jnp.where(low, stats[0], stats[1]), jnp.where(low, stats[2], stats[3])], axis=1)


def _block_diag_values(vc):
    head = lax.broadcasted_iota(jnp.int32, vc.shape, 1) // HEAD_D
    return jnp.concatenate([jnp.where(head == h, vc, jnp.zeros_like(vc)) for h in range(N_HEADS)], axis=0)


def _mla_attn_kernel(q_ref, k_ref, v_ref, o_ref, m_scr, l_scr, acc_scr, *, tq, tk, rq, ck, past, nkv):
    qi = pl.program_id(1)
    kj = pl.program_id(2)
    delta = past + qi * tq - kj * tk

    @pl.when(kj == 0)
    def _():
        m_scr[...] = jnp.full(m_scr.shape, NEG_INF, F32)
        l_scr[...] = jnp.zeros(l_scr.shape, F32)
        acc_scr[...] = jnp.zeros(acc_scr.shape, F32)

    def tiles(masked):
        v_bd = [_block_diag_values(v_ref[c * ck:(c + 1) * ck, :]) for c in range(tk // ck)]
        for r in range(tq // rq):
            rows = slice(r * rq, (r + 1) * rq)
            m = [m_scr[h, rows, :] for h in range(N_HEADS)]
            l = [l_scr[h, rows, :] for h in range(N_HEADS)]
            acc = acc_scr[rows, :]
            for c in range(tk // ck):
                keys = slice(c * ck, (c + 1) * ck)
                if masked:
                    q_chunk = (past + qi * tq + r * rq + lax.broadcasted_iota(jnp.int32, (rq, ck), 0)) // CHUNK
                    k_chunk = (kj * tk + c * ck + lax.broadcasted_iota(jnp.int32, (rq, ck), 1)) // CHUNK
                    allowed = k_chunk <= q_chunk
                ps, alphas = [], []
                for h in range(N_HEADS):
                    sl = slice(h * HEAD_PAD, (h + 1) * HEAD_PAD)
                    s = _dot_nt(q_ref[rows, sl], k_ref[keys, sl])
                    if masked:
                        s = jnp.where(allowed, s, NEG_INF)
                    m_new = jnp.maximum(m[h], jnp.max(s, -1, keepdims=True))
                    alpha = jnp.exp2(m[h] - m_new)
                    p = jnp.exp2(s - _lane_tile(m_new, ck // LANES))
                    l[h] = alpha * l[h] + jnp.sum(p, -1, keepdims=True)
                    m[h] = m_new
                    ps.append(p.astype(BF16))
                    alphas.append(alpha)
                acc = _per_head_lanes(alphas) * acc + _dot(jnp.concatenate(ps, axis=1), v_bd[c])
            for h in range(N_HEADS):
                m_scr[h, rows, :] = m[h]
                l_scr[h, rows, :] = l[h]
            acc_scr[rows, :] = acc

    @pl.when(delta >= tk)
    def _():
        tiles(False)

    @pl.when((delta < tk) & (delta + tq > 0))
    def _():
        tiles(True)

    @pl.when(kj == nkv - 1)
    def _():
        o_ref[...] = acc_scr[...] / _per_head_lanes([l_scr[h] for h in range(N_HEADS)])


def _mla_attn(q, k, v, past, tq, tk):
    B, T, W = q.shape
    S = k.shape[1]
    nq, nkv = T // tq, S // tk
    rq, ck = min(MLA_TILE[0], tq), min(MLA_TILE[1], tk)

    def kv_map(bi, i, j):
        last = (past + (i + 1) * tq - 1) // tk
        return (bi, jnp.minimum(j, last), 0)

    kern = functools.partial(_mla_attn_kernel, tq=tq, tk=tk, rq=rq, ck=ck, past=past, nkv=nkv)
    return pl.pallas_call(
        kern, grid=(B, nq, nkv),
        in_specs=[pl.BlockSpec((None, tq, W), lambda bi, i, j: (bi, i, 0)),
                  pl.BlockSpec((None, tk, W), kv_map), pl.BlockSpec((None, tk, GROUP_W), kv_map)],
        out_specs=pl.BlockSpec((None, tq, GROUP_W), lambda bi, i, j: (bi, i, 0)),
        out_shape=jax.ShapeDtypeStruct((B, T, GROUP_W), F32),
        scratch_shapes=[pltpu.VMEM((N_HEADS, tq, LANES), F32), pltpu.VMEM((N_HEADS, tq, LANES), F32),
                        pltpu.VMEM((tq, GROUP_W), F32)],
        compiler_params=_params(("parallel", "parallel", "arbitrary")), name="mla_attn",
    )(q, k, v)


def _sb_attn_kernel(q_ref, k_ref, v_ref, u_ref, o_ref, c_scr, acc_scr, *, tq, tk, rq, ck, past, nkv):
    qi = pl.program_id(1)
    j = pl.program_id(2)
    last = (past + (qi + 1) * tq - 2) // tk
    jb = last - j
    delta = past + qi * tq - jb * tk

    @pl.when(j == 0)
    def _():
        c_scr[...] = jnp.zeros(c_scr.shape, F32)
        acc_scr[...] = jnp.zeros(acc_scr.shape, F32)

    def tiles(masked):
        u2 = u_ref[...]
        v_bd = [_block_diag_values(v_ref[c * ck:(c + 1) * ck, :]) for c in range(tk // ck)]
        head = lax.broadcasted_iota(jnp.int32, (rq, GROUP_W), 1) // HEAD_D
        for r in range(tq // rq):
            rows = slice(r * rq, (r + 1) * rq)
            q_r = q_ref[rows, :]
            q_h = [jnp.where(head == h, q_r, jnp.zeros_like(q_r)) for h in range(N_HEADS)]
            carry = [c_scr[h, rows, :] for h in range(N_HEADS)]
            acc = acc_scr[rows, :]
            for c in reversed(range(tk // ck)):
                k_c = k_ref[c * ck:(c + 1) * ck, :]
                if masked:
                    q_pos = delta + r * rq + lax.broadcasted_iota(jnp.int32, (rq, ck), 0)
                    allowed = (c * ck + lax.broadcasted_iota(jnp.int32, (rq, ck), 1)) < q_pos
                ws = []
                for h in range(N_HEADS):
                    z = _dot_nt(q_h[h], k_c)
                    sp = jnp.log(1.0 + jnp.exp2(-jnp.abs(z))) * LOG2E
                    log_beta = jnp.minimum(z, 0.0) - sp
                    log_1m = log_beta - z
                    if masked:
                        log_1m = jnp.where(allowed, log_1m, 0.0)
                    after = _dot(log_1m.astype(BF16), u2) + _lane_tile(carry[h], ck // LANES)
                    w = jnp.exp2(log_beta + after)
                    if masked:
                        w = jnp.where(allowed, w, 0.0)
                    ws.append(w.astype(BF16))
                    carry[h] = carry[h] + jnp.sum(log_1m, -1, keepdims=True)
                acc = acc + _dot(jnp.concatenate(ws, axis=1), v_bd[c])
            for h in range(N_HEADS):
                c_scr[h, rows, :] = carry[h]
            acc_scr[rows, :] = acc

    @pl.when((delta >= tk) & (jb >= 0))
    def _():
        tiles(False)

    @pl.when((delta < tk) & (jb >= 0))
    def _():
        tiles(True)

    @pl.when(j == nkv - 1)
    def _():
        o_ref[...] = acc_scr[...]


def _sb_attn(q, k, v, past, tq, tk):
    B, T, W = q.shape
    S = k.shape[1]
    nq, nkv = T // tq, S // tk
    rq, ck = min(SB_TILE[0], tq), min(SB_TILE[1], tk)
    later =lax.broadcasted_iota(jnp.int32, (ck, ck), 0) > lax.broadcasted_iota(jnp.int32, (ck, ck), 1)
    u2 = later.astype(BF16)

    def kv_map(bi, i, j):
        last = (past + (i + 1) * tq - 2) // tk
        return (bi, jnp.maximum(last - j, 0), 0)

    kern = functools.partial(_sb_attn_kernel, tq=tq, tk=tk, rq=rq, ck=ck, past=past, nkv=nkv)
    return pl.pallas_call(
        kern, grid=(B, nq, nkv),
        in_specs=[pl.BlockSpec((None, tq, W), lambda bi, i, j: (bi, i, 0)),
                  pl.BlockSpec((None, tk, W), kv_map), pl.BlockSpec((None, tk, W), kv_map),
                  _full_spec((ck, ck))],
        out_specs=pl.BlockSpec((None, tq, GROUP_W), lambda bi, i, j: (bi, i, 0)),
        out_shape=jax.ShapeDtypeStruct((B, T, GROUP_W), F32),
        scratch_shapes=[pltpu.VMEM((N_HEADS, tq, LANES), F32), pltpu.VMEM((tq, GROUP_W), F32)],
        compiler_params=_params(("parallel", "parallel", "arbitrary")), name="sb_attn",
    )(q, k, v, u2)


def _softplus(x):
    return jnp.maximum(x, 0.0) + jnp.log1p(jnp.exp(-jnp.abs(x)))


def _split_bf16(a):
    hi = a.astype(BF16)
    return hi, (a - hi.astype(F32)).astype(BF16)


def _dot_split(a, b):
    a_hi, a_lo = _split_bf16(a)
    b_hi, b_lo = _split_bf16(b)
    m = a.shape[0]
    top = _dot(jnp.concatenate([a_hi, a_lo], 0), b_hi)
    return top[:m] + top[m:] + _dot(a_hi, b_lo)


def _head_sumsq(x, hsum):
    hi, lo = _split_bf16(x * x)
    m = x.shape[0]
    both = _dot(jnp.concatenate([hi, lo], 0), hsum)
    return both[:m] + both[m:]


def _gdn_kernel(qkv_ref, gz_ref, conv0_ref, s0_ref, cw_ref, alog_ref, dtb_ref, ng_ref, hsum_ref,
                o_ref, s_out_ref, xp_scr, s_scr, o_scr, *, L, nt, bb):
    ti = pl.program_id(1)
    halo = CONV_W - 1

    @pl.when(ti == 0)
    def _():
        xp_scr[:, 8 - halo:8, :] = conv0_ref[...]
        s_scr[...] = s0_ref[...]

    hsum = hsum_ref[...]
    ii = lax.broadcasted_iota(jnp.int32, (L, L), 0)
    jj = lax.broadcasted_iota(jnp.int32, (L, L), 1)
    incl = ii >= jj
    strict = ii > jj
    eye = (ii == jj).astype(F32)

    units = []
    for b in range(bb):
        x = qkv_ref[b]
        xp_scr[b, 8:8 + L, :] = x
        conv = jnp.zeros((L, N_GDN_QKV), F32)
        for i in range(CONV_W):
            conv = conv + xp_scr[b, 8 - halo + i:8 - halo + i + L, :] * cw_ref[i:i + 1, :]
        xp_scr[b, 8 - halo:8, :] = x[L - halo:, :]
        conv = jax.nn.silu(conv)
        qa = conv[:, :GROUP_W]
        ka = conv[:, GROUP_W:2 * GROUP_W]
        va = conv[:, 2 * GROUP_W:]
        qa = qa * lax.rsqrt(_head_sumsq(qa, hsum) + 1e-6) * (HEAD_D ** -0.5)
        ka = ka * lax.rsqrt(_head_sumsq(ka, hsum) + 1e-6)
        ab = gz_ref[b, :, GROUP_W:]
        g_all = -jnp.exp(alog_ref[...]) * _softplus(ab + dtb_ref[...])
        beta_all = jax.nn.sigmoid(ab)
        for h in range(N_HEADS):
            hs = slice(h * HEAD_D, (h + 1) * HEAD_D)
            units.append(dict(b=b, h=h, q=qa[:, hs], k=ka[:, hs], v=va[:, hs], g=g_all[:, h:h + 1],
                              beta=beta_all[:, N_HEADS + h:N_HEADS + h + 1]))

    for un in units:
        g_b = jnp.broadcast_to(un['g'], (L, L))
        gc_row = jnp.sum(jnp.where(ii <= jj, g_b, 0.0), 0, keepdims=True)
        gc_col = jnp.sum(jnp.where(ii == jj, jnp.broadcast_to(gc_row, (L, L)), 0.0), 1, keepdims=True)
        un['decay'] = jnp.where(incl, jnp.exp(jnp.where(incl, gc_col - gc_row, 0.0)), 0.0)
        un['e_col'] = jnp.exp(gc_col)
        gc_last = gc_row[:, L - 1:L]
        un['g_last'] = jnp.exp(gc_last)
        un['k_tail'] = jnp.exp(gc_last - gc_col)
        un['kb'] = un['k'] * un['beta']
        un['kbf'] = un['k'].astype(BF16)
    for un in units:
        m = jnp.where(strict, _dot_nt(un['kb'].astype(BF16), un['kbf']) * un['decay'], 0.0)
        un['p'] = -m
        un['t'] = eye - m

    n_sq = int(math.log2(L)) - 1
    for r in range(n_sq + 1):
        for un in units:
            p_hi, p_lo = _split_bf16(un['p'])
            if r == 0:
                lhs_hi, lhs_lo = p_hi, p_lo
            else:
                t_hi, t_lo = _split_bf16(un['t'])
                lhs_hi = jnp.concatenate([p_hi, t_hi], 0) if r < n_sq else t_hi
                lhs_lo = jnp.concatenate([p_lo, t_lo], 0) if r < n_sq else t_lo
            rows = lhs_hi.shape[0]
            top = _dot(jnp.concatenate([lhs_hi, lhs_lo], 0), p_hi)
            prod = top[:rows] + top[rows:] + _dot(lhs_hi, p_lo)
            if r == 0:
                un['p'] = prod
            elif r < n_sq:
                un['p'] = prod[:L]
                un['t'] = un['t'] + prod[L:]
            else:
                un['t'] = un['t'] + prod
    for un in units:
        un['u'] = _dot_split(un['t'], un['v'] * un['beta'])
        un['w'] = _dot_split(un['t'], un['kb'] * un['e_col'])
        un['attn'] = jnp.where(incl, _dot_nt(un['q'].astype(BF16), un['kbf']) * un['decay'], 0.0).astype(BF16)

    for un in units:
        un['S'] = s_scr[un['b'], un['h']]
        un['Sb'] = un['S'].astype(BF16)
    for un in units:
        un['v_new'] = (un['u'] - _dot(un['w'].astype(BF16), un['Sb'])).astype(BF16)
    for un in units:
        hs = slice(un['h'] * HEAD_D, (un['h'] + 1) * HEAD_D)
        o_h = _dot((un['q'] * un['e_col']).astype(BF16), un['Sb']) + _dot(un['attn'], un['v_new'])
        o_scr[un['b'], :, hs] = o_h
        k_dec = (un['k'] * un['k_tail']).astype(BF16)
        s_scr[un['b'], un['h']] = un['S'] * un['g_last'] + _dot_tn(k_dec, un['v_new'])

    for b in range(bb):
        o = o_scr[b]
        ms = _head_sumsq(o, hsum) * (1.0 / HEAD_D)
        o_ref[b] = o * lax.rsqrt(ms + RMS_EPS) * ng_ref[...] * jax.nn.silu(gz_ref[b, :, :GROUP_W])

    @pl.when(ti == nt - 1)
    def _():
        s_out_ref[...] = s_scr[...]


def _gdn(qkv, gz, conv0, s0, conv_w, a_log, dt_bias, norm_g):
    B, T, _ = qkv.shape
    L = min(CHUNK, T)
    nt = T // L
    pad = lambda a: jnp.concatenate([a, jnp.zeros((LANES - a.shape[0],), F32)]).reshape(1, LANES)
    alog = pad(a_log)
    dtb = pad(dt_bias)
    ng = jnp.tile(norm_g, N_HEADS).reshape(1, GROUP_W)
    hid = jnp.arange(GROUP_W) // HEAD_D
    hsum = (hid[:, None] == hid[None, :]).astype(BF16)
    bb = GDN_BATCH_PER_STEP

    def row(f):
        return pl.BlockSpec((bb, L, f), lambda bi, i: (bi, i, 0))

    state = pl.BlockSpec((bb, N_HEADS, HEAD_D, HEAD_D), lambda bi, i: (bi, 0, 0, 0))
    kern = functools.partial(_gdn_kernel, L=L, nt=nt, bb=bb)
    return pl.pallas_call(
        kern, grid=(B // bb, nt),
        in_specs=[row(N_GDN_QKV), row(GZ_PAD),
                  pl.BlockSpec((bb, CONV_W - 1, N_GDN_QKV), lambda bi, i: (bi, 0, 0)), state,
                  _full_spec(conv_w.shape), _full_spec(alog.shape), _full_spec(dtb.shape),
                  _full_spec(ng.shape), _full_spec(hsum.shape)],
        out_specs=[row(GROUP_W), state],
        out_shape=[jax.ShapeDtypeStruct((B, T, GROUP_W), F32),
                   jax.ShapeDtypeStruct((B, N_HEADS, HEAD_D, HEAD_D), F32)],
        scratch_shapes=[pltpu.VMEM((bb, 8 + L, N_GDN_QKV), F32), pltpu.VMEM((bb, N_HEADS, HEAD_D, HEAD_D), F32),
                        pltpu.VMEM((bb, L, GROUP_W), F32)],
        compiler_params=_params(("parallel", "arbitrary")), name="gdn",
    )(qkv, gz, conv0, s0, conv_w, alog, dtb, ng, hsum)


def _s5_kernel(u_ref, h0r_ref, h0i_ref, lr_ref, li_ref, br_ref, bi_ref, cr_ref, ci_ref, d_ref, wg_ref,
               o_ref, hr_out_ref, hi_out_ref, hr_scr, hi_scr, xr_scr, xi_scr, *, tt, nb, nt):
    ti = pl.program_id(0)

    @pl.when(ti == 0)
    def _():
        hr_scr[...] = h0r_ref[...]
        hi_scr[...] = h0i_ref[...]

    u = u_ref[...]
    ub = u.astype(BF16)
    xr_scr[...] = _dot(ub, br_ref[...])
    xi_scr[...] = _dot(ub, bi_ref[...])
    lr = jnp.broadcast_to(lr_ref[...], (nb, S5_STATE))
    li = jnp.broadcast_to(li_ref[...], (nb, S5_STATE))

    def step(t, carry):
        hr, hi = carry
        r0 = pl.multiple_of(t * nb, nb)
        nr = lr * hr - li * hi + xr_scr[pl.ds(r0, nb), :]
        ni = lr * hi + li * hr + xi_scr[pl.ds(r0, nb), :]
        xr_scr[pl.ds(r0, nb), :] = nr
        xi_scr[pl.ds(r0, nb), :] = ni
        return nr, ni

    hr, hi = lax.fori_loop(0, tt, step, (hr_scr[...], hi_scr[...]), unroll=8)
    hr_scr[...] = hr
    hi_scr[...] = hi
    y = _dot(xr_scr[...].astype(BF16), cr_ref[...]) - _dot(xi_scr[...].astype(BF16), ci_ref[...]) + d_ref[...] * u
    act = jax.nn.gelu(y)
    o_ref[...] = act * jax.nn.sigmoid(_dot(act.astype(BF16), wg_ref[...]))

    @pl.when(ti == nt - 1)
    def _():
        hr_out_ref[...] = hr
        hi_out_ref[...] = hi


def _prep_s5_weights(lam_re, lam_im, log_step, b_re, b_im, c_re, c_im, d, w_glu):
    dt = jnp.exp(log_step)[:, None]
    mag = jnp.exp(lam_re * dt)
    bar_re, bar_im = mag * jnp.cos(lam_im * dt), mag * jnp.sin(lam_im * dt)
    den = lam_re * lam_re + lam_im * lam_im
    f_re = ((bar_re - 1.0) * lam_re + bar_im * lam_im) / den
    f_im = (bar_im * lam_re - (bar_re - 1.0) * lam_im) / den
    bb_re = f_re[..., None] * b_re - f_im[..., None] * b_im
    bb_im = f_re[..., None] * b_im + f_im[..., None] * b_re
    eye = jnp.eye(S5_G, dtype=F32)

    def in_mat(b):
        return jnp.einsum('gpc,gh->gchp', b, eye).reshape(S5_G * S5_CG, S5_STATE).astype(BF16)

    def out_mat(c):
        return jnp.einsum('gcp,gh->gphc', c, eye).reshape(S5_STATE, S5_G * S5_CG).astype(BF16)

    return (bar_re.reshape(1, S5_STATE), bar_im.reshape(1, S5_STATE),
            in_mat(bb_re), in_mat(bb_im), out_mat(c_re), out_mat(c_im),
            d.reshape(1, GROUP_W), w_glu.astype(BF16))


def _s5(u_tm, h0_re, h0_im, weights, B, tt):
    T = u_tm.shape[0]
    u2 = u_tm.reshape(T * B, GROUP_W)
    nt = T // tt
    lr, li, br, bi, cr, ci, d, wg = weights
    rows = pl.BlockSpec((tt * B, GROUP_W), lambda i: (i, 0))
    st = _full_spec((B, S5_STATE))
    kern = functools.partial(_s5_kernel, tt=tt, nb=B, nt=nt)
    o, hr, hi = pl.pallas_call(
        kern, grid=(nt,),
        in_specs=[rows, st, st] + [_full_spec(w.shape) for w in weights],
        out_specs=[rows, st, st],
        out_shape=[jax.ShapeDtypeStruct((T * B, GROUP_W), F32), jax.ShapeDtypeStruct((B, S5_STATE), F32),
                   jax.ShapeDtypeStruct((B, S5_STATE), F32)],
        scratch_shapes=[pltpu.VMEM((B, S5_STATE), F32), pltpu.VMEM((B, S5_STATE), F32),
                        pltpu.VMEM((tt * B, S5_STATE), F32), pltpu.VMEM((tt * B, S5_STATE), F32)],
        compiler_params=_params(("arbitrary",)), name="s5",
    )(u2, h0_re.reshape(B, S5_STATE), h0_im.reshape(B, S5_STATE), *weights)
    return o.reshape(T, B * GROUP_W), hr.reshape(B, S5_G, S5_P), hi.reshape(B, S5_G, S5_P)


def _outproj_kernel(oa_ref, ob_ref, oc_ref, od_ref, x_ref, gn_ref, w_ref, g_ref, b_ref, o_ref, *, alpha):
    gn = gn_ref[...]
    mix = jnp.concatenate([
        _rms_rows(oa_ref[...], gn[0:1]).astype(BF16),
        _rms_rows(ob_ref[...], gn[1:2]).astype(BF16),
        oc_ref[...].astype(BF16),
        _rms_rows(od_ref[...], gn[2:3]).astype(BF16)], -1)
    y = alpha * x_ref[...] + _dot(mix, w_ref[...])
    o_ref[...] = _ln_rows(y, g_ref[...], b_ref[...])


def _outproj(oa, ob, oc, od_tm, x, gn, w_out, g, b, alpha, tm):
    B, T, D = x.shape

    def row(f):
        return pl.BlockSpec((None, tm, f), lambda bi, i: (bi, i, 0))

    kern = functools.partial(_outproj_kernel, alpha=alpha)
    return pl.pallas_call(
        kern, grid=(B, T // tm),
        in_specs=[row(GROUP_W), row(GROUP_W), row(GROUP_W), pl.BlockSpec((tm, GROUP_W), lambda bi, i: (i, bi)),
                  row(D), _full_spec(gn.shape), _full_spec(w_out.shape), _full_spec((1, D)), _full_spec((1, D))],
        out_specs=row(D), out_shape=jax.ShapeDtypeStruct((B, T, D), F32),
        compiler_params=_params(("parallel", "parallel")), name="outproj",
    )(oa, ob, oc, od_tm, x, gn, w_out, g.reshape(1, D), b.reshape(1, D))


def _ffn_kernel(x_ref, wg_ref, wu_ref, wd_ref, g_ref, b_ref, o_ref, *, alpha):
    x = x_ref[...]
    xb = x.astype(BF16)
    hid = (jax.nn.silu(_dot(xb, wg_ref[...])) * _dot(xb, wu_ref[...])).astype(BF16)
    y = alpha * x + _dot(hid, wd_ref[...])
    o_ref[...] = _ln_rows(y, g_ref[...], b_ref[...])


def _ffn(x, wg, wu, wd, g, b, alpha, tm):
    B, T, D = x.shape
    row = pl.BlockSpec((None, tm, D), lambda bi, i: (bi, i, 0))

    def resident(shape):
        return pl.BlockSpec(shape, lambda *_: (0,) * len(shape), pipeline_mode=pl.Buffered(1))

    kern = functools.partial(_ffn_kernel, alpha=alpha)
    return pl.pallas_call(
        kern, grid=(B, T // tm),
        in_specs=[row, resident(wg.shape), resident(wu.shape), resident(wd.shape),
                  _full_spec((1, D)), _full_spec((1, D))],
        out_specs=row, out_shape=jax.ShapeDtypeStruct((B, T, D), F32),
        compiler_params=_params(("parallel", "parallel")), name="ffn",
    )(x, wg, wu, wd, g.reshape(1, D), b.reshape(1, D))


def _router_kernel(x_ref, wr_ref, br_ref, idx_ref, gate_ref):
    logits = _dot_nt(wr_ref[...], x_ref[...], HIGHEST) + br_ref[...]
    eid = lax.broadcasted_iota(jnp.int32, logits.shape, 0)
    m1 = jnp.max(logits, 0, keepdims=True)
    i1 = jnp.min(jnp.where(logits == m1, eid, N_EXP), 0, keepdims=True)
    rest = jnp.where(eid == i1, -jnp.inf, logits)
    m2 = jnp.max(rest, 0, keepdims=True)
    i2 = jnp.min(jnp.where(rest == m2, eid, N_EXP), 0, keepdims=True)
    e2 = jnp.exp(m2 - m1)
    den = 1.0 + e2
    idx_ref[...] = jnp.concatenate([i1, i2], 0)
    gate_ref[...] = jnp.concatenate([1.0 / den, e2 / den], 0)


def _router(x2, w_router, b_router, tm):
    N, D = x2.shape
    return pl.pallas_call(
        _router_kernel, grid=(N // tm,),
        in_specs=[pl.BlockSpec((tm, D), lambda i: (i, 0)), _full_spec((N_EXP, D)), _full_spec((N_EXP, 1))],
        out_specs=[pl.BlockSpec((TOP_K, tm), lambda i: (0, i)), pl.BlockSpec((TOP_K, tm), lambda i: (0, i))],
        out_shape=[jax.ShapeDtypeStruct((TOP_K, N), jnp.int32), jax.ShapeDtypeStruct((TOP_K, N), F32)],
        compiler_params=_params(("parallel",)), name="moe_router",
    )(x2, w_router.T, b_router.reshape(N_EXP, 1))


def _expert_kernel(blk_exp_ref, src_ref, dst_ref, x_hbm, wg_ref, wu_ref, wd_ref, out_hbm,
                   xbuf, ybuf, gsem, ssem):
    del blk_exp_ref
    n_sub = MOE_BLK // MOE_SUB

    def rows(j):
        return pl.ds(j * MOE_SUB, MOE_SUB)

    def gather(j):
        for r in range(j * MOE_SUB, (j + 1) * MOE_SUB):
            pltpu.make_async_copy(x_hbm.at[pl.ds(src_ref[0, 0, r], 1)], xbuf.at[pl.ds(r, 1)], gsem.at[j]).start()

    def scatter(j):
        for r in range(j * MOE_SUB, (j + 1) * MOE_SUB):
            pltpu.make_async_copy(ybuf.at[pl.ds(r, 1)], out_hbm.at[pl.ds(dst_ref[0, 0, r], 1)], ssem.at[j]).start()

    gather(0)
    for j in range(n_sub):
        pltpu.make_async_copy(x_hbm.at[pl.ds(0, MOE_SUB)], xbuf.at[rows(j)], gsem.at[j]).wait()
        if j + 1 < n_sub:
            gather(j + 1)
        if j >= 1:
            scatter(j - 1)
        xb = xbuf[rows(j), :].astype(BF16)
        hid = (jax.nn.silu(_dot(xb, wg_ref[...])) * _dot(xb, wu_ref[...])).astype(BF16)
        ybuf[rows(j), :] = _dot(hid, wd_ref[...])
    scatter(n_sub - 1)
    for j in range(n_sub):
        pltpu.make_async_copy(ybuf.at[rows(j)], out_hbm.at[pl.ds(0, MOE_SUB)], ssem.at[j]).wait()


def _combine_kernel(x_ref, y0_ref, y1_ref, gate_ref, g_ref, b_ref, o_ref, *, alpha):
    gate = gate_ref[...]
    f = y0_ref[...] * gate[:, 0:1] + y1_ref[...] * gate[:, 1:2]
    o_ref[...] = _ln_rows(alpha * x_ref[...] + f, g_ref[...], b_ref[...])


def _moe(x, w_router, b_router, wg, wu, wd, g, b, alpha):
    B, T, D = x.shape
    N = B * T
    x2 = x.reshape(N, D)
    tm = math.gcd(N, MOE_BLK)
    idx, gates = _router(x2, w_router, b_router, tm)

    n_slots = N * TOP_K
    e_flat = idx.T.reshape(-1)
    onehot = (e_flat[:, None] == jnp.arange(N_EXP, dtype=jnp.int32)[None, :]).astype(jnp.int32)
    csum = jnp.cumsum(onehot, 0)
    counts = csum[-1]
    rank = jnp.sum((csum - onehot) * onehot, 1)
    padded = (counts + MOE_BLK - 1) // MOE_BLK * MOE_BLK
    pad_end = jnp.cumsum(padded)
    pad_start = pad_end - padded
    dest = pad_start[e_flat] + rank
    n_blk = -(-(n_slots + N_EXP * (MOE_BLK - 1)) // MOE_BLK)
    rows = n_blk * MOE_BLK
    row_f = jnp.full((rows,), n_slots, jnp.int32).at[dest].set(jnp.arange(n_slots, dtype=jnp.int32))
    row_src = jnp.minimum(row_f // TOP_K, N - 1)
    row_dst = jnp.where(row_f < n_slots, (row_f % TOP_K) * N + row_f // TOP_K,
                        n_slots + jnp.arange(rows, dtype=jnp.int32) % MOE_BLK)
    blk_start = jnp.arange(n_blk, dtype=jnp.int32) * MOE_BLK
    blk_exp = jnp.minimum(jnp.sum((pad_end[None, :] <= blk_start[:, None]).astype(jnp.int32), 1), N_EXP - 1)

    idx_spec = pl.BlockSpec((1, 1, MOE_BLK), lambda i, be: (i, 0, 0), memory_space=pltpu.SMEM)
    y = pl.pallas_call(
        _expert_kernel,
        grid_spec=pltpu.PrefetchScalarGridSpec(
            num_scalar_prefetch=1, grid=(n_blk,),
            in_specs=[idx_spec, idx_spec,
                      pl.BlockSpec(memory_space=pl.ANY),
                      pl.BlockSpec((None, D, D_FF_EXP), lambda i, be: (be[i], 0, 0)),
                      pl.BlockSpec((None, D, D_FF_EXP), lambda i, be: (be[i], 0, 0)),
                      pl.BlockSpec((None, D_FF_EXP, D), lambda i, be: (be[i], 0, 0))],
            out_specs=pl.BlockSpec(memory_space=pl.ANY),
            scratch_shapes=[pltpu.VMEM((MOE_BLK, D), F32), pltpu.VMEM((MOE_BLK, D), F32),
                            pltpu.SemaphoreType.DMA((MOE_BLK // MOE_SUB,)),
                            pltpu.SemaphoreType.DMA((MOE_BLK // MOE_SUB,))]),
        out_shape=jax.ShapeDtypeStruct((n_slots + MOE_BLK, D), F32),
        compiler_params=_params(("arbitrary",)), name="moe_experts",
    )(blk_exp, row_src.reshape(n_blk, 1, MOE_BLK), row_dst.reshape(n_blk, 1, MOE_BLK), x2, wg, wu, wd)

    kern = functools.partial(_combine_kernel, alpha=alpha)
    rows_spec = pl.BlockSpec((tm, D), lambda i: (i, 0))
    out = pl.pallas_call(
        kern, grid=(N // tm,),
        in_specs=[rows_spec, rows_spec, pl.BlockSpec((tm, D), lambda i: (i + N // tm, 0)),
                  pl.BlockSpec((tm, TOP_K), lambda i: (i, 0)), _full_spec((1, D)), _full_spec((1, D))],
        out_specs=rows_spec, out_shape=jax.ShapeDtypeStruct((N, D), F32),
        compiler_params=_params(("parallel",)), name="moe_combine",
    )(x2, y, y, gates.T, g.reshape(1, D), b.reshape(1, D))
    return out.reshape(B, T, D)


def _round_up(n, m):
    return -(-n // m) * m


def _run_trunk(x, states, P, depth):
    lat_c, kr_c, sbk_c, sbv_c, conv_c, ssm_c, re_c, im_c = states
    B, T, D = x.shape
    past = lat_c.shape[2]
    alpha = (2 * depth) ** 0.25
    long_seq = T >= 512
    tm = 512 if long_seq else T
    tq = ATTN_TQ if long_seq else T
    tk = ATTN_TK if long_seq else 2 * LANES
    S = past + T
    S_pad = _round_up(S, tk)

    tab = _rope_table(past, T)
    x = _layer_norm(x, P['ln_in_g'], P['ln_in_b'], tm)
    outs = [[] for _ in range(8)]
    for l in range(depth):
        pm, sbq, sbk, sbv, sbk_b, sbv_b, gq, gz, s5u = _inproj(x, _prep_inproj_weights(P['w_in'][l]), tm)

        qn, kvn, wq, wqr, wk, wv, e = _prep_mla_weights(P['mla_q_norm'][l], P['mla_kv_norm'][l],
                                                        P['mla_w_uq'][l], P['mla_w_ukv'][l])
        q_a, lat_n, kr_n, krp_n = _mla_prep(pm, tab, qn, kvn, wq, wqr, tm)
        kr_cache = jnp.pad(kr_c[l], ((0, 0), (0, 0), (0, LANES - ROPE)))
        lat_all = jnp.pad(jnp.concatenate([lat_c[l], lat_n], 1), ((0, 0), (0, S_pad - S), (0, 0)))
        krp_all = jnp.pad(jnp.concatenate([kr_cache, krp_n], 1), ((0, 0), (0, S_pad - S), (0, 0)))
        k_a, v_a = _mla_kv(lat_all, krp_all, wk, wv, e, tk)
        o_a = _mla_attn(q_a, k_a, v_a, past, tq, tk)

        if past or S_pad != S:
            k_all = jnp.pad(jnp.concatenate([sbk_c[l].reshape(B, past, GROUP_W).astype(BF16), sbk_b], 1),
                            ((0, 0), (0, S_pad - S), (0, 0)))
            v_all = jnp.pad(jnp.concatenate([sbv_c[l].reshape(B, past, GROUP_W).astype(BF16), sbv_b], 1),
                            ((0, 0), (0, S_pad - S), (0, 0)))
        else:
            k_all, v_all = sbk_b, sbv_b
        o_b = _sb_attn(sbq, k_all, v_all, past, tq, tk)

        o_c, ssm_n = _gdn(gq, gz, conv_c[l], ssm_c[l], P['gdn_conv_w'][l], P['gdn_a_log'][l],
                          P['gdn_dt_bias'][l], P['gdn_norm'][l])
        conv_n = jnp.concatenate([conv_c[l], gq], 1)[:, T:]

        s5w = _prep_s5_weights(P['s5_lam_re'][l], P['s5_lam_im'][l], P['s5_log_step'][l], P['s5_b_re'][l],
                               P['s5_b_im'][l], P['s5_c_re'][l], P['s5_c_im'][l], P['s5_d'][l], P['s5_w_glu'][l])
        o_d, re_n, im_n = _s5(s5u, re_c[l], im_c[l], s5w, B, min(64, T))

        x = _outproj(o_a, o_b, o_c, o_d, x, P['grp_norm'][l], P['w_out'][l].astype(BF16),
                     P['ln1_g'][l], P['ln1_b'][l], alpha, tm)
        i = l // 2
        if l % 2 == 0:
            x = _ffn(x, P['ffn_w_gate'][i].astype(BF16), P['ffn_w_up'][i].astype(BF16),
                     P['ffn_w_down'][i].astype(BF16), P['ln2_g'][l], P['ln2_b'][l], alpha, min(tm, 256))
        else:
            x = _moe(x, P['moe_w_router'][i], P['moe_b_router'][i], P['moe_w_gate'][i].astype(BF16),
                     P['moe_w_up'][i].astype(BF16), P['moe_w_down'][i].astype(BF16),
                     P['ln2_g'][l], P['ln2_b'][l], alpha)
        news = (lat_n, kr_n, sbk.reshape(B, T, N_HEADS, HEAD_D), sbv.reshape(B, T, N_HEADS, HEAD_D),
                conv_n, ssm_n, re_n, im_n)
        for lst, a in zip(outs, news):
            lst.append(a)
    return x, [jnp.stack(lst) for lst in outs]


def kernel(x_prompt, x_sample, cache_mla_latent, cache_mla_krope, cache_sb_k, cache_sb_v,
           state_gdn_conv, state_gdn_ssm, state_s5_re, state_s5_im,
           ln_in_g, ln_in_b, w_in, mla_q_norm, mla_kv_norm, mla_w_uq, mla_w_ukv,
           gdn_conv_w, gdn_a_log, gdn_dt_bias, gdn_norm,
           s5_lam_re, s5_lam_im, s5_log_step, s5_b_re, s5_b_im, s5_c_re, s5_c_im, s5_d, s5_w_glu,
           grp_norm, w_out, ln1_g, ln1_b, ln2_g, ln2_b,
           ffn_w_gate, ffn_w_up, ffn_w_down,
           moe_w_router, moe_b_router, moe_w_gate, moe_w_up, moe_w_down):
    P = dict(ln_in_g=ln_in_g, ln_in_b=ln_in_b, w_in=w_in, mla_q_norm=mla_q_norm, mla_kv_norm=mla_kv_norm,
             mla_w_uq=mla_w_uq, mla_w_ukv=mla_w_ukv, gdn_conv_w=gdn_conv_w, gdn_a_log=gdn_a_log,
             gdn_dt_bias=gdn_dt_bias, gdn_norm=gdn_norm, s5_lam_re=s5_lam_re, s5_lam_im=s5_lam_im,
             s5_log_step=s5_log_step, s5_b_re=s5_b_re, s5_b_im=s5_b_im, s5_c_re=s5_c_re, s5_c_im=s5_c_im,
             s5_d=s5_d, s5_w_glu=s5_w_glu, grp_norm=grp_norm, w_out=w_out, ln1_g=ln1_g, ln1_b=ln1_b,
             ln2_g=ln2_g, ln2_b=ln2_b, ffn_w_gate=ffn_w_gate, ffn_w_up=ffn_w_up, ffn_w_down=ffn_w_down,
             moe_w_router=moe_w_router, moe_b_router=moe_b_router, moe_w_gate=moe_w_gate,
             moe_w_up=moe_w_up, moe_w_down=moe_w_down)
    depth = w_in.shape[0]
    Bp = x_prompt.shape[0]
    dt = x_prompt.dtype
    prompt_states = (jnp.zeros((depth, Bp, 0, KV_LORA), dt), jnp.zeros((depth, Bp, 0, ROPE), dt),
                     jnp.zeros((depth, Bp, 0, N_HEADS, HEAD_D), dt), jnp.zeros((depth, Bp, 0, N_HEADS, HEAD_D), dt),
                     jnp.zeros((depth, Bp, CONV_W - 1, N_GDN_QKV), dt),
                     jnp.zeros((depth, Bp, N_HEADS, HEAD_D, HEAD_D), F32),
                     jnp.zeros((depth, Bp, S5_G, S5_P), F32), jnp.zeros((depth, Bp, S5_G, S5_P), F32))
    y_prompt, st_p = _run_trunk(x_prompt, prompt_states, P, depth)
    sample_states = (cache_mla_latent, cache_mla_krope, cache_sb_k, cache_sb_v,
                     state_gdn_conv, state_gdn_ssm, state_s5_re, state_s5_im)
    y_sample, st_s = _run_trunk(x_sample, sample_states, P, depth)
    return (y_prompt, y_sample, *st_p, *st_s)
```

```python
import functools
import math

import jax
import jax.numpy as jnp
from jax import lax
from jax.experimental import pallas as pl
from jax.experimental.pallas import tpu as pltpu

F32 = jnp.float32
BF16 = jnp.bfloat16
HIGHEST = lax.Precision.HIGHEST

D_MODEL = 1024
CHUNK = 64
GROUP_W = 256
N_HEADS = 4
HEAD_D = 64
NOPE = 64
ROPE = 32
Q_LORA = 192
KV_LORA = 128
ROPE_THETA = 10000.0
MLA_SCALE = (NOPE + ROPE) ** -0.5
SB_SCALE = HEAD_D ** -0.5
LOG2E = math.log2(math.e)
CONV_W = 4
N_GDN_QKV = 768
S5_CG = 16
S5_G = 16
S5_P = 64
S5_STATE = S5_G * S5_P
D_FF = 2816
N_EXP = 8
TOP_K = 2
D_FF_EXP = 1792
MOE_BLK = 512
LN_EPS = 1e-5
RMS_EPS = 1e-6
NEG_INF = -1e30
OFF_SB = Q_LORA + KV_LORA + ROPE
OFF_GDN = OFF_SB + 3 * GROUP_W
OFF_S5 = OFF_GDN + N_GDN_QKV + 2 * N_HEADS + GROUP_W

LANES = 128
HEAD_PAD = LANES
VMEM_LIMIT = 48 * 1024 * 1024
GDN_BATCH_PER_STEP = 8
MOE_SUB = 128
ATTN_TQ = 512
ATTN_TK = 512
MLA_TILE = (256, 512)
SB_TILE = (512, 256)


def _params(sem):
    return pltpu.CompilerParams(dimension_semantics=sem, vmem_limit_bytes=VMEM_LIMIT)


def _dot(a, b, precision=None):
    return jnp.dot(a, b, preferred_element_type=F32, precision=precision)


def _dot_nt(a, b, precision=None):
    return lax.dot_general(a, b, (((1,), (1,)), ((), ())), preferred_element_type=F32, precision=precision)


def _dot_tn(a, b, precision=None):
    return lax.dot_general(a, b, (((0,), (0,)), ((), ())), preferred_element_type=F32, precision=precision)


def _ln_rows(x, g, b):
    mu = jnp.mean(x, -1, keepdims=True)
    xc = x - mu
    var = jnp.mean(xc * xc, -1, keepdims=True)
    return xc * lax.rsqrt(var + LN_EPS) * g + b


def _rms_rows(x, g, n=None):
    n = x.shape[-1] if n is None else n
    ms = jnp.sum(x * x, -1, keepdims=True) * (1.0 / n)
    return x * lax.rsqrt(ms + RMS_EPS) * g


def _full_spec(shape):
    nd = len(shape)
    return pl.BlockSpec(shape, lambda *_: (0,) * nd)


def _ln_kernel(x_ref, g_ref, b_ref, o_ref):
    o_ref[...] = _ln_rows(x_ref[...], g_ref[...], b_ref[...])


def _layer_norm(x, g, b, tm):
    B, T, D = x.shape
    row = pl.BlockSpec((None, tm, D), lambda bi, i: (bi, i, 0))
    return pl.pallas_call(
        _ln_kernel, grid=(B, T // tm),
        in_specs=[row, _full_spec((1, D)), _full_spec((1, D))],
        out_specs=row, out_shape=jax.ShapeDtypeStruct((B, T, D), F32),
        compiler_params=_params(("parallel", "parallel")), name="ln_in",
    )(x, g.reshape(1, D), b.reshape(1, D))


MLA_IN_PAD = 640
GZ_PAD = 384


def _inproj_kernel(x_ref, wm_ref, wsb_ref, wgq_ref, wgz_ref, ws5_ref,
                   pm_ref, sbq_ref, sbk_ref, sbv_ref, sbkb_ref, sbvb_ref, gq_ref, gz_ref, s5_ref):
    xb = x_ref[...].astype(BF16)
    pm_ref[...] = _dot(xb, wm_ref[...])
    sb = _dot(xb, wsb_ref[...])
    sbq_ref[...] = (sb[:, :GROUP_W] * (SB_SCALE * LOG2E)).astype(BF16)
    sbk_ref[...] = sb[:, GROUP_W:2 * GROUP_W]
    sbv_ref[...] = sb[:, 2 * GROUP_W:]
    sbkb_ref[...] = sb[:, GROUP_W:2 * GROUP_W].astype(BF16)
    sbvb_ref[...] = sb[:, 2 * GROUP_W:].astype(BF16)
    gq_ref[...] = _dot(xb, wgq_ref[...])
    gz_ref[...] = _dot(xb, wgz_ref[...])
    s5_ref[...] = _dot(xb, ws5_ref[...])


def _rotate_half_cols(w):
    half = w.shape[-1] // 2
    return jnp.concatenate([-w[..., half:], w[..., :half]], -1)


def _prep_inproj_weights(w_in):
    zeros = lambda n: jnp.zeros((D_MODEL, n), F32)
    w_cq = w_in[:, :Q_LORA]
    w_ckv = w_in[:, Q_LORA:Q_LORA + KV_LORA]
    w_kr = w_in[:, Q_LORA + KV_LORA:OFF_SB]
    wm = jnp.concatenate([w_ckv, w_cq, zeros(64), w_kr, zeros(96), _rotate_half_cols(w_kr), zeros(96)], 1)
    wsb = w_in[:, OFF_SB:OFF_GDN]
    wgq = w_in[:, OFF_GDN:OFF_GDN + N_GDN_QKV]
    o_a = OFF_GDN + N_GDN_QKV
    wgz = jnp.concatenate([w_in[:, o_a + 2 * N_HEADS:OFF_S5], w_in[:, o_a:o_a + 2 * N_HEADS],
                           zeros(GZ_PAD - GROUP_W - 2 * N_HEADS)], 1)
    ws5 = w_in[:, OFF_S5:]
    return tuple(w.astype(BF16) for w in (wm, wsb, wgq, wgz, ws5))


def _inproj(x, weights, tm):
    B, T, D = x.shape
    wm, wsb, wgq, wgz, ws5 = weights

    def row(f):
        return pl.BlockSpec((None, tm, f), lambda bi, i: (bi, i, 0))

    def out(f, dt=F32):
        return jax.ShapeDtypeStruct((B, T, f), dt)

    return pl.pallas_call(
        _inproj_kernel, grid=(B, T // tm),
        in_specs=[row(D)] + [_full_spec(w.shape) for w in weights],
        out_specs=[row(MLA_IN_PAD), row(GROUP_W), row(GROUP_W), row(GROUP_W), row(GROUP_W), row(GROUP_W),
                   row(N_GDN_QKV), row(GZ_PAD), pl.BlockSpec((tm, GROUP_W), lambda bi, i: (i, bi))],
        out_shape=[out(MLA_IN_PAD), out(GROUP_W, BF16), out(GROUP_W), out(GROUP_W), out(GROUP_W, BF16),
                   out(GROUP_W, BF16), out(N_GDN_QKV), out(GZ_PAD),
                   jax.ShapeDtypeStruct((T, B * GROUP_W), F32)],
        compiler_params=_params(("parallel", "parallel")), name="inproj",
    )(x, *weights)


def _mla_prep_kernel(pm_ref, tab_ref, qn_ref, kvn_ref, wq_ref, wqr_ref, q_ref, lat_ref, kr_ref, krp_ref):
    pm = pm_ref[...]
    tab = tab_ref[...]
    c_kv = pm[:, :KV_LORA]
    c_q = pm[:, KV_LORA:KV_LORA + 2 * LANES]
    k_r = pm[:, 3 * LANES:4 * LANES]
    k_rr = pm[:, 4 * LANES:5 * LANES]
    lat_ref[...] = _rms_rows(c_kv, kvn_ref[...])
    nq = _rms_rows(c_q, qn_ref[...], n=Q_LORA).astype(BF16)
    qp = _dot(nq, wq_ref[...])
    qr = _dot(nq, wqr_ref[...])
    cq, sq = tab[:, :LANES], tab[:, LANES:2 * LANES]
    for h in range(N_HEADS):
        sl = slice(h * HEAD_PAD, (h + 1) * HEAD_PAD)
        q_ref[:, sl] = ((qp[:, sl] * cq + qr[:, sl] * sq) * (MLA_SCALE * LOG2E)).astype(BF16)
    kr_new = k_r * tab[:, 2 * LANES:3 * LANES] + k_rr * tab[:, 3 * LANES:]
    krp_ref[...] = kr_new
    kr_ref[...] = kr_new[:, :ROPE]


def _rope_table(past, T):
    half = ROPE // 2
    inv = ROPE_THETA ** (-jnp.arange(half, dtype=F32) / half)
    ang = (past + jnp.arange(T, dtype=jnp.int32)).astype(F32)[:, None] * inv
    c, s = jnp.cos(ang), jnp.sin(ang)
    one, zero = jnp.ones((T, NOPE), F32), jnp.zeros((T, NOPE), F32)
    z32, z96 = jnp.zeros((T, 32), F32), jnp.zeros((T, 96), F32)
    return jnp.concatenate([one, c, c, z32, zero, s, s, z32, c, c, z96, s, s, z96], 1)


def _prep_mla_weights(q_norm, kv_norm, w_uq, w_ukv):
    w3 = w_uq.reshape(Q_LORA, N_HEADS, NOPE + ROPE)
    zq = jnp.zeros((Q_LORA, N_HEADS, HEAD_PAD - NOPE - ROPE), F32)
    wq = jnp.concatenate([w3, zq], -1).reshape(Q_LORA, N_HEADS * HEAD_PAD)
    w3r = jnp.concatenate([jnp.zeros((Q_LORA, N_HEADS, NOPE), F32), _rotate_half_cols(w3[..., NOPE:]), zq], -1)
    wqr = w3r.reshape(Q_LORA, N_HEADS * HEAD_PAD)
    pad_rows = jnp.zeros((2 * LANES - Q_LORA, N_HEADS * HEAD_PAD), F32)
    wq = jnp.concatenate([wq, pad_rows], 0).astype(BF16)
    wqr = jnp.concatenate([wqr, pad_rows], 0).astype(BF16)
    qn = jnp.concatenate([q_norm, jnp.zeros((2 * LANES - Q_LORA,), F32)]).reshape(1, 2 * LANES)
    kvn = kv_norm.reshape(1, KV_LORA)
    kv3 = w_ukv.reshape(KV_LORA, N_HEADS, NOPE + HEAD_D)
    zk = jnp.zeros((KV_LORA, N_HEADS, HEAD_PAD - NOPE), F32)
    wk = jnp.concatenate([kv3[..., :NOPE], zk], -1).reshape(KV_LORA, N_HEADS * HEAD_PAD).astype(BF16)
    wv = kv3[..., NOPE:].reshape(KV_LORA, GROUP_W).astype(BF16)
    e = jnp.zeros((LANES, N_HEADS, HEAD_PAD), F32)
    e = e.at[jnp.arange(ROPE)[:, None], jnp.arange(N_HEADS)[None, :], NOPE + jnp.arange(ROPE)[:, None]].set(1.0)
    e = e.reshape(LANES, N_HEADS * HEAD_PAD).astype(BF16)
    return qn, kvn, wq, wqr, wk, wv, e


def _mla_prep(pm, tab, qn, kvn, wq, wqr, tm):
    B, T, _ = pm.shape

    def row(f):
        return pl.BlockSpec((None, tm, f), lambda bi, i: (bi, i, 0))

    return pl.pallas_call(
        _mla_prep_kernel, grid=(B, T // tm),
        in_specs=[row(MLA_IN_PAD), pl.BlockSpec((tm, 4 * LANES), lambda bi, i: (i, 0)),
                  _full_spec(qn.shape), _full_spec(kvn.shape), _full_spec(wq.shape), _full_spec(wqr.shape)],
        out_specs=[row(N_HEADS * HEAD_PAD), row(KV_LORA), row(ROPE), row(LANES)],
        out_shape=[jax.ShapeDtypeStruct((B, T, N_HEADS * HEAD_PAD), BF16),
                   jax.ShapeDtypeStruct((B, T, KV_LORA), F32),
                   jax.ShapeDtypeStruct((B, T, ROPE), F32),
                   jax.ShapeDtypeStruct((B, T, LANES), F32)],
        compiler_params=_params(("parallel", "parallel")), name="mla_prep",
    )(pm, tab, qn, kvn, wq, wqr)


def _mla_kv_kernel(lat_ref, krp_ref, wk_ref, wv_ref, e_ref, k_ref, v_ref):
    lb = lat_ref[...].astype(BF16)
    kb = krp_ref[...].astype(BF16)
    k_ref[...] = (_dot(lb, wk_ref[...]) + _dot(kb, e_ref[...])).astype(BF16)
    v_ref[...] = _dot(lb, wv_ref[...]).astype(BF16)


def _mla_kv(lat, krp, wk, wv, e, tm):
    B, S, _ = lat.shape

    def row(f):
        return pl.BlockSpec((None, tm, f), lambda bi, i: (bi, i, 0))

    return pl.pallas_call(
        _mla_kv_kernel, grid=(B, S // tm),
        in_specs=[row(KV_LORA), row(LANES), _full_spec(wk.shape), _full_spec(wv.shape), _full_spec(e.shape)],
        out_specs=[row(N_HEADS * HEAD_PAD), row(GROUP_W)],
        out_shape=[jax.ShapeDtypeStruct((B, S, N_HEADS * HEAD_PAD), BF16),
                   jax.ShapeDtypeStruct((B, S, GROUP_W), BF16)],
        compiler_params=_params(("parallel", "parallel")), name="mla_kv",
    )(lat, krp, wk, wv, e)


def _lane_tile(a, n):
    return a if n == 1 else jnp.concatenate([a] * n, axis=1)


def _per_head_lanes(stats):
    low = lax.broadcasted_iota(jnp.int32, stats[0].shape, 1) < HEAD_D
    return jnp.concatenate([jnp.where(low, stats[0], stats[1]), jnp.where(low, stats[2], stats[3])], axis=1)


def _block_diag_values(vc):
    head = lax.broadcasted_iota(jnp.int32, vc.shape, 1) // HEAD_D
    return jnp.concatenate([jnp.where(head == h, vc, jnp.zeros_like(vc)) for h in range(N_HEADS)], axis=0)


def _mla_attn_kernel(q_ref, k_ref, v_ref, o_ref, m_scr, l_scr, acc_scr, *, tq, tk, rq, ck, past, nkv):
    qi = pl.program_id(1)
    kj = pl.program_id(2)
    delta = past + qi * tq - kj * tk

    @pl.when(kj == 0)
    def _():
        m_scr[...] = jnp.full(m_scr.shape, NEG_INF, F32)
        l_scr[...] = jnp.zeros(l_scr.shape, F32)
        acc_scr[...] = jnp.zeros(acc_scr.shape, F32)

    def tiles(masked):
        v_bd = [_block_diag_values(v_ref[c * ck:(c + 1) * ck, :]) for c in range(tk // ck)]
        for r in range(tq // rq):
            rows = slice(r * rq, (r + 1) * rq)
            m = [m_scr[h, rows, :] for h in range(N_HEADS)]
            l = [l_scr[h, rows, :] for h in range(N_HEADS)]
            acc = acc_scr[rows, :]
            for c in range(tk // ck):
                keys = slice(c * ck, (c + 1) * ck)
                if masked:
                    q_chunk = (past + qi * tq + r * rq + lax.broadcasted_iota(jnp.int32, (rq, ck), 0)) // CHUNK
                    k_chunk = (kj * tk + c * ck + lax.broadcasted_iota(jnp.int32, (rq, ck), 1)) // CHUNK
                    allowed = k_chunk <= q_chunk
                ps, alphas = [], []
                for h in range(N_HEADS):
                    sl = slice(h * HEAD_PAD, (h + 1) * HEAD_PAD)
                    s = _dot_nt(q_ref[rows, sl], k_ref[keys, sl])
                    if masked:
                        s = jnp.where(allowed, s, NEG_INF)
                    m_new = jnp.maximum(m[h], jnp.max(s, -1, keepdims=True))
                    alpha = jnp.exp2(m[h] - m_new)
                    p = jnp.exp2(s - _lane_tile(m_new, ck // LANES))
                    l[h] = alpha * l[h] + jnp.sum(p, -1, keepdims=True)
                    m[h] = m_new
                    ps.append(p.astype(BF16))
                    alphas.append(alpha)
                acc = _per_head_lanes(alphas) * acc + _dot(jnp.concatenate(ps, axis=1), v_bd[c])
            for h in range(N_HEADS):
                m_scr[h, rows, :] = m[h]
                l_scr[h, rows, :] = l[h]
            acc_scr[rows, :] = acc

    @pl.when(delta >= tk)
    def _():
        tiles(False)

    @pl.when((delta < tk) & (delta + tq > 0))
    def _():
        tiles(True)

    @pl.when(kj == nkv - 1)
    def _():
        o_ref[...] = acc_scr[...] / _per_head_lanes([l_scr[h] for h in range(N_HEADS)])


def _mla_attn(q, k, v, past, tq, tk):
    B, T, W = q.shape
    S = k.shape[1]
    nq, nkv = T // tq, S // tk
    rq, ck = min(MLA_TILE[0], tq), min(MLA_TILE[1], tk)

    def kv_map(bi, i, j):
        last = (past + (i + 1) * tq - 1) // tk
        return (bi, jnp.minimum(j, last), 0)

    kern = functools.partial(_mla_attn_kernel, tq=tq, tk=tk, rq=rq, ck=ck, past=past, nkv=nkv)
    return pl.pallas_call(
        kern, grid=(B, nq, nkv),
        in_specs=[pl.BlockSpec((None, tq, W), lambda bi, i, j: (bi, i, 0)),
                  pl.BlockSpec((None, tk, W), kv_map), pl.BlockSpec((None, tk, GROUP_W), kv_map)],
        out_specs=pl.BlockSpec((None, tq, GROUP_W), lambda bi, i, j: (bi, i, 0)),
        out_shape=jax.ShapeDtypeStruct((B, T, GROUP_W), F32),
        scratch_shapes=[pltpu.VMEM((N_HEADS, tq, LANES), F32), pltpu.VMEM((N_HEADS, tq, LANES), F32),
                        pltpu.VMEM((tq, GROUP_W), F32)],
        compiler_params=_params(("parallel", "parallel", "arbitrary")), name="mla_attn",
    )(q, k, v)


def _sb_attn_kernel(q_ref, k_ref, v_ref, u_ref, o_ref, c_scr, acc_scr, *, tq, tk, rq, ck, past, nkv):
    qi = pl.program_id(1)
    j = pl.program_id(2)
    last = (past + (qi + 1) * tq - 2) // tk
    jb = last - j
    delta = past + qi * tq - jb * tk

    @pl.when(j == 0)
    def _():
        c_scr[...] = jnp.zeros(c_scr.shape, F32)
        acc_scr[...] = jnp.zeros(acc_scr.shape, F32)

    def tiles(masked):
        u2 = u_ref[...]
        v_bd = [_block_diag_values(v_ref[c * ck:(c + 1) * ck, :]) for c in range(tk // ck)]
        head = lax.broadcasted_iota(jnp.int32, (rq, GROUP_W), 1) // HEAD_D
        for r in range(tq // rq):
            rows = slice(r * rq, (r + 1) * rq)
            q_r = q_ref[rows, :]
            q_h = [jnp.where(head == h, q_r, jnp.zeros_like(q_r)) for h in range(N_HEADS)]
            carry = [c_scr[h, rows, :] for h in range(N_HEADS)]
            acc = acc_scr[rows, :]
            for c in reversed(range(tk // ck)):
                k_c = k_ref[c * ck:(c + 1) * ck, :]
                if masked:
                    q_pos = delta + r * rq + lax.broadcasted_iota(jnp.int32, (rq, ck), 0)
                    allowed = (c * ck + lax.broadcasted_iota(jnp.int32, (rq, ck), 1)) < q_pos
                ws = []
                for h in range(N_HEADS):
                    z = _dot_nt(q_h[h], k_c)
                    sp = jnp.log(1.0 + jnp.exp2(-jnp.abs(z))) * LOG2E
                    log_beta = jnp.minimum(z, 0.0) - sp
                    log_1m = log_beta - z
                    if masked:
                        log_1m = jnp.where(allowed, log_1m, 0.0)
                    after = _dot(log_1m.astype(BF16), u2) + _lane_tile(carry[h], ck // LANES)
                    w = jnp.exp2(log_beta + after)
                    if masked:
                        w = jnp.where(allowed, w, 0.0)
                    ws.append(w.astype(BF16))
                    carry[h] = carry[h] + jnp.sum(log_1m, -1, keepdims=True)
                acc = acc + _dot(jnp.concatenate(ws, axis=1), v_bd[c])
            for h in range(N_HEADS):
                c_scr[h, rows, :] = carry[h]
            acc_scr[rows, :] = acc

    @pl.when((delta >= tk) & (jb >= 0))
    def _():
        tiles(False)

    @pl.when((delta < tk) & (jb >= 0))
    def _():
        tiles(True)

    @pl.when(j == nkv - 1)
    def _():
        o_ref[...] = acc_scr[...]


def _sb_attn(q, k, v, past, tq, tk):
    B, T, W = q.shape
    S = k.shape[1]
    nq, nkv = T // tq, S // tk
    rq, ck = min(SB_TILE[0], tq), min(SB_TILE[1], tk)
    later = lax.broadcasted_iota(jnp.int32, (ck, ck), 0) > lax.broadcasted_iota(jnp.int32, (ck, ck), 1)
    u2 = later.astype(BF16)

    def kv_map(bi, i, j):
        last = (past + (i + 1) * tq - 2) // tk
        return (bi, jnp.maximum(last - j, 0), 0)

    kern = functools.partial(_sb_attn_kernel, tq=tq, tk=tk, rq=rq, ck=ck, past=past, nkv=nkv)
    return pl.pallas_call(
        kern, grid=(B, nq, nkv),
        in_specs=[pl.BlockSpec((None, tq, W), lambda bi, i, j: (bi, i, 0)),
                  pl.BlockSpec((None, tk, W), kv_map), pl.BlockSpec((None, tk, W), kv_map),
                  _full_spec((ck, ck))],
        out_specs=pl.BlockSpec((None, tq, GROUP_W), lambda bi, i, j: (bi, i, 0)),
        out_shape=jax.ShapeDtypeStruct((B, T, GROUP_W), F32),
        scratch_shapes=[pltpu.VMEM((N_HEADS, tq, LANES), F32), pltpu.VMEM((tq, GROUP_W), F32)],
        compiler_params=_params(("parallel", "parallel", "arbitrary")), name="sb_attn",
    )(q, k, v, u2)


def _softplus(x):
    return jnp.maximum(x, 0.0) + jnp.log1p(jnp.exp(-jnp.abs(x)))


def _split_bf16(a):
    hi = a.astype(BF16)
    return hi, (a - hi.astype(F32)).astype(BF16)


def _dot_split(a, b):
    a_hi, a_lo = _split_bf16(a)
    b_hi, b_lo = _split_bf16(b)
    m = a.shape[0]
    top = _dot(jnp.concatenate([a_hi, a_lo], 0), b_hi)
    return top[:m] + top[m:] + _dot(a_hi, b_lo)


def _head_sumsq(x, hsum):
    hi, lo = _split_bf16(x * x)
    m = x.shape[0]
    both = _dot(jnp.concatenate([hi, lo], 0), hsum)
    return both[:m] + both[m:]


def _gdn_kernel(qkv_ref, gz_ref, conv0_ref, s0_ref, cw_ref, alog_ref, dtb_ref, ng_ref, hsum_ref,
                o_ref, s_out_ref, xp_scr, s_scr, o_scr, *, L, nt, bb):
    ti = pl.program_id(1)
    halo = CONV_W - 1

    @pl.when(ti == 0)
    def _():
        xp_scr[:, 8 - halo:8, :] = conv0_ref[...]
        s_scr[...] = s0_ref[...]

    hsum = hsum_ref[...]
    ii = lax.broadcasted_iota(jnp.int32, (L, L), 0)
    jj = lax.broadcasted_iota(jnp.int32, (L, L), 1)
    incl = ii >= jj
    strict = ii > jj
    eye = (ii == jj).astype(F32)

    units = []
    for b in range(bb):
        x = qkv_ref[b]
        xp_scr[b, 8:8 + L, :] = x
        conv = jnp.zeros((L, N_GDN_QKV), F32)
        for i in range(CONV_W):
            conv = conv + xp_scr[b, 8 - halo + i:8 - halo + i + L, :] * cw_ref[i:i + 1, :]
        xp_scr[b, 8 - halo:8, :] = x[L - halo:, :]
        conv = jax.nn.silu(conv)
        qa = conv[:, :GROUP_W]
        ka = conv[:, GROUP_W:2 * GROUP_W]
        va = conv[:, 2 * GROUP_W:]
        qa = qa * lax.rsqrt(_head_sumsq(qa, hsum) + 1e-6) * (HEAD_D ** -0.5)
        ka = ka * lax.rsqrt(_head_sumsq(ka, hsum) + 1e-6)
        ab = gz_ref[b, :, GROUP_W:]
        g_all = -jnp.exp(alog_ref[...]) * _softplus(ab + dtb_ref[...])
        beta_all = jax.nn.sigmoid(ab)
        for h in range(N_HEADS):
            hs = slice(h * HEAD_D, (h + 1) * HEAD_D)
            units.append(dict(b=b, h=h, q=qa[:, hs], k=ka[:, hs], v=va[:, hs], g=g_all[:, h:h + 1],
                              beta=beta_all[:, N_HEADS + h:N_HEADS + h + 1]))

    for un in units:
        g_b = jnp.broadcast_to(un['g'], (L, L))
        gc_row = jnp.sum(jnp.where(ii <= jj, g_b, 0.0), 0, keepdims=True)
        gc_col = jnp.sum(jnp.where(ii == jj, jnp.broadcast_to(gc_row, (L, L)), 0.0), 1, keepdims=True)
        un['decay'] = jnp.where(incl, jnp.exp(jnp.where(incl, gc_col - gc_row, 0.0)), 0.0)
        un['e_col'] = jnp.exp(gc_col)
        gc_last = gc_row[:, L - 1:L]
        un['g_last'] = jnp.exp(gc_last)
        un['k_tail'] = jnp.exp(gc_last - gc_col)
        un['kb'] = un['k'] * un['beta']
        un['kbf'] = un['k'].astype(BF16)
    for un in units:
        m = jnp.where(strict, _dot_nt(un['kb'].astype(BF16), un['kbf']) * un['decay'], 0.0)
        un['p'] = -m
        un['t'] = eye - m

    n_sq = int(math.log2(L)) - 1
    for r in range(n_sq + 1):
        for un in units:
            p_hi, p_lo = _split_bf16(un['p'])
            if r == 0:
                lhs_hi, lhs_lo = p_hi, p_lo
            else:
                t_hi, t_lo = _split_bf16(un['t'])
                lhs_hi = jnp.concatenate([p_hi, t_hi], 0) if r < n_sq else t_hi
                lhs_lo = jnp.concatenate([p_lo, t_lo], 0) if r < n_sq else t_lo
            rows = lhs_hi.shape[0]
            top = _dot(jnp.concatenate([lhs_hi, lhs_lo], 0), p_hi)
            prod = top[:rows] + top[rows:] + _dot(lhs_hi, p_lo)
            if r == 0:
                un['p'] = prod
            elif r < n_sq:
                un['p'] = prod[:L]
                un['t'] = un['t'] + prod[L:]
            else:
                un['t'] = un['t'] + prod
    for un in units:
        un['u'] = _dot_split(un['t'], un['v'] * un['beta'])
        un['w'] = _dot_split(un['t'], un['kb'] * un['e_col'])
        un['attn'] = jnp.where(incl, _dot_nt(un['q'].astype(BF16), un['kbf']) * un['decay'], 0.0).astype(BF16)

    for un in units:
        un['S'] = s_scr[un['b'], un['h']]
        un['Sb'] = un['S'].astype(BF16)
    for un in units:
        un['v_new'] = (un['u'] - _dot(un['w'].astype(BF16), un['Sb'])).astype(BF16)
    for un in units:
        hs = slice(un['h'] * HEAD_D, (un['h'] + 1) * HEAD_D)
        o_h = _dot((un['q'] * un['e_col']).astype(BF16), un['Sb']) + _dot(un['attn'], un['v_new'])
        o_scr[un['b'], :, hs] = o_h
        k_dec = (un['k'] * un['k_tail']).astype(BF16)
        s_scr[un['b'], un['h']] = un['S'] * un['g_last'] + _dot_tn(k_dec, un['v_new'])

    for b in range(bb):
        o = o_scr[b]
        ms = _head_sumsq(o, hsum) * (1.0 / HEAD_D)
        o_ref[b] = o * lax.rsqrt(ms + RMS_EPS) * ng_ref[...] * jax.nn.silu(gz_ref[b, :, :GROUP_W])

    @pl.when(ti == nt - 1)
    def _():
        s_out_ref[...] = s_scr[...]


def _gdn(qkv, gz, conv0, s0, conv_w, a_log, dt_bias, norm_g):
    B, T, _ = qkv.shape
    L = min(CHUNK, T)
    nt = T // L
    pad = lambda a: jnp.concatenate([a, jnp.zeros((LANES - a.shape[0],), F32)]).reshape(1, LANES)
    alog = pad(a_log)
    dtb = pad(dt_bias)
    ng = jnp.tile(norm_g, N_HEADS).reshape(1, GROUP_W)
    hid = jnp.arange(GROUP_W) // HEAD_D
    hsum = (hid[:, None] == hid[None, :]).astype(BF16)
    bb = GDN_BATCH_PER_STEP

    def row(f):
        return pl.BlockSpec((bb, L, f), lambda bi, i: (bi, i, 0))

    state = pl.BlockSpec((bb, N_HEADS, HEAD_D, HEAD_D), lambda bi, i: (bi, 0, 0, 0))
    kern = functools.partial(_gdn_kernel, L=L, nt=nt, bb=bb)
    return pl.pallas_call(
        kern, grid=(B // bb, nt),
        in_specs=[row(N_GDN_QKV), row(GZ_PAD),
                  pl.BlockSpec((bb, CONV_W - 1, N_GDN_QKV), lambda bi, i: (bi, 0, 0)), state,
                  _full_spec(conv_w.shape), _full_spec(alog.shape), _full_spec(dtb.shape),
                  _full_spec(ng.shape), _full_spec(hsum.shape)],
        out_specs=[row(GROUP_W), state],
        out_shape=[jax.ShapeDtypeStruct((B, T, GROUP_W), F32),
                   jax.ShapeDtypeStruct((B, N_HEADS, HEAD_D, HEAD_D), F32)],
        scratch_shapes=[pltpu.VMEM((bb, 8 + L, N_GDN_QKV), F32), pltpu.VMEM((bb, N_HEADS, HEAD_D, HEAD_D), F32),
                        pltpu.VMEM((bb, L, GROUP_W), F32)],
        compiler_params=_params(("parallel", "arbitrary")), name="gdn",
    )(qkv, gz, conv0, s0, conv_w, alog, dtb, ng, hsum)


def _s5_kernel(u_ref, h0r_ref, h0i_ref, lr_ref, li_ref, br_ref, bi_ref, cr_ref, ci_ref, d_ref, wg_ref,
               o_ref, hr_out_ref, hi_out_ref, hr_scr, hi_scr, xr_scr, xi_scr, *, tt, nb, nt):
    ti = pl.program_id(0)

    @pl.when(ti == 0)
    def _():
        hr_scr[...] = h0r_ref[...]
        hi_scr[...] = h0i_ref[...]

    u = u_ref[...]
    ub = u.astype(BF16)
    xr_scr[...] = _dot(ub, br_ref[...])
    xi_scr[...] = _dot(ub, bi_ref[...])
    lr = jnp.broadcast_to(lr_ref[...], (nb, S5_STATE))
    li = jnp.broadcast_to(li_ref[...], (nb, S5_STATE))

    def step(t, carry):
        hr, hi = carry
        r0 = pl.multiple_of(t * nb, nb)
        nr = lr * hr - li * hi + xr_scr[pl.ds(r0, nb), :]
        ni = lr * hi + li * hr + xi_scr[pl.ds(r0, nb), :]
        xr_scr[pl.ds(r0, nb), :] = nr
        xi_scr[pl.ds(r0, nb), :] = ni
        return nr, ni

    hr, hi = lax.fori_loop(0, tt, step, (hr_scr[...], hi_scr[...]), unroll=8)
    hr_scr[...] = hr
    hi_scr[...] = hi
    y = _dot(xr_scr[...].astype(BF16), cr_ref[...]) - _dot(xi_scr[...].astype(BF16), ci_ref[...]) + d_ref[...] * u
    act = jax.nn.gelu(y)
    o_ref[...] = act * jax.nn.sigmoid(_dot(act.astype(BF16), wg_ref[...]))

    @pl.when(ti == nt - 1)
    def _():
        hr_out_ref[...] = hr
        hi_out_ref[...] = hi


def _prep_s5_weights(lam_re, lam_im, log_step, b_re, b_im, c_re, c_im, d, w_glu):
    dt = jnp.exp(log_step)[:, None]
    mag = jnp.exp(lam_re * dt)
    bar_re, bar_im = mag * jnp.cos(lam_im * dt), mag * jnp.sin(lam_im * dt)
    den = lam_re * lam_re + lam_im * lam_im
    f_re = ((bar_re - 1.0) * lam_re + bar_im * lam_im) / den
    f_im = (bar_im * lam_re - (bar_re - 1.0) * lam_im) / den
    bb_re = f_re[..., None] * b_re - f_im[..., None] * b_im
    bb_im = f_re[..., None] * b_im + f_im[..., None] * b_re
    eye = jnp.eye(S5_G, dtype=F32)

    def in_mat(b):
        return jnp.einsum('gpc,gh->gchp', b, eye).reshape(S5_G * S5_CG, S5_STATE).astype(BF16)

    def out_mat(c):
        return jnp.einsum('gcp,gh->gphc', c, eye).reshape(S5_STATE, S5_G * S5_CG).astype(BF16)

    return (bar_re.reshape(1, S5_STATE), bar_im.reshape(1, S5_STATE),
            in_mat(bb_re), in_mat(bb_im), out_mat(c_re), out_mat(c_im),
            d.reshape(1, GROUP_W), w_glu.astype(BF16))


def _s5(u_tm, h0_re, h0_im, weights, B, tt):
    T = u_tm.shape[0]
    u2 = u_tm.reshape(T * B, GROUP_W)
    nt = T // tt
    lr, li, br, bi, cr, ci, d, wg = weights
    rows = pl.BlockSpec((tt * B, GROUP_W), lambda i: (i, 0))
    st = _full_spec((B, S5_STATE))
    kern = functools.partial(_s5_kernel, tt=tt, nb=B, nt=nt)
    o, hr, hi = pl.pallas_call(
        kern, grid=(nt,),
        in_specs=[rows, st, st] + [_full_spec(w.shape) for w in weights],
        out_specs=[rows, st, st],
        out_shape=[jax.ShapeDtypeStruct((T * B, GROUP_W), F32), jax.ShapeDtypeStruct((B, S5_STATE), F32),
                   jax.ShapeDtypeStruct((B, S5_STATE), F32)],
        scratch_shapes=[pltpu.VMEM((B, S5_STATE), F32), pltpu.VMEM((B, S5_STATE), F32),
                        pltpu.VMEM((tt * B, S5_STATE), F32), pltpu.VMEM((tt * B, S5_STATE), F32)],
        compiler_params=_params(("arbitrary",)), name="s5",
    )(u2, h0_re.reshape(B, S5_STATE), h0_im.reshape(B, S5_STATE), *weights)
    return o.reshape(T, B * GROUP_W), hr.reshape(B, S5_G, S5_P), hi.reshape(B, S5_G, S5_P)


def _outproj_kernel(oa_ref, ob_ref, oc_ref, od_ref, x_ref, gn_ref, w_ref, g_ref, b_ref, o_ref, *, alpha):
    gn = gn_ref[...]
    mix = jnp.concatenate([
        _rms_rows(oa_ref[...], gn[0:1]).astype(BF16),
        _rms_rows(ob_ref[...], gn[1:2]).astype(BF16),
        oc_ref[...].astype(BF16),
        _rms_rows(od_ref[...], gn[2:3]).astype(BF16)], -1)
    y = alpha * x_ref[...] + _dot(mix, w_ref[...])
    o_ref[...] = _ln_rows(y, g_ref[...], b_ref[...])


def _outproj(oa, ob, oc, od_tm, x, gn, w_out, g, b, alpha, tm):
    B, T, D = x.shape

    def row(f):
        return pl.BlockSpec((None, tm, f), lambda bi, i: (bi, i, 0))

    kern = functools.partial(_outproj_kernel, alpha=alpha)
    return pl.pallas_call(
        kern, grid=(B, T // tm),
        in_specs=[row(GROUP_W), row(GROUP_W), row(GROUP_W), pl.BlockSpec((tm, GROUP_W), lambda bi, i: (i, bi)),
                  row(D), _full_spec(gn.shape), _full_spec(w_out.shape), _full_spec((1, D)), _full_spec((1, D))],
        out_specs=row(D), out_shape=jax.ShapeDtypeStruct((B, T, D), F32),
        compiler_params=_params(("parallel", "parallel")), name="outproj",
    )(oa, ob, oc, od_tm, x, gn, w_out, g.reshape(1, D), b.reshape(1, D))


def _ffn_kernel(x_ref, wg_ref, wu_ref, wd_ref, g_ref, b_ref, o_ref, *, alpha):
    x = x_ref[...]
    xb = x.astype(BF16)
    hid = (jax.nn.silu(_dot(xb, wg_ref[...])) * _dot(xb, wu_ref[...])).astype(BF16)
    y = alpha * x + _dot(hid, wd_ref[...])
    o_ref[...] = _ln_rows(y, g_ref[...], b_ref[...])


def _ffn(x, wg, wu, wd, g, b, alpha, tm):
    B, T, D = x.shape
    row = pl.BlockSpec((None, tm, D), lambda bi, i: (bi, i, 0))

    def resident(shape):
        return pl.BlockSpec(shape, lambda *_: (0,) * len(shape), pipeline_mode=pl.Buffered(1))

    kern = functools.partial(_ffn_kernel, alpha=alpha)
    return pl.pallas_call(
        kern, grid=(B, T // tm),
        in_specs=[row, resident(wg.shape), resident(wu.shape), resident(wd.shape),
                  _full_spec((1, D)), _full_spec((1, D))],
        out_specs=row, out_shape=jax.ShapeDtypeStruct((B, T, D), F32),
        compiler_params=_params(("parallel", "parallel")), name="ffn",
    )(x, wg, wu, wd, g.reshape(1, D), b.reshape(1, D))


def _router_kernel(x_ref, wr_ref, br_ref, idx_ref, gate_ref):
    logits = _dot_nt(wr_ref[...], x_ref[...], HIGHEST) + br_ref[...]
    eid = lax.broadcasted_iota(jnp.int32, logits.shape, 0)
    m1 = jnp.max(logits, 0, keepdims=True)
    i1 = jnp.min(jnp.where(logits == m1, eid, N_EXP), 0, keepdims=True)
    rest = jnp.where(eid == i1, -jnp.inf, logits)
    m2 = jnp.max(rest, 0, keepdims=True)
    i2 = jnp.min(jnp.where(rest == m2, eid, N_EXP), 0, keepdims=True)
    e2 = jnp.exp(m2 - m1)
    den = 1.0 + e2
    idx_ref[...] = jnp.concatenate([i1, i2], 0)
    gate_ref[...] = jnp.concatenate([1.0 / den, e2 / den], 0)


def _router(x2, w_router, b_router, tm):
    N, D = x2.shape
    return pl.pallas_call(
        _router_kernel, grid=(N // tm,),
        in_specs=[pl.BlockSpec((tm, D), lambda i: (i, 0)), _full_spec((N_EXP, D)), _full_spec((N_EXP, 1))],
        out_specs=[pl.BlockSpec((TOP_K, tm), lambda i: (0, i)), pl.BlockSpec((TOP_K, tm), lambda i: (0, i))],
        out_shape=[jax.ShapeDtypeStruct((TOP_K, N), jnp.int32), jax.ShapeDtypeStruct((TOP_K, N), F32)],
        compiler_params=_params(("parallel",)), name="moe_router",
    )(x2, w_router.T, b_router.reshape(N_EXP, 1))


def _expert_kernel(blk_exp_ref, src_ref, src_next_ref, dst_ref, x_hbm, wg_ref, wu_ref, wd_ref, out_hbm,
                   xbuf, ybuf, gsem, ssem, *, n_blk):
    del blk_exp_ref
    i = pl.program_id(0)
    n_sub = MOE_BLK // MOE_SUB

    def rows(j):
        return pl.ds(j * MOE_SUB, MOE_SUB)

    def gather(j, idx_ref):
        for r in range(j * MOE_SUB, (j + 1) * MOE_SUB):
            pltpu.make_async_copy(x_hbm.at[pl.ds(idx_ref[0, 0, r], 1)], xbuf.at[pl.ds(r, 1)], gsem.at[j]).start()

    def scatter(j):
        for r in range(j * MOE_SUB, (j + 1) * MOE_SUB):
            pltpu.make_async_copy(ybuf.at[pl.ds(r, 1)], out_hbm.at[pl.ds(dst_ref[0, 0, r], 1)], ssem.at[j]).start()

    def wait_gather(j):
        pltpu.make_async_copy(x_hbm.at[pl.ds(0, MOE_SUB)], xbuf.at[rows(j)], gsem.at[j]).wait()

    def wait_scatter(j):
        pltpu.make_async_copy(ybuf.at[rows(j)], out_hbm.at[pl.ds(0, MOE_SUB)], ssem.at[j]).wait()

    @pl.when(i == 0)
    def _():
        gather(0, src_ref)

    for j in range(n_sub):
        wait_gather(j)
        if j == n_sub - 1:
            @pl.when(i > 0)
            def _():
                wait_scatter(n_sub - 1)
        if j + 1 < n_sub:
            gather(j + 1, src_ref)
        if j == 1:
            gather(0, src_next_ref)
        if j >= 1:
            scatter(j - 1)
        xb = xbuf[rows(j), :].astype(BF16)
        hid = (jax.nn.silu(_dot(xb, wg_ref[...])) * _dot(xb, wu_ref[...])).astype(BF16)
        ybuf[rows(j), :] = _dot(hid, wd_ref[...])
    scatter(n_sub - 1)
    for j in range(n_sub - 1):
        wait_scatter(j)

    @pl.when(i == n_blk - 1)
    def _():
        wait_scatter(n_sub - 1)
        wait_gather(0)


def _combine_kernel(x_ref, y0_ref, y1_ref, gate_ref, g_ref, b_ref, o_ref, *, alpha):
    gate = gate_ref[...]
    f = y0_ref[...] * gate[:, 0:1] + y1_ref[...] * gate[:, 1:2]
    o_ref[...] = _ln_rows(alpha * x_ref[...] + f, g_ref[...], b_ref[...])


def _moe(x, w_router, b_router, wg, wu, wd, g, b, alpha):
    B, T, D = x.shape
    N = B * T
    x2 = x.reshape(N, D)
    tm = math.gcd(N, MOE_BLK)
    idx, gates = _router(x2, w_router, b_router, tm)

    n_slots = N * TOP_K
    e_flat = idx.T.reshape(-1)
    onehot = (e_flat[:, None] == jnp.arange(N_EXP, dtype=jnp.int32)[None, :]).astype(jnp.int32)
    csum = jnp.cumsum(onehot, 0)
    counts = csum[-1]
    rank = jnp.sum((csum - onehot) * onehot, 1)
    padded = (counts + MOE_BLK - 1) // MOE_BLK * MOE_BLK
    pad_end = jnp.cumsum(padded)
    pad_start = pad_end - padded
    dest = pad_start[e_flat] + rank
    n_blk = -(-(n_slots + N_EXP * (MOE_BLK - 1)) // MOE_BLK)
    rows = n_blk * MOE_BLK
    row_f = jnp.full((rows,), n_slots, jnp.int32).at[dest].set(jnp.arange(n_slots, dtype=jnp.int32))
    row_src = jnp.minimum(row_f // TOP_K, N - 1)
    row_dst = jnp.where(row_f < n_slots, (row_f % TOP_K) * N + row_f // TOP_K,
                        n_slots + jnp.arange(rows, dtype=jnp.int32) % MOE_BLK)
    blk_start = jnp.arange(n_blk, dtype=jnp.int32) * MOE_BLK
    blk_exp = jnp.minimum(jnp.sum((pad_end[None, :] <= blk_start[:, None]).astype(jnp.int32), 1), N_EXP - 1)

    idx_spec = pl.BlockSpec((1, 1, MOE_BLK), lambda i, be: (i, 0, 0), memory_space=pltpu.SMEM)
    next_spec = pl.BlockSpec((1, 1, MOE_BLK), lambda i, be: (jnp.minimum(i + 1, n_blk - 1), 0, 0),
                             memory_space=pltpu.SMEM)
    y = pl.pallas_call(
        functools.partial(_expert_kernel, n_blk=n_blk),
        grid_spec=pltpu.PrefetchScalarGridSpec(
            num_scalar_prefetch=1, grid=(n_blk,),
            in_specs=[idx_spec, next_spec, idx_spec,
                      pl.BlockSpec(memory_space=pl.ANY),
                      pl.BlockSpec((None, D, D_FF_EXP), lambda i, be: (be[i], 0, 0)),
                      pl.BlockSpec((None, D, D_FF_EXP), lambda i, be: (be[i], 0, 0)),
                      pl.BlockSpec((None, D_FF_EXP, D), lambda i, be: (be[i], 0, 0))],
            out_specs=pl.BlockSpec(memory_space=pl.ANY),
            scratch_shapes=[pltpu.VMEM((MOE_BLK, D), F32), pltpu.VMEM((MOE_BLK, D), F32),
                            pltpu.SemaphoreType.DMA((MOE_BLK // MOE_SUB,)),
                            pltpu.SemaphoreType.DMA((MOE_BLK // MOE_SUB,))]),
        out_shape=jax.ShapeDtypeStruct((n_slots + MOE_BLK, D), F32),
        compiler_params=_params(("arbitrary",)), name="moe_experts",
    )(blk_exp, row_src.reshape(n_blk, 1, MOE_BLK), row_src.reshape(n_blk, 1, MOE_BLK),
      row_dst.reshape(n_blk, 1, MOE_BLK), x2, wg, wu, wd)

    kern = functools.partial(_combine_kernel, alpha=alpha)
    rows_spec = pl.BlockSpec((tm, D), lambda i: (i, 0))
    out = pl.pallas_call(
        kern, grid=(N // tm,),
        in_specs=[rows_spec, rows_spec, pl.BlockSpec((tm, D), lambda i: (i + N // tm, 0)),
                  pl.BlockSpec((tm, TOP_K), lambda i: (i, 0)), _full_spec((1, D)), _full_spec((1, D))],
        out_specs=rows_spec, out_shape=jax.ShapeDtypeStruct((N, D), F32),
        compiler_params=_params(("parallel",)), name="moe_combine",
    )(x2, y, y, gates.T, g.reshape(1, D), b.reshape(1, D))
    return out.reshape(B, T, D)


def _round_up(n, m):
    return -(-n // m) * m


def _run_trunk(x, states, P, depth):
    lat_c, kr_c, sbk_c, sbv_c, conv_c, ssm_c, re_c, im_c = states
    B, T, D = x.shape
    past = lat_c.shape[2]
    alpha = (2 * depth) ** 0.25
    long_seq = T >= 512
    tm = 512 if long_seq else T
    tq = ATTN_TQ if long_seq else T
    tk = ATTN_TK if long_seq else 2 * LANES
    S = past + T
    S_pad = _round_up(S, tk)

    tab = _rope_table(past, T)
    x = _layer_norm(x, P['ln_in_g'], P['ln_in_b'], tm)
    outs = [[] for _ in range(8)]
    for l in range(depth):
        pm, sbq, sbk, sbv, sbk_b, sbv_b, gq, gz, s5u = _inproj(x, _prep_inproj_weights(P['w_in'][l]), tm)

        qn, kvn, wq, wqr, wk, wv, e = _prep_mla_weights(P['mla_q_norm'][l], P['mla_kv_norm'][l],
                                                        P['mla_w_uq'][l], P['mla_w_ukv'][l])
        q_a, lat_n, kr_n, krp_n = _mla_prep(pm, tab, qn, kvn, wq, wqr, tm)
        if past or S_pad != S:
            kr_cache = jnp.pad(kr_c[l], ((0, 0), (0, 0), (0, LANES - ROPE)))
            lat_all = jnp.pad(jnp.concatenate([lat_c[l], lat_n], 1), ((0, 0), (0, S_pad - S), (0, 0)))
            krp_all = jnp.pad(jnp.concatenate([kr_cache, krp_n], 1), ((0, 0), (0, S_pad - S), (0, 0)))
        else:
            lat_all, krp_all = lat_n, krp_n
        k_a, v_a = _mla_kv(lat_all, krp_all, wk, wv, e, tk)
        o_a = _mla_attn(q_a, k_a, v_a, past, tq, tk)

        if past or S_pad != S:
            k_all = jnp.pad(jnp.concatenate([sbk_c[l].reshape(B, past, GROUP_W).astype(BF16), sbk_b], 1),
                            ((0, 0), (0, S_pad - S), (0, 0)))
            v_all = jnp.pad(jnp.concatenate([sbv_c[l].reshape(B, past, GROUP_W).astype(BF16), sbv_b], 1),
                            ((0, 0), (0, S_pad - S), (0, 0)))
        else:
            k_all, v_all = sbk_b, sbv_b
        o_b = _sb_attn(sbq, k_all, v_all, past, tq, tk)

        o_c, ssm_n = _gdn(gq, gz, conv_c[l], ssm_c[l], P['gdn_conv_w'][l], P['gdn_a_log'][l],
                          P['gdn_dt_bias'][l], P['gdn_norm'][l])
        conv_n = jnp.concatenate([conv_c[l], gq], 1)[:, T:]

        s5w = _prep_s5_weights(P['s5_lam_re'][l], P['s5_lam_im'][l], P['s5_log_step'][l], P['s5_b_re'][l],
                               P['s5_b_im'][l], P['s5_c_re'][l], P['s5_c_im'][l], P['s5_d'][l], P['s5_w_glu'][l])
        o_d, re_n, im_n = _s5(s5u, re_c[l], im_c[l], s5w, B, min(64, T))

        x = _outproj(o_a, o_b, o_c, o_d, x, P['grp_norm'][l], P['w_out'][l].astype(BF16),
                     P['ln1_g'][l], P['ln1_b'][l], alpha, tm)
        i = l // 2
        if l % 2 == 0:
            x = _ffn(x, P['ffn_w_gate'][i].astype(BF16), P['ffn_w_up'][i].astype(BF16),
                     P['ffn_w_down'][i].astype(BF16), P['ln2_g'][l], P['ln2_b'][l], alpha, min(tm, 256))
        else:
            x = _moe(x, P['moe_w_router'][i], P['moe_b_router'][i], P['moe_w_gate'][i].astype(BF16),
                     P['moe_w_up'][i].astype(BF16), P['moe_w_down'][i].astype(BF16),
                     P['ln2_g'][l], P['ln2_b'][l], alpha)
        news = (lat_n, kr_n, sbk.reshape(B, T, N_HEADS, HEAD_D), sbv.reshape(B, T, N_HEADS, HEAD_D),
                conv_n, ssm_n, re_n, im_n)
        for lst, a in zip(outs, news):
            lst.append(a)
    return x, [jnp.stack(lst) for lst in outs]


def kernel(x_prompt, x_sample, cache_mla_latent, cache_mla_krope, cache_sb_k, cache_sb_v,
           state_gdn_conv, state_gdn_ssm, state_s5_re, state_s5_im,
           ln_in_g, ln_in_b, w_in, mla_q_norm, mla_kv_norm, mla_w_uq, mla_w_ukv,
           gdn_conv_w, gdn_a_log, gdn_dt_bias, gdn_norm,
           s5_lam_re, s5_lam_im, s5_log_step, s5_b_re, s5_b_im, s5_c_re, s5_c_im, s5_d, s5_w_glu,
           grp_norm, w_out, ln1_g, ln1_b, ln2_g, ln2_b,
           ffn_w_gate, ffn_w_up, ffn_w_down,
           moe_w_router, moe_b_router, moe_w_gate, moe_w_up, moe_w_down):
    P = dict(ln_in_g=ln_in_g, ln_in_b=ln_in_b, w_in=w_in, mla_q_norm=mla_q_norm, mla_kv_norm=mla_kv_norm,
             mla_w_uq=mla_w_uq, mla_w_ukv=mla_w_ukv, gdn_conv_w=gdn_conv_w, gdn_a_log=gdn_a_log,
             gdn_dt_bias=gdn_dt_bias, gdn_norm=gdn_norm, s5_lam_re=s5_lam_re, s5_lam_im=s5_lam_im,
             s5_log_step=s5_log_step, s5_b_re=s5_b_re, s5_b_im=s5_b_im, s5_c_re=s5_c_re, s5_c_im=s5_c_im,
             s5_d=s5_d, s5_w_glu=s5_w_glu, grp_norm=grp_norm, w_out=w_out, ln1_g=ln1_g, ln1_b=ln1_b,
             ln2_g=ln2_g, ln2_b=ln2_b, ffn_w_gate=ffn_w_gate, ffn_w_up=ffn_w_up, ffn_w_down=ffn_w_down,
             moe_w_router=moe_w_router, moe_b_router=moe_b_router, moe_w_gate=moe_w_gate,
             moe_w_up=moe_w_up, moe_w_down=moe_w_down)
    depth = w_in.shape[0]
    Bp = x_prompt.shape[0]
    dt = x_prompt.dtype
    prompt_states = (jnp.zeros((depth, Bp, 0, KV_LORA), dt), jnp.zeros((depth, Bp, 0, ROPE), dt),
                     jnp.zeros((depth, Bp, 0, N_HEADS, HEAD_D), dt), jnp.zeros((depth, Bp, 0, N_HEADS, HEAD_D), dt),
                     jnp.zeros((depth, Bp, CONV_W - 1, N_GDN_QKV), dt),
                     jnp.zeros((depth, Bp, N_HEADS, HEAD_D, HEAD_D), F32),
                     jnp.zeros((depth, Bp, S5_G, S5_P), F32), jnp.zeros((depth, Bp, S5_G, S5_P), F32))
    y_prompt, st_p = _run_trunk(x_prompt, prompt_states, P, depth)
    sample_states = (cache_mla_latent, cache_mla_krope, cache_sb_k, cache_sb_v,
                     state_gdn_conv, state_gdn_ssm, state_s5_re, state_s5_im)
    y_sample, st_s = _run_trunk(x_sample, sample_states, P, depth)
    return (y_prompt, y_sample, *st_p, *st_s)
```

```python
import functools
import math

import jax
import jax.numpy as jnp
from jax import lax
from jax.experimental import pallas as pl
from jax.experimental.pallas import tpu as pltpu

F32 = jnp.float32
BF16 = jnp.bfloat16
HIGHEST = lax.Precision.HIGHEST

D_MODEL = 1024
CHUNK = 64
GROUP_W = 256
N_HEADS = 4
HEAD_D = 64
NOPE = 64
ROPE = 32
Q_LORA = 192
KV_LORA = 128
ROPE_THETA = 10000.0
MLA_SCALE = (NOPE + ROPE) ** -0.5
SB_SCALE = HEAD_D ** -0.5
LOG2E = math.log2(math.e)
CONV_W = 4
N_GDN_QKV = 768
S5_CG = 16
S5_G = 16
S5_P = 64
S5_STATE = S5_G * S5_P
D_FF = 2816
N_EXP = 8
TOP_K = 2
D_FF_EXP = 1792
MOE_BLK = 512
LN_EPS = 1e-5
RMS_EPS = 1e-6
NEG_INF = -1e30
OFF_SB = Q_LORA + KV_LORA + ROPE
OFF_GDN = OFF_SB + 3 * GROUP_W
OFF_S5 = OFF_GDN + N_GDN_QKV + 2 * N_HEADS + GROUP_W

LANES = 128
HEAD_PAD = LANES
VMEM_LIMIT = 48 * 1024 * 1024
GDN_BATCH_PER_STEP = 8
MOE_SUB = 128
ATTN_TQ = 512
ATTN_TK = 512
MLA_TILE = (256, 512)
SB_TILE = (512, 256)


def _params(sem):
    return pltpu.CompilerParams(dimension_semantics=sem, vmem_limit_bytes=VMEM_LIMIT)


def _dot(a, b, precision=None):
    return jnp.dot(a, b, preferred_element_type=F32, precision=precision)


def _dot_nt(a, b, precision=None):
    return lax.dot_general(a, b, (((1,), (1,)), ((), ())), preferred_element_type=F32, precision=precision)


def _dot_tn(a, b, precision=None):
    return lax.dot_general(a, b, (((0,), (0,)), ((), ())), preferred_element_type=F32, precision=precision)


def _ln_rows(x, g, b):
    mu = jnp.mean(x, -1, keepdims=True)
    xc = x - mu
    var = jnp.mean(xc * xc, -1, keepdims=True)
    return xc * lax.rsqrt(var + LN_EPS) * g + b


def _rms_rows(x, g, n=None):
    n = x.shape[-1] if n is None else n
    ms = jnp.sum(x * x, -1, keepdims=True) * (1.0 / n)
    return x * lax.rsqrt(ms + RMS_EPS) * g


def _full_spec(shape):
    nd = len(shape)
    return pl.BlockSpec(shape, lambda *_: (0,) * nd)


MLA_IN_PAD = 640
GZ_PAD = 384


def _inproj_kernel(x_ref, lng_ref, lnb_ref, tab_ref, qn_ref, kvn_ref, wq_ref, wqr_ref,
                   wm_ref, wsb_ref, wgq_ref, wgz_ref, ws5_ref, *out_refs, apply_ln):
    if apply_ln:
        xln_ref, *out_refs = out_refs
    (q_ref, lat_ref, kr_ref, krp_ref, sbq_ref, sbk_ref, sbv_ref, sbkb_ref, sbvb_ref,
     gq_ref, gz_ref, s5_ref) = out_refs
    x = x_ref[...]
    if apply_ln:
        x = _ln_rows(x, lng_ref[...], lnb_ref[...])
        xln_ref[...] = x
    xb = x.astype(BF16)
    _mla_prep_rows(_dot(xb, wm_ref[...]), tab_ref[...], qn_ref, kvn_ref, wq_ref, wqr_ref,
                   q_ref, lat_ref, kr_ref, krp_ref)
    sb = _dot(xb, wsb_ref[...])
    sbq_ref[...] = (sb[:, :GROUP_W] * (SB_SCALE * LOG2E)).astype(BF16)
    sbk_ref[...] = sb[:, GROUP_W:2 * GROUP_W]
    sbv_ref[...] = sb[:, 2 * GROUP_W:]
    sbkb_ref[...] = sb[:, GROUP_W:2 * GROUP_W].astype(BF16)
    sbvb_ref[...] = sb[:, 2 * GROUP_W:].astype(BF16)
    gq_ref[...] = _dot(xb, wgq_ref[...])
    gz_ref[...] = _dot(xb, wgz_ref[...])
    s5_ref[...] = _dot(xb, ws5_ref[...])


def _rotate_half_cols(w):
    half = w.shape[-1] // 2
    return jnp.concatenate([-w[..., half:], w[..., :half]], -1)


def _prep_inproj_weights(w_in):
    zeros = lambda n: jnp.zeros((D_MODEL, n), F32)
    w_cq = w_in[:, :Q_LORA]
    w_ckv = w_in[:, Q_LORA:Q_LORA + KV_LORA]
    w_kr = w_in[:, Q_LORA + KV_LORA:OFF_SB]
    wm = jnp.concatenate([w_ckv, w_cq, zeros(64), w_kr, zeros(96), _rotate_half_cols(w_kr), zeros(96)], 1)
    wsb = w_in[:, OFF_SB:OFF_GDN]
    wgq = w_in[:, OFF_GDN:OFF_GDN + N_GDN_QKV]
    o_a = OFF_GDN + N_GDN_QKV
    wgz = jnp.concatenate([w_in[:, o_a + 2 * N_HEADS:OFF_S5], w_in[:, o_a:o_a + 2 * N_HEADS],
                           zeros(GZ_PAD - GROUP_W - 2 * N_HEADS)], 1)
    ws5 = w_in[:, OFF_S5:]
    return tuple(w.astype(BF16) for w in (wm, wsb, wgq, wgz, ws5))


def _inproj(x, ln_g, ln_b, apply_ln, tab, mla_weights, weights, tm):
    B, T, D = x.shape
    qn, kvn, wq, wqr = mla_weights
    consts = (ln_g.reshape(1, D), ln_b.reshape(1, D))
    resident = (qn, kvn, wq, wqr) + tuple(weights)

    def row(f):
        return pl.BlockSpec((None, tm, f), lambda bi, i: (bi, i, 0))

    def out(f, dt=F32):
        return jax.ShapeDtypeStruct((B, T, f), dt)

    out_specs = [row(N_HEADS * HEAD_PAD), row(KV_LORA), row(ROPE), row(LANES),
                 row(GROUP_W), row(GROUP_W), row(GROUP_W), row(GROUP_W), row(GROUP_W),
                 row(N_GDN_QKV), row(GZ_PAD), pl.BlockSpec((tm, GROUP_W), lambda bi, i: (i, bi))]
    out_shape = [out(N_HEADS * HEAD_PAD, BF16), out(KV_LORA), out(ROPE), out(LANES),
                 out(GROUP_W, BF16), out(GROUP_W), out(GROUP_W), out(GROUP_W, BF16), out(GROUP_W, BF16),
                 out(N_GDN_QKV), out(GZ_PAD), jax.ShapeDtypeStruct((T, B * GROUP_W), F32)]
    if apply_ln:
        out_specs, out_shape = [row(D)] + out_specs, [out(D)] + out_shape
    return pl.pallas_call(
        functools.partial(_inproj_kernel, apply_ln=apply_ln), grid=(B, T // tm),
        in_specs=[row(D)] + [_full_spec(c.shape) for c in consts]
        + [pl.BlockSpec((tm, 4 * LANES), lambda bi, i: (i, 0))] + [_full_spec(w.shape) for w in resident],
        out_specs=out_specs, out_shape=out_shape,
        compiler_params=_params(("parallel", "parallel")), name="inproj",
    )(x, *consts, tab, *resident)


def _mla_prep_rows(pm, tab, qn_ref, kvn_ref, wq_ref, wqr_ref, q_ref, lat_ref, kr_ref, krp_ref):
    c_kv = pm[:, :KV_LORA]
    c_q = pm[:, KV_LORA:KV_LORA + 2 * LANES]
    k_r = pm[:, 3 * LANES:4 * LANES]
    k_rr = pm[:, 4 * LANES:5 * LANES]
    lat_ref[...] = _rms_rows(c_kv, kvn_ref[...])
    nq = _rms_rows(c_q, qn_ref[...], n=Q_LORA).astype(BF16)
    qp = _dot(nq, wq_ref[...])
    qr = _dot(nq, wqr_ref[...])
    cq, sq = tab[:, :LANES], tab[:, LANES:2 * LANES]
    for h in range(N_HEADS):
        sl = slice(h * HEAD_PAD, (h + 1) * HEAD_PAD)
        q_ref[:, sl] = ((qp[:, sl] * cq + qr[:, sl] * sq) * (MLA_SCALE * LOG2E)).astype(BF16)
    kr_new = k_r * tab[:, 2 * LANES:3 * LANES] + k_rr * tab[:, 3 * LANES:]
    krp_ref[...] = kr_new
    kr_ref[...] = kr_new[:, :ROPE]


def _rope_table(past, T):
    half = ROPE // 2
    inv = ROPE_THETA ** (-jnp.arange(half, dtype=F32) / half)
    ang = (past + jnp.arange(T, dtype=jnp.int32)).astype(F32)[:, None] * inv
    c, s = jnp.cos(ang), jnp.sin(ang)
    one, zero = jnp.ones((T, NOPE), F32), jnp.zeros((T, NOPE), F32)
    z32, z96 = jnp.zeros((T, 32), F32), jnp.zeros((T, 96), F32)
    return jnp.concatenate([one, c, c, z32, zero, s, s, z32, c, c, z96, s, s, z96], 1)


def _prep_mla_weights(q_norm, kv_norm, w_uq, w_ukv):
    w3 = w_uq.reshape(Q_LORA, N_HEADS, NOPE + ROPE)
    zq = jnp.zeros((Q_LORA, N_HEADS, HEAD_PAD - NOPE - ROPE), F32)
    wq = jnp.concatenate([w3, zq], -1).reshape(Q_LORA, N_HEADS * HEAD_PAD)
    w3r = jnp.concatenate([jnp.zeros((Q_LORA, N_HEADS, NOPE), F32), _rotate_half_cols(w3[..., NOPE:]), zq], -1)
    wqr = w3r.reshape(Q_LORA, N_HEADS * HEAD_PAD)
    pad_rows = jnp.zeros((2 * LANES - Q_LORA, N_HEADS * HEAD_PAD), F32)
    wq = jnp.concatenate([wq, pad_rows], 0).astype(BF16)
    wqr = jnp.concatenate([wqr, pad_rows], 0).astype(BF16)
    qn = jnp.concatenate([q_norm, jnp.zeros((2 * LANES - Q_LORA,), F32)]).reshape(1, 2 * LANES)
    kvn = kv_norm.reshape(1, KV_LORA)
    kv3 = w_ukv.reshape(KV_LORA, N_HEADS, NOPE + HEAD_D)
    zk = jnp.zeros((KV_LORA, N_HEADS, HEAD_PAD - NOPE), F32)
    wk = jnp.concatenate([kv3[..., :NOPE], zk], -1).reshape(KV_LORA, N_HEADS * HEAD_PAD).astype(BF16)
    wv = kv3[..., NOPE:].reshape(KV_LORA, GROUP_W).astype(BF16)
    e = jnp.zeros((LANES, N_HEADS, HEAD_PAD), F32)
    e = e.at[jnp.arange(ROPE)[:, None], jnp.arange(N_HEADS)[None, :], NOPE + jnp.arange(ROPE)[:, None]].set(1.0)
    e = e.reshape(LANES, N_HEADS * HEAD_PAD).astype(BF16)
    return qn, kvn, wq, wqr, wk, wv, e


def _mla_kv_kernel(lat_ref, krp_ref, wk_ref, wv_ref, e_ref, k_ref, v_ref):
    lb = lat_ref[...].astype(BF16)
    kb = krp_ref[...].astype(BF16)
    k_ref[...] = (_dot(lb, wk_ref[...]) + _dot(kb, e_ref[...])).astype(BF16)
    v_ref[...] = _dot(lb, wv_ref[...]).astype(BF16)


def _mla_kv(lat, krp, wk, wv, e, tm):
    B, S, _ = lat.shape

    def row(f):
        return pl.BlockSpec((None, tm, f), lambda bi, i: (bi, i, 0))

    return pl.pallas_call(
        _mla_kv_kernel, grid=(B, S // tm),
        in_specs=[row(KV_LORA), row(LANES), _full_spec(wk.shape), _full_spec(wv.shape), _full_spec(e.shape)],
        out_specs=[row(N_HEADS * HEAD_PAD), row(GROUP_W)],
        out_shape=[jax.ShapeDtypeStruct((B, S, N_HEADS * HEAD_PAD), BF16),
                   jax.ShapeDtypeStruct((B, S, GROUP_W), BF16)],
        compiler_params=_params(("parallel", "parallel")), name="mla_kv",
    )(lat, krp, wk, wv, e)


def _lane_tile(a, n):
    return a if n == 1 else jnp.concatenate([a] * n, axis=1)


def _per_head_lanes(stats):
    low = lax.broadcasted_iota(jnp.int32, stats[0].shape, 1) < HEAD_D
    return jnp.concatenate([jnp.where(low, stats[0], stats[1]), jnp.where(low, stats[2], stats[3])], axis=1)


def _block_diag_values(vc):
    head = lax.broadcasted_iota(jnp.int32, vc.shape, 1) // HEAD_D
    return jnp.concatenate([jnp.where(head == h, vc, jnp.zeros_like(vc)) for h in range(N_HEADS)], axis=0)


def _mla_attn_kernel(q_ref, k_ref, v_ref, o_ref, m_scr, l_scr, acc_scr, *, tq, tk, rq, ck, past, nkv):
    qi = pl.program_id(1)
    kj = pl.program_id(2)
    delta = past + qi * tq - kj * tk

    @pl.when(kj == 0)
    def _():
        m_scr[...] = jnp.full(m_scr.shape, NEG_INF, F32)
        l_scr[...] = jnp.zeros(l_scr.shape, F32)
        acc_scr[...] = jnp.zeros(acc_scr.shape, F32)

    def tiles(masked):
        v_bd = [_block_diag_values(v_ref[c * ck:(c + 1) * ck, :]) for c in range(tk // ck)]
        for r in range(tq // rq):
            rows = slice(r * rq, (r + 1) * rq)
            m = [m_scr[h, rows, :] for h in range(N_HEADS)]
            l = [l_scr[h, rows, :] for h in range(N_HEADS)]
            acc = acc_scr[rows, :]
            for c in range(tk // ck):
                keys = slice(c * ck, (c + 1) * ck)
                if masked:
                    q_chunk = (past + qi * tq + r * rq + lax.broadcasted_iota(jnp.int32, (rq, ck), 0)) // CHUNK
                    k_chunk = (kj * tk + c * ck + lax.broadcasted_iota(jnp.int32, (rq, ck), 1)) // CHUNK
                    allowed = k_chunk <= q_chunk
                ps, alphas = [], []
                for h in range(N_HEADS):
                    sl = slice(h * HEAD_PAD, (h + 1) * HEAD_PAD)
                    s = _dot_nt(q_ref[rows, sl], k_ref[keys, sl])
                    if masked:
                        s = jnp.where(allowed, s, NEG_INF)
                    m_new = jnp.maximum(m[h], jnp.max(s, -1, keepdims=True))
                    alpha = jnp.exp2(m[h] - m_new)
                    p = jnp.exp2(s - _lane_tile(m_new, ck // LANES))
                    l[h] = alpha * l[h] + jnp.sum(p, -1, keepdims=True)
                    m[h] = m_new
                    ps.append(p.astype(BF16))
                    alphas.append(alpha)
                acc = _per_head_lanes(alphas) * acc + _dot(jnp.concatenate(ps, axis=1), v_bd[c])
            for h in range(N_HEADS):
                m_scr[h, rows, :] = m[h]
                l_scr[h, rows, :] = l[h]
            acc_scr[rows, :] = acc

    @pl.when(delta >= tk)
    def _():
        tiles(False)

    @pl.when((delta < tk) & (delta + tq > 0))
    def _():
        tiles(True)

    @pl.when(kj == nkv - 1)
    def _():
        o_ref[...] = acc_scr[...] / _per_head_lanes([l_scr[h] for h in range(N_HEADS)])


def _mla_attn(q, k, v, past, tq, tk):
    B, T, W = q.shape
    S = k.shape[1]
    nq, nkv = T // tq, S // tk
    rq, ck = min(MLA_TILE[0], tq), min(MLA_TILE[1], tk)

    def kv_map(bi, i, j):
        last = (past + (i + 1) * tq - 1) // tk
        return (bi, jnp.minimum(j, last), 0)

    kern = functools.partial(_mla_attn_kernel, tq=tq, tk=tk, rq=rq, ck=ck, past=past, nkv=nkv)
    return pl.pallas_call(
        kern, grid=(B, nq, nkv),
        in_specs=[pl.BlockSpec((None, tq, W), lambda bi, i, j: (bi, i, 0)),
                  pl.BlockSpec((None, tk, W), kv_map), pl.BlockSpec((None, tk, GROUP_W), kv_map)],
        out_specs=pl.BlockSpec((None, tq, GROUP_W), lambda bi, i, j: (bi, i, 0)),
        out_shape=jax.ShapeDtypeStruct((B, T, GROUP_W), F32),
        scratch_shapes=[pltpu.VMEM((N_HEADS, tq, LANES), F32), pltpu.VMEM((N_HEADS, tq, LANES), F32),
                        pltpu.VMEM((tq, GROUP_W), F32)],
        compiler_params=_params(("parallel", "parallel", "arbitrary")), name="mla_attn",
    )(q, k, v)


def _sb_attn_kernel(q_ref, k_ref, v_ref, u_ref, o_ref, c_scr, acc_scr, *, tq, tk, rq, ck, past, nkv):
    qi = pl.program_id(1)
    j = pl.program_id(2)
    last = (past + (qi + 1) * tq - 2) // tk
    jb = last - j
    delta = past + qi * tq - jb * tk

    @pl.when(j == 0)
    def _():
        c_scr[...] = jnp.zeros(c_scr.shape, F32)
        acc_scr[...] = jnp.zeros(acc_scr.shape, F32)

    def tiles(masked):
        u2 = u_ref[...]
        v_bd = [_block_diag_values(v_ref[c * ck:(c + 1) * ck, :]) for c in range(tk // ck)]
        head = lax.broadcasted_iota(jnp.int32, (rq, GROUP_W), 1) // HEAD_D
        for r in range(tq // rq):
            rows = slice(r * rq, (r + 1) * rq)
            q_r = q_ref[rows, :]
            q_h = [jnp.where(head == h, q_r, jnp.zeros_like(q_r)) for h in range(N_HEADS)]
            carry = [c_scr[h, rows, :] for h in range(N_HEADS)]
            acc = acc_scr[rows, :]
            for c in reversed(range(tk // ck)):
                k_c = k_ref[c * ck:(c + 1) * ck, :]
                if masked:
                    q_pos = delta + r * rq + lax.broadcasted_iota(jnp.int32, (rq, ck), 0)
                    allowed = (c * ck + lax.broadcasted_iota(jnp.int32, (rq, ck), 1)) < q_pos
                ws = []
                for h in range(N_HEADS):
                    z = _dot_nt(q_h[h], k_c)
                    sp = jnp.log(1.0 + jnp.exp2(-jnp.abs(z))) * LOG2E
                    log_beta = jnp.minimum(z, 0.0) - sp
                    log_1m = log_beta - z
                    if masked:
                        log_1m = jnp.where(allowed, log_1m, 0.0)
                    after = _dot(log_1m.astype(BF16), u2) + _lane_tile(carry[h], ck // LANES)
                    w = jnp.exp2(log_beta + after)
                    if masked:
                        w = jnp.where(allowed, w, 0.0)
                    ws.append(w.astype(BF16))
                    carry[h] = carry[h] + jnp.sum(log_1m, -1, keepdims=True)
                acc = acc + _dot(jnp.concatenate(ws, axis=1), v_bd[c])
            for h in range(N_HEADS):
                c_scr[h, rows, :] = carry[h]
            acc_scr[rows, :] = acc

    @pl.when((delta >= tk) & (jb >= 0))
    def _():
        tiles(False)

    @pl.when((delta < tk) & (jb >= 0))
    def _():
        tiles(True)

    @pl.when(j == nkv - 1)
    def _():
        o_ref[...] = acc_scr[...]


def _sb_attn(q, k, v, past, tq, tk):
    B, T, W = q.shape
    S = k.shape[1]
    nq, nkv = T // tq, S // tk
    rq, ck = min(SB_TILE[0], tq), min(SB_TILE[1], tk)
    later = lax.broadcasted_iota(jnp.int32, (ck, ck), 0) > lax.broadcasted_iota(jnp.int32, (ck, ck), 1)
    u2 = later.astype(BF16)

    def kv_map(bi, i, j):
        last = (past + (i + 1) * tq - 2) // tk
        return (bi, jnp.maximum(last - j, 0), 0)

    kern = functools.partial(_sb_attn_kernel, tq=tq, tk=tk, rq=rq, ck=ck, past=past, nkv=nkv)
    return pl.pallas_call(
        kern, grid=(B, nq, nkv),
        in_specs=[pl.BlockSpec((None, tq, W), lambda bi, i, j: (bi, i, 0)),
                  pl.BlockSpec((None, tk, W), kv_map), pl.BlockSpec((None, tk, W), kv_map),
                  _full_spec((ck, ck))],
        out_specs=pl.BlockSpec((None, tq, GROUP_W), lambda bi, i, j: (bi, i, 0)),
        out_shape=jax.ShapeDtypeStruct((B, T, GROUP_W), F32),
        scratch_shapes=[pltpu.VMEM((N_HEADS, tq, LANES), F32), pltpu.VMEM((tq, GROUP_W), F32)],
        compiler_params=_params(("parallel", "parallel", "arbitrary")), name="sb_attn",
    )(q, k, v, u2)


def _softplus(x):
    return jnp.maximum(x, 0.0) + jnp.log1p(jnp.exp(-jnp.abs(x)))


def _split_bf16(a):
    hi = a.astype(BF16)
    return hi, (a - hi.astype(F32)).astype(BF16)


def _dot_split(a, b):
    a_hi, a_lo = _split_bf16(a)
    b_hi, b_lo = _split_bf16(b)
    m = a.shape[0]
    top = _dot(jnp.concatenate([a_hi, a_lo], 0), b_hi)
    return top[:m] + top[m:] + _dot(a_hi, b_lo)


def _head_sumsq(x, hsum):
    hi, lo = _split_bf16(x * x)
    m = x.shape[0]
    both = _dot(jnp.concatenate([hi, lo], 0), hsum)
    return both[:m] + both[m:]


def _gdn_kernel(qkv_ref, gz_ref, conv0_ref, s0_ref, cw_ref, alog_ref, dtb_ref, ng_ref, hsum_ref,
                o_ref, s_out_ref, xp_scr, s_scr, o_scr, *, L, nt, bb):
    ti = pl.program_id(1)
    halo = CONV_W - 1

    @pl.when(ti == 0)
    def _():
        xp_scr[:, 8 - halo:8, :] = conv0_ref[...]
        s_scr[...] = s0_ref[...]

    hsum = hsum_ref[...]
    ii = lax.broadcasted_iota(jnp.int32, (L, L), 0)
    jj = lax.broadcasted_iota(jnp.int32, (L, L), 1)
    incl = ii >= jj
    strict = ii > jj
    eye = (ii == jj).astype(F32)

    units = []
    for b in range(bb):
        x = qkv_ref[b]
        xp_scr[b, 8:8 + L, :] = x
        conv = jnp.zeros((L, N_GDN_QKV), F32)
        for i in range(CONV_W):
            conv = conv + xp_scr[b, 8 - halo + i:8 - halo + i + L, :] * cw_ref[i:i + 1, :]
        xp_scr[b, 8 - halo:8, :] = x[L - halo:, :]
        conv = jax.nn.silu(conv)
        qa = conv[:, :GROUP_W]
        ka = conv[:, GROUP_W:2 * GROUP_W]
        va = conv[:, 2 * GROUP_W:]
        qa = qa * lax.rsqrt(_head_sumsq(qa, hsum) + 1e-6) * (HEAD_D ** -0.5)
        ka = ka * lax.rsqrt(_head_sumsq(ka, hsum) + 1e-6)
        ab = gz_ref[b, :, GROUP_W:]
        g_all = -jnp.exp(alog_ref[...]) * _softplus(ab + dtb_ref[...])
        beta_all = jax.nn.sigmoid(ab)
        for h in range(N_HEADS):
            hs = slice(h * HEAD_D, (h + 1) * HEAD_D)
            units.append(dict(b=b, h=h, q=qa[:, hs], k=ka[:, hs], v=va[:, hs], g=g_all[:, h:h + 1],
                              beta=beta_all[:, N_HEADS + h:N_HEADS + h + 1]))

    for un in units:
        g_b = jnp.broadcast_to(un['g'], (L, L))
        gc_row = jnp.sum(jnp.where(ii <= jj, g_b, 0.0), 0, keepdims=True)
        gc_col = jnp.sum(jnp.where(ii == jj, jnp.broadcast_to(gc_row, (L, L)), 0.0), 1, keepdims=True)
        un['decay'] = jnp.where(incl, jnp.exp(jnp.where(incl, gc_col - gc_row, 0.0)), 0.0)
        un['e_col'] = jnp.exp(gc_col)
        gc_last = gc_row[:, L - 1:L]
        un['g_last'] = jnp.exp(gc_last)
        un['k_tail'] = jnp.exp(gc_last - gc_col)
        un['kb'] = un['k'] * un['beta']
        un['kbf'] = un['k'].astype(BF16)
    for un in units:
        m = jnp.where(strict, _dot_nt(un['kb'].astype(BF16), un['kbf']) * un['decay'], 0.0)
        un['p'] = -m
        un['t'] = eye - m

    n_sq = int(math.log2(L)) - 1
    for r in range(n_sq + 1):
        for un in units:
            p_hi, p_lo = _split_bf16(un['p'])
            if r == 0:
                lhs_hi, lhs_lo = p_hi, p_lo
            else:
                t_hi, t_lo = _split_bf16(un['t'])
                lhs_hi = jnp.concatenate([p_hi, t_hi], 0) if r < n_sq else t_hi
                lhs_lo = jnp.concatenate([p_lo, t_lo], 0) if r < n_sq else t_lo
            rows = lhs_hi.shape[0]
            top = _dot(jnp.concatenate([lhs_hi, lhs_lo], 0), p_hi)
            prod = top[:rows] + top[rows:] + _dot(lhs_hi, p_lo)
            if r == 0:
                un['p'] = prod
            elif r < n_sq:
                un['p'] = prod[:L]
                un['t'] = un['t'] + prod[L:]
            else:
                un['t'] = un['t'] + prod
    for un in units:
        un['u'] = _dot_split(un['t'], un['v'] * un['beta'])
        un['w'] = _dot_split(un['t'], un['kb'] * un['e_col'])
        un['attn'] = jnp.where(incl, _dot_nt(un['q'].astype(BF16), un['kbf']) * un['decay'], 0.0).astype(BF16)

    for un in units:
        un['S'] = s_scr[un['b'], un['h']]
        un['Sb'] = un['S'].astype(BF16)
    for un in units:
        un['v_new'] = (un['u'] - _dot(un['w'].astype(BF16), un['Sb'])).astype(BF16)
    for un in units:
        hs = slice(un['h'] * HEAD_D, (un['h'] + 1) * HEAD_D)
        o_h = _dot((un['q'] * un['e_col']).astype(BF16), un['Sb']) + _dot(un['attn'], un['v_new'])
        o_scr[un['b'], :, hs] = o_h
        k_dec = (un['k'] * un['k_tail']).astype(BF16)
        s_scr[un['b'], un['h']] = un['S'] * un['g_last'] + _dot_tn(k_dec, un['v_new'])

    for b in range(bb):
        o = o_scr[b]
        ms = _head_sumsq(o, hsum) * (1.0 / HEAD_D)
        o_ref[b] = o * lax.rsqrt(ms + RMS_EPS) * ng_ref[...] * jax.nn.silu(gz_ref[b, :, :GROUP_W])

    @pl.when(ti == nt - 1)
    def _():
        s_out_ref[...] = s_scr[...]


def _gdn(qkv, gz, conv0, s0, conv_w, a_log, dt_bias, norm_g):
    B, T, _ = qkv.shape
    L = min(CHUNK, T)
    nt = T // L
    pad = lambda a: jnp.concatenate([a, jnp.zeros((LANES - a.shape[0],), F32)]).reshape(1, LANES)
    alog = pad(a_log)
    dtb = pad(dt_bias)
    ng = jnp.tile(norm_g, N_HEADS).reshape(1, GROUP_W)
    hid = jnp.arange(GROUP_W) // HEAD_D
    hsum = (hid[:, None] == hid[None, :]).astype(BF16)
    bb = GDN_BATCH_PER_STEP

    def row(f):
        return pl.BlockSpec((bb, L, f), lambda bi, i: (bi, i, 0))

    state = pl.BlockSpec((bb, N_HEADS, HEAD_D, HEAD_D), lambda bi, i: (bi, 0, 0, 0))
    kern = functools.partial(_gdn_kernel, L=L, nt=nt, bb=bb)
    return pl.pallas_call(
        kern, grid=(B // bb, nt),
        in_specs=[row(N_GDN_QKV), row(GZ_PAD),
                  pl.BlockSpec((bb, CONV_W - 1, N_GDN_QKV), lambda bi, i: (bi, 0, 0)), state,
                  _full_spec(conv_w.shape), _full_spec(alog.shape), _full_spec(dtb.shape),
                  _full_spec(ng.shape), _full_spec(hsum.shape)],
        out_specs=[row(GROUP_W), state],
        out_shape=[jax.ShapeDtypeStruct((B, T, GROUP_W), F32),
                   jax.ShapeDtypeStruct((B, N_HEADS, HEAD_D, HEAD_D), F32)],
        scratch_shapes=[pltpu.VMEM((bb, 8 + L, N_GDN_QKV), F32), pltpu.VMEM((bb, N_HEADS, HEAD_D, HEAD_D), F32),
                        pltpu.VMEM((bb, L, GROUP_W), F32)],
        compiler_params=_params(("parallel", "arbitrary")), name="gdn",
    )(qkv, gz, conv0, s0, conv_w, alog, dtb, ng, hsum)


def _s5_kernel(u_ref, h0r_ref, h0i_ref, lr_ref, li_ref, br_ref, bi_ref, cr_ref, ci_ref, d_ref, wg_ref,
               o_ref, hr_out_ref, hi_out_ref, hr_scr, hi_scr, xr_scr, xi_scr, *, tt, nb, nt):
    ti = pl.program_id(0)

    @pl.when(ti == 0)
    def _():
        hr_scr[...] = h0r_ref[...]
        hi_scr[...] = h0i_ref[...]

    u = u_ref[...]
    ub = u.astype(BF16)
    xr_scr[...] = _dot(ub, br_ref[...])
    xi_scr[...] = _dot(ub, bi_ref[...])
    lr = jnp.broadcast_to(lr_ref[...], (nb, S5_STATE))
    li = jnp.broadcast_to(li_ref[...], (nb, S5_STATE))

    def step(t, carry):
        hr, hi = carry
        r0 = pl.multiple_of(t * nb, nb)
        nr = lr * hr - li * hi + xr_scr[pl.ds(r0, nb), :]
        ni = lr * hi + li * hr + xi_scr[pl.ds(r0, nb), :]
        xr_scr[pl.ds(r0, nb), :] = nr
        xi_scr[pl.ds(r0, nb), :] = ni
        return nr, ni

    hr, hi = lax.fori_loop(0, tt, step, (hr_scr[...], hi_scr[...]), unroll=8)
    hr_scr[...] = hr
    hi_scr[...] = hi
    y = _dot(xr_scr[...].astype(BF16), cr_ref[...]) - _dot(xi_scr[...].astype(BF16), ci_ref[...]) + d_ref[...] * u
    act = jax.nn.gelu(y)
    o_ref[...] = act * jax.nn.sigmoid(_dot(act.astype(BF16), wg_ref[...]))

    @pl.when(ti == nt - 1)
    def _():
        hr_out_ref[...] = hr
        hi_out_ref[...] = hi


def _prep_s5_weights(lam_re, lam_im, log_step, b_re, b_im, c_re, c_im, d, w_glu):
    dt = jnp.exp(log_step)[:, None]
    mag = jnp.exp(lam_re * dt)
    bar_re, bar_im = mag * jnp.cos(lam_im * dt), mag * jnp.sin(lam_im * dt)
    den = lam_re * lam_re + lam_im * lam_im
    f_re = ((bar_re - 1.0) * lam_re + bar_im * lam_im) / den
    f_im = (bar_im * lam_re - (bar_re - 1.0) * lam_im) / den
    bb_re = f_re[..., None] * b_re - f_im[..., None] * b_im
    bb_im = f_re[..., None] * b_im + f_im[..., None] * b_re
    eye = jnp.eye(S5_G, dtype=F32)

    def in_mat(b):
        return jnp.einsum('gpc,gh->gchp', b, eye).reshape(S5_G * S5_CG, S5_STATE).astype(BF16)

    def out_mat(c):
        return jnp.einsum('gcp,gh->gphc', c, eye).reshape(S5_STATE, S5_G * S5_CG).astype(BF16)

    return (bar_re.reshape(1, S5_STATE), bar_im.reshape(1, S5_STATE),
            in_mat(bb_re), in_mat(bb_im), out_mat(c_re), out_mat(c_im),
            d.reshape(1, GROUP_W), w_glu.astype(BF16))


def _s5(u_tm, h0_re, h0_im, weights, B, tt):
    T = u_tm.shape[0]
    u2 = u_tm.reshape(T * B, GROUP_W)
    nt = T // tt
    lr, li, br, bi, cr, ci, d, wg = weights
    rows = pl.BlockSpec((tt * B, GROUP_W), lambda i: (i, 0))
    st = _full_spec((B, S5_STATE))
    kern = functools.partial(_s5_kernel, tt=tt, nb=B, nt=nt)
    o, hr, hi = pl.pallas_call(
        kern, grid=(nt,),
        in_specs=[rows, st, st] + [_full_spec(w.shape) for w in weights],
        out_specs=[rows, st, st],
        out_shape=[jax.ShapeDtypeStruct((T * B, GROUP_W), F32), jax.ShapeDtypeStruct((B, S5_STATE), F32),
                   jax.ShapeDtypeStruct((B, S5_STATE), F32)],
        scratch_shapes=[pltpu.VMEM((B, S5_STATE), F32), pltpu.VMEM((B, S5_STATE), F32),
                        pltpu.VMEM((tt * B, S5_STATE), F32), pltpu.VMEM((tt * B, S5_STATE), F32)],
        compiler_params=_params(("arbitrary",)), name="s5",
    )(u2, h0_re.reshape(B, S5_STATE), h0_im.reshape(B, S5_STATE), *weights)
    return o.reshape(T, B * GROUP_W), hr.reshape(B, S5_G, S5_P), hi.reshape(B, S5_G, S5_P)


def _outproj_kernel(oa_ref, ob_ref, oc_ref, od_ref, x_ref, gn_ref, w_ref, g_ref, b_ref, o_ref, *, alpha):
    gn = gn_ref[...]
    mix = jnp.concatenate([
        _rms_rows(oa_ref[...], gn[0:1]).astype(BF16),
        _rms_rows(ob_ref[...], gn[1:2]).astype(BF16),
        oc_ref[...].astype(BF16),
        _rms_rows(od_ref[...], gn[2:3]).astype(BF16)], -1)
    y = alpha * x_ref[...] + _dot(mix, w_ref[...])
    o_ref[...] = _ln_rows(y, g_ref[...], b_ref[...])


def _outproj(oa, ob, oc, od_tm, x, gn, w_out, g, b, alpha, tm):
    B, T, D = x.shape

    def row(f):
        return pl.BlockSpec((None, tm, f), lambda bi, i: (bi, i, 0))

    kern = functools.partial(_outproj_kernel, alpha=alpha)
    return pl.pallas_call(
        kern, grid=(B, T // tm),
        in_specs=[row(GROUP_W), row(GROUP_W), row(GROUP_W), pl.BlockSpec((tm, GROUP_W), lambda bi, i: (i, bi)),
                  row(D), _full_spec(gn.shape), _full_spec(w_out.shape), _full_spec((1, D)), _full_spec((1, D))],
        out_specs=row(D), out_shape=jax.ShapeDtypeStruct((B, T, D), F32),
        compiler_params=_params(("parallel", "parallel")), name="outproj",
    )(oa, ob, oc, od_tm, x, gn, w_out, g.reshape(1, D), b.reshape(1, D))


def _ffn_kernel(x_ref, wg_ref, wu_ref, wd_ref, g_ref, b_ref, o_ref, *, alpha):
    x = x_ref[...]
    xb = x.astype(BF16)
    hid = (jax.nn.silu(_dot(xb, wg_ref[...])) * _dot(xb, wu_ref[...])).astype(BF16)
    y = alpha * x + _dot(hid, wd_ref[...])
    o_ref[...] = _ln_rows(y, g_ref[...], b_ref[...])


def _ffn(x, wg, wu, wd, g, b, alpha, tm):
    B, T, D = x.shape
    row = pl.BlockSpec((None, tm, D), lambda bi, i: (bi, i, 0))

    def resident(shape):
        return pl.BlockSpec(shape, lambda *_: (0,) * len(shape), pipeline_mode=pl.Buffered(1))

    kern = functools.partial(_ffn_kernel, alpha=alpha)
    return pl.pallas_call(
        kern, grid=(B, T // tm),
        in_specs=[row, resident(wg.shape), resident(wu.shape), resident(wd.shape),
                  _full_spec((1, D)), _full_spec((1, D))],
        out_specs=row, out_shape=jax.ShapeDtypeStruct((B, T, D), F32),
        compiler_params=_params(("parallel", "parallel")), name="ffn",
    )(x, wg, wu, wd, g.reshape(1, D), b.reshape(1, D))


def _router_kernel(x_ref, wr_ref, br_ref, idx_ref, gate_ref):
    logits = _dot_nt(wr_ref[...], x_ref[...], HIGHEST) + br_ref[...]
    eid = lax.broadcasted_iota(jnp.int32, logits.shape, 0)
    m1 = jnp.max(logits, 0, keepdims=True)
    i1 = jnp.min(jnp.where(logits == m1, eid, N_EXP), 0, keepdims=True)
    rest = jnp.where(eid == i1, -jnp.inf, logits)
    m2 = jnp.max(rest, 0, keepdims=True)
    i2 = jnp.min(jnp.where(rest == m2, eid, N_EXP), 0, keepdims=True)
    e2 = jnp.exp(m2 - m1)
    den = 1.0 + e2
    idx_ref[...] = jnp.concatenate([i1, i2], 0)
    gate_ref[...] = jnp.concatenate([1.0 / den, e2 / den], 0)


def _router(x2, w_router, b_router, tm):
    N, D = x2.shape
    return pl.pallas_call(
        _router_kernel, grid=(N // tm,),
        in_specs=[pl.BlockSpec((tm, D), lambda i: (i, 0)), _full_spec((N_EXP, D)), _full_spec((N_EXP, 1))],
        out_specs=[pl.BlockSpec((TOP_K, tm), lambda i: (0, i)), pl.BlockSpec((TOP_K, tm), lambda i: (0, i))],
        out_shape=[jax.ShapeDtypeStruct((TOP_K, N), jnp.int32), jax.ShapeDtypeStruct((TOP_K, N), F32)],
        compiler_params=_params(("parallel",)), name="moe_router",
    )(x2, w_router.T, b_router.reshape(N_EXP, 1))


def _expert_kernel(blk_exp_ref, src_ref, src_next_ref, dst_ref, x_hbm, wg_ref, wu_ref, wd_ref, out_hbm,
                   xbuf, ybuf, gsem, ssem, *, n_blk):
    del blk_exp_ref
    i = pl.program_id(0)
    n_sub = MOE_BLK // MOE_SUB

    def rows(j):
        return pl.ds(j * MOE_SUB, MOE_SUB)

    def gather(j, idx_ref):
        for r in range(j * MOE_SUB, (j + 1) * MOE_SUB):
            pltpu.make_async_copy(x_hbm.at[pl.ds(idx_ref[0, 0, r], 1)], xbuf.at[pl.ds(r, 1)], gsem.at[j]).start()

    def scatter(j):
        for r in range(j * MOE_SUB, (j + 1) * MOE_SUB):
            pltpu.make_async_copy(ybuf.at[pl.ds(r, 1)], out_hbm.at[pl.ds(dst_ref[0, 0, r], 1)], ssem.at[j]).start()

    def wait_gather(j):
        pltpu.make_async_copy(x_hbm.at[pl.ds(0, MOE_SUB)], xbuf.at[rows(j)], gsem.at[j]).wait()

    def wait_scatter(j):
        pltpu.make_async_copy(ybuf.at[rows(j)], out_hbm.at[pl.ds(0, MOE_SUB)], ssem.at[j]).wait()

    @pl.when(i == 0)
    def _():
        gather(0, src_ref)

    for j in range(n_sub):
        wait_gather(j)
        if j == n_sub - 1:
            @pl.when(i > 0)
            def _():
                wait_scatter(n_sub - 1)
        if j + 1 < n_sub:
            gather(j + 1, src_ref)
        if j == 1:
            gather(0, src_next_ref)
        if j >= 1:
            scatter(j - 1)
        xb = xbuf[rows(j), :].astype(BF16)
        hid = (jax.nn.silu(_dot(xb, wg_ref[...])) * _dot(xb, wu_ref[...])).astype(BF16)
        ybuf[rows(j), :] = _dot(hid, wd_ref[...])
    scatter(n_sub - 1)
    for j in range(n_sub - 1):
        wait_scatter(j)

    @pl.when(i == n_blk - 1)
    def _():
        wait_scatter(n_sub - 1)
        wait_gather(0)


def _combine_kernel(x_ref, y0_ref, y1_ref, gate_ref, g_ref, b_ref, o_ref, *, alpha):
    gate = gate_ref[...]
    f = y0_ref[...] * gate[:, 0:1] + y1_ref[...] * gate[:, 1:2]
    o_ref[...] = _ln_rows(alpha * x_ref[...] + f, g_ref[...], b_ref[...])


def _moe(x, w_router, b_router, wg, wu, wd, g, b, alpha):
    B, T, D = x.shape
    N = B * T
    x2 = x.reshape(N, D)
    tm = math.gcd(N, MOE_BLK)
    idx, gates = _router(x2, w_router, b_router, tm)

    n_slots = N * TOP_K
    e_flat = idx.T.reshape(-1)
    onehot = (e_flat[:, None] == jnp.arange(N_EXP, dtype=jnp.int32)[None, :]).astype(jnp.int32)
    csum = jnp.cumsum(onehot, 0)
    counts = csum[-1]
    rank = jnp.sum((csum - onehot) * onehot, 1)
    padded = (counts + MOE_BLK - 1) // MOE_BLK * MOE_BLK
    pad_end = jnp.cumsum(padded)
    pad_start = pad_end - padded
    dest = pad_start[e_flat] + rank
    n_blk = -(-(n_slots + N_EXP * (MOE_BLK - 1)) // MOE_BLK)
    rows = n_blk * MOE_BLK
    row_f = jnp.full((rows,), n_slots, jnp.int32).at[dest].set(jnp.arange(n_slots, dtype=jnp.int32))
    row_src = jnp.minimum(row_f // TOP_K, N - 1)
    row_dst = jnp.where(row_f < n_slots, (row_f % TOP_K) * N + row_f // TOP_K,
                        n_slots + jnp.arange(rows, dtype=jnp.int32) % MOE_BLK)
    blk_start = jnp.arange(n_blk, dtype=jnp.int32) * MOE_BLK
    blk_exp = jnp.minimum(jnp.sum((pad_end[None, :] <= blk_start[:, None]).astype(jnp.int32), 1), N_EXP - 1)

    idx_spec = pl.BlockSpec((1, 1, MOE_BLK), lambda i, be: (i, 0, 0), memory_space=pltpu.SMEM)
    next_spec = pl.BlockSpec((1, 1, MOE_BLK), lambda i, be: (jnp.minimum(i + 1, n_blk - 1), 0, 0),
                             memory_space=pltpu.SMEM)
    y = pl.pallas_call(
        functools.partial(_expert_kernel, n_blk=n_blk),
        grid_spec=pltpu.PrefetchScalarGridSpec(
            num_scalar_prefetch=1, grid=(n_blk,),
            in_specs=[idx_spec, next_spec, idx_spec,
                      pl.BlockSpec(memory_space=pl.ANY),
                      pl.BlockSpec((None, D, D_FF_EXP), lambda i, be: (be[i], 0, 0)),
                      pl.BlockSpec((None, D, D_FF_EXP), lambda i, be: (be[i], 0, 0)),
                      pl.BlockSpec((None, D_FF_EXP, D), lambda i, be: (be[i], 0, 0))],
            out_specs=pl.BlockSpec(memory_space=pl.ANY),
            scratch_shapes=[pltpu.VMEM((MOE_BLK, D), F32), pltpu.VMEM((MOE_BLK, D), F32),
                            pltpu.SemaphoreType.DMA((MOE_BLK // MOE_SUB,)),
                            pltpu.SemaphoreType.DMA((MOE_BLK // MOE_SUB,))]),
        out_shape=jax.ShapeDtypeStruct((n_slots + MOE_BLK, D), F32),
        compiler_params=_params(("arbitrary",)), name="moe_experts",
    )(blk_exp, row_src.reshape(n_blk, 1, MOE_BLK), row_src.reshape(n_blk, 1, MOE_BLK),
      row_dst.reshape(n_blk, 1, MOE_BLK), x2, wg, wu, wd)

    kern = functools.partial(_combine_kernel, alpha=alpha)
    rows_spec = pl.BlockSpec((tm, D), lambda i: (i, 0))
    out = pl.pallas_call(
        kern, grid=(N // tm,),
        in_specs=[rows_spec, rows_spec, pl.BlockSpec((tm, D), lambda i: (i + N // tm, 0)),
                  pl.BlockSpec((tm, TOP_K), lambda i: (i, 0)), _full_spec((1, D)), _full_spec((1, D))],
        out_specs=rows_spec, out_shape=jax.ShapeDtypeStruct((N, D), F32),
        compiler_params=_params(("parallel",)), name="moe_combine",
    )(x2, y, y, gates.T, g.reshape(1, D), b.reshape(1, D))
    return out.reshape(B, T, D)


def _round_up(n, m):
    return -(-n // m) * m


def _run_trunk(x, states, P, depth):
    lat_c, kr_c, sbk_c, sbv_c, conv_c, ssm_c, re_c, im_c = states
    B, T, D = x.shape
    past = lat_c.shape[2]
    alpha = (2 * depth) ** 0.25
    long_seq = T >= 512
    tm = 512 if long_seq else T
    tq = ATTN_TQ if long_seq else T
    tk = ATTN_TK if long_seq else 2 * LANES
    S = past + T
    S_pad = _round_up(S, tk)

    tab = _rope_table(past, T)
    outs = [[] for _ in range(8)]
    for l in range(depth):
        qn, kvn, wq, wqr, wk, wv, e = _prep_mla_weights(P['mla_q_norm'][l], P['mla_kv_norm'][l],
                                                        P['mla_w_uq'][l], P['mla_w_ukv'][l])
        proj = _inproj(x, P['ln_in_g'], P['ln_in_b'], l == 0, tab, (qn, kvn, wq, wqr),
                       _prep_inproj_weights(P['w_in'][l]), tm)
        if l == 0:
            x, *proj = proj
        q_a, lat_n, kr_n, krp_n, sbq, sbk, sbv, sbk_b, sbv_b, gq, gz, s5u = proj

        if past or S_pad != S:
            kr_cache = jnp.pad(kr_c[l], ((0, 0), (0, 0), (0, LANES - ROPE)))
            lat_all = jnp.pad(jnp.concatenate([lat_c[l], lat_n], 1), ((0, 0), (0, S_pad - S), (0, 0)))
            krp_all = jnp.pad(jnp.concatenate([kr_cache, krp_n], 1), ((0, 0), (0, S_pad - S), (0, 0)))
        else:
            lat_all, krp_all = lat_n, krp_n
        k_a, v_a = _mla_kv(lat_all, krp_all, wk, wv, e, tk)
        o_a = _mla_attn(q_a, k_a, v_a, past, tq, tk)

        if past or S_pad != S:
            k_all = jnp.pad(jnp.concatenate([sbk_c[l].reshape(B, past, GROUP_W).astype(BF16), sbk_b], 1),
                            ((0, 0), (0, S_pad - S), (0, 0)))
            v_all = jnp.pad(jnp.concatenate([sbv_c[l].reshape(B, past, GROUP_W).astype(BF16), sbv_b], 1),
                            ((0, 0), (0, S_pad - S), (0, 0)))
        else:
            k_all, v_all = sbk_b, sbv_b
        o_b = _sb_attn(sbq, k_all, v_all, past, tq, tk)

        o_c, ssm_n = _gdn(gq, gz, conv_c[l], ssm_c[l], P['gdn_conv_w'][l], P['gdn_a_log'][l],
                          P['gdn_dt_bias'][l], P['gdn_norm'][l])
        conv_n = jnp.concatenate([conv_c[l], gq], 1)[:, T:]

        s5w = _prep_s5_weights(P['s5_lam_re'][l], P['s5_lam_im'][l], P['s5_log_step'][l], P['s5_b_re'][l],
                               P['s5_b_im'][l], P['s5_c_re'][l], P['s5_c_im'][l], P['s5_d'][l], P['s5_w_glu'][l])
        o_d, re_n, im_n = _s5(s5u, re_c[l], im_c[l], s5w, B, min(64, T))

        x = _outproj(o_a, o_b, o_c, o_d, x, P['grp_norm'][l], P['w_out'][l].astype(BF16),
                     P['ln1_g'][l], P['ln1_b'][l], alpha, tm)
        i = l // 2
        if l % 2 == 0:
            x = _ffn(x, P['ffn_w_gate'][i].astype(BF16), P['ffn_w_up'][i].astype(BF16),
                     P['ffn_w_down'][i].astype(BF16), P['ln2_g'][l], P['ln2_b'][l], alpha, min(tm, 256))
        else:
            x = _moe(x, P['moe_w_router'][i], P['moe_b_router'][i], P['moe_w_gate'][i].astype(BF16),
                     P['moe_w_up'][i].astype(BF16), P['moe_w_down'][i].astype(BF16),
                     P['ln2_g'][l], P['ln2_b'][l], alpha)
        news = (lat_n, kr_n, sbk.reshape(B, T, N_HEADS, HEAD_D), sbv.reshape(B, T, N_HEADS, HEAD_D),
                conv_n, ssm_n, re_n, im_n)
        for lst, a in zip(outs, news):
            lst.append(a)
    return x, [jnp.stack(lst) for lst in outs]


def kernel(x_prompt, x_sample, cache_mla_latent, cache_mla_krope, cache_sb_k, cache_sb_v,
           state_gdn_conv, state_gdn_ssm, state_s5_re, state_s5_im,
           ln_in_g, ln_in_b, w_in, mla_q_norm, mla_kv_norm, mla_w_uq, mla_w_ukv,
           gdn_conv_w, gdn_a_log, gdn_dt_bias, gdn_norm,
           s5_lam_re, s5_lam_im, s5_log_step, s5_b_re, s5_b_im, s5_c_re, s5_c_im, s5_d, s5_w_glu,
           grp_norm, w_out, ln1_g, ln1_b, ln2_g, ln2_b,
           ffn_w_gate, ffn_w_up, ffn_w_down,
           moe_w_router, moe_b_router, moe_w_gate, moe_w_up, moe_w_down):
    P = dict(ln_in_g=ln_in_g, ln_in_b=ln_in_b, w_in=w_in, mla_q_norm=mla_q_norm, mla_kv_norm=mla_kv_norm,
             mla_w_uq=mla_w_uq, mla_w_ukv=mla_w_ukv, gdn_conv_w=gdn_conv_w, gdn_a_log=gdn_a_log,
             gdn_dt_bias=gdn_dt_bias, gdn_norm=gdn_norm, s5_lam_re=s5_lam_re, s5_lam_im=s5_lam_im,
             s5_log_step=s5_log_step, s5_b_re=s5_b_re, s5_b_im=s5_b_im, s5_c_re=s5_c_re, s5_c_im=s5_c_im,
             s5_d=s5_d, s5_w_glu=s5_w_glu, grp_norm=grp_norm, w_out=w_out, ln1_g=ln1_g, ln1_b=ln1_b,
             ln2_g=ln2_g, ln2_b=ln2_b, ffn_w_gate=ffn_w_gate, ffn_w_up=ffn_w_up, ffn_w_down=ffn_w_down,
             moe_w_router=moe_w_router, moe_b_router=moe_b_router, moe_w_gate=moe_w_gate,
             moe_w_up=moe_w_up, moe_w_down=moe_w_down)
    depth = w_in.shape[0]
    Bp = x_prompt.shape[0]
    dt = x_prompt.dtype
    prompt_states = (jnp.zeros((depth, Bp, 0, KV_LORA), dt), jnp.zeros((depth, Bp, 0, ROPE), dt),
                     jnp.zeros((depth, Bp, 0, N_HEADS, HEAD_D), dt), jnp.zeros((depth, Bp, 0, N_HEADS, HEAD_D), dt),
                     jnp.zeros((depth, Bp, CONV_W - 1, N_GDN_QKV), dt),
                     jnp.zeros((depth, Bp, N_HEADS, HEAD_D, HEAD_D), F32),
                     jnp.zeros((depth, Bp, S5_G, S5_P), F32), jnp.zeros((depth, Bp, S5_G, S5_P), F32))
    y_prompt, st_p = _run_trunk(x_prompt, prompt_states, P, depth)
    sample_states = (cache_mla_latent, cache_mla_krope, cache_sb_k, cache_sb_v,
                     state_gdn_conv, state_gdn_ssm, state_s5_re, state_s5_im)
    y_sample, st_s = _run_trunk(x_sample, sample_states, P, depth)
    return (y_prompt, y_sample, *st_p, *st_s)
```

```python
import functools
import math

import jax
import jax.numpy as jnp
from jax import lax
from jax.experimental import pallas as pl
from jax.experimental.pallas import tpu as pltpu

F32 = jnp.float32
BF16 = jnp.bfloat16
HIGHEST = lax.Precision.HIGHEST

D_MODEL = 1024
CHUNK = 64
GROUP_W = 256
N_HEADS = 4
HEAD_D = 64
NOPE = 64
ROPE = 32
Q_LORA = 192
KV_LORA = 128
ROPE_THETA = 10000.0
MLA_SCALE = (NOPE + ROPE) ** -0.5
SB_SCALE = HEAD_D ** -0.5
LOG2E = math.log2(math.e)
CONV_W = 4
N_GDN_QKV = 768
S5_CG = 16
S5_G = 16
S5_P = 64
S5_STATE = S5_G * S5_P
D_FF = 2816
N_EXP = 8
TOP_K = 2
D_FF_EXP = 1792
MOE_BLK = 512
LN_EPS = 1e-5
RMS_EPS = 1e-6
NEG_INF = -1e30
OFF_SB = Q_LORA + KV_LORA + ROPE
OFF_GDN = OFF_SB + 3 * GROUP_W
OFF_S5 = OFF_GDN + N_GDN_QKV + 2 * N_HEADS + GROUP_W

LANES = 128
HEAD_PAD = LANES
VMEM_LIMIT = 48 * 1024 * 1024
GDN_BATCH_PER_STEP = 8
MOE_SUB = 128
ATTN_TQ = 512
ATTN_TK = 512
MLA_TILE = (256, 512)
SB_TILE = (512, 256)


def _params(sem):
    return pltpu.CompilerParams(dimension_semantics=sem, vmem_limit_bytes=VMEM_LIMIT)


def _dot(a, b, precision=None):
    return jnp.dot(a, b, preferred_element_type=F32, precision=precision)


def _dot_nt(a, b, precision=None):
    return lax.dot_general(a, b, (((1,), (1,)), ((), ())), preferred_element_type=F32, precision=precision)


def _dot_tn(a, b, precision=None):
    return lax.dot_general(a, b, (((0,), (0,)), ((), ())), preferred_element_type=F32, precision=precision)


def _ln_rows(x, g, b):
    mu = jnp.mean(x, -1, keepdims=True)
    xc = x - mu
    var = jnp.mean(xc * xc, -1, keepdims=True)
    return xc * lax.rsqrt(var + LN_EPS) * g + b


def _rms_rows(x, g, n=None):
    n = x.shape[-1] if n is None else n
    ms = jnp.sum(x * x, -1, keepdims=True) * (1.0 / n)
    return x * lax.rsqrt(ms + RMS_EPS) * g


def _full_spec(shape):
    nd = len(shape)
    return pl.BlockSpec(shape, lambda *_: (0,) * nd)


MLA_IN_PAD = 640
GZ_PAD = 384


def _inproj_kernel(x_ref, lng_ref, lnb_ref, tab_ref, qn_ref, kvn_ref, wq_ref, wqr_ref,
                   wm_ref, wsb_ref, wgq_ref, wgz_ref, ws5_ref, *out_refs, apply_ln):
    if apply_ln:
        xln_ref, *out_refs = out_refs
    (q_ref, lat_ref, kr_ref, krp_ref, sbq_ref, sbk_ref, sbv_ref, sbkb_ref, sbvb_ref,
     gq_ref, gz_ref, s5_ref) = out_refs
    x = x_ref[...]
    if apply_ln:
        x = _ln_rows(x, lng_ref[...], lnb_ref[...])
        xln_ref[...] = x
    xb = x.astype(BF16)
    _mla_prep_rows(_dot(xb, wm_ref[...]), tab_ref[...], qn_ref, kvn_ref, wq_ref, wqr_ref,
                   q_ref, lat_ref, kr_ref, krp_ref)
    sb = _dot(xb, wsb_ref[...])
    sbq_ref[...] = (sb[:, :GROUP_W] * (SB_SCALE * LOG2E)).astype(BF16)
    sbk_ref[...] = sb[:, GROUP_W:2 * GROUP_W]
    sbv_ref[...] = sb[:, 2 * GROUP_W:]
    sbkb_ref[...] = sb[:, GROUP_W:2 * GROUP_W].astype(BF16)
    sbvb_ref[...] = sb[:, 2 * GROUP_W:].astype(BF16)
    gq_ref[...] = _dot(xb, wgq_ref[...])
    gz_ref[...] = _dot(xb, wgz_ref[...])
    s5_ref[...] = _dot(xb, ws5_ref[...])


def _rotate_half_cols(w):
    half = w.shape[-1] // 2
    return jnp.concatenate([-w[..., half:], w[..., :half]], -1)


def _prep_inproj_weights(w_in):
    zeros = lambda n: jnp.zeros((D_MODEL, n), F32)
    w_cq = w_in[:, :Q_LORA]
    w_ckv = w_in[:, Q_LORA:Q_LORA + KV_LORA]
    w_kr = w_in[:, Q_LORA + KV_LORA:OFF_SB]
    wm = jnp.concatenate([w_ckv, w_cq, zeros(64), w_kr, zeros(96), _rotate_half_cols(w_kr), zeros(96)], 1)
    wsb = w_in[:, OFF_SB:OFF_GDN]
    wgq = w_in[:, OFF_GDN:OFF_GDN + N_GDN_QKV]
    o_a = OFF_GDN + N_GDN_QKV
    wgz = jnp.concatenate([w_in[:, o_a + 2 * N_HEADS:OFF_S5], w_in[:, o_a:o_a + 2 * N_HEADS],
                           zeros(GZ_PAD - GROUP_W - 2 * N_HEADS)], 1)
    ws5 = w_in[:, OFF_S5:]
    return tuple(w.astype(BF16) for w in (wm, wsb, wgq, wgz, ws5))


def _inproj(x, ln_g, ln_b, apply_ln, tab, mla_weights, weights, tm):
    B, T, D = x.shape
    qn, kvn, wq, wqr = mla_weights
    consts = (ln_g.reshape(1, D), ln_b.reshape(1, D))
    resident = (qn, kvn, wq, wqr) + tuple(weights)

    def row(f):
        return pl.BlockSpec((None, tm, f), lambda bi, i: (bi, i, 0))

    def out(f, dt=F32):
        return jax.ShapeDtypeStruct((B, T, f), dt)

    out_specs = [row(N_HEADS * HEAD_PAD), row(KV_LORA), row(ROPE), row(LANES),
                 row(GROUP_W), row(GROUP_W), row(GROUP_W), row(GROUP_W), row(GROUP_W),
                 row(N_GDN_QKV), row(GZ_PAD), pl.BlockSpec((tm, GROUP_W), lambda bi, i: (i, bi))]
    out_shape = [out(N_HEADS * HEAD_PAD, BF16), out(KV_LORA), out(ROPE), out(LANES),
                 out(GROUP_W, BF16), out(GROUP_W), out(GROUP_W), out(GROUP_W, BF16), out(GROUP_W, BF16),
                 out(N_GDN_QKV), out(GZ_PAD), jax.ShapeDtypeStruct((T, B * GROUP_W), F32)]
    if apply_ln:
        out_specs, out_shape = [row(D)] + out_specs, [out(D)] + out_shape
    return pl.pallas_call(
        functools.partial(_inproj_kernel, apply_ln=apply_ln), grid=(B, T // tm),
        in_specs=[row(D)] + [_full_spec(c.shape) for c in consts]
        + [pl.BlockSpec((tm, 4 * LANES), lambda bi, i: (i, 0))] + [_full_spec(w.shape) for w in resident],
        out_specs=out_specs, out_shape=out_shape,
        compiler_params=_params(("parallel", "parallel")), name="inproj",
    )(x, *consts, tab, *resident)


def _mla_prep_rows(pm, tab, qn_ref, kvn_ref, wq_ref, wqr_ref, q_ref, lat_ref, kr_ref, krp_ref):
    c_kv = pm[:, :KV_LORA]
    c_q = pm[:, KV_LORA:KV_LORA + 2 * LANES]
    k_r = pm[:, 3 * LANES:4 * LANES]
    k_rr = pm[:, 4 * LANES:5 * LANES]
    lat_ref[...] = _rms_rows(c_kv, kvn_ref[...])
    nq = _rms_rows(c_q, qn_ref[...], n=Q_LORA).astype(BF16)
    qp = _dot(nq, wq_ref[...])
    qr = _dot(nq, wqr_ref[...])
    cq, sq = tab[:, :LANES], tab[:, LANES:2 * LANES]
    for h in range(N_HEADS):
        sl = slice(h * HEAD_PAD, (h + 1) * HEAD_PAD)
        q_ref[:, sl] = ((qp[:, sl] * cq + qr[:, sl] * sq) * (MLA_SCALE * LOG2E)).astype(BF16)
    kr_new = k_r * tab[:, 2 * LANES:3 * LANES] + k_rr * tab[:, 3 * LANES:]
    krp_ref[...] = kr_new
    kr_ref[...] = kr_new[:, :ROPE]


def _rope_table(past, T):
    half = ROPE // 2
    inv = ROPE_THETA ** (-jnp.arange(half, dtype=F32) / half)
    ang = (past + jnp.arange(T, dtype=jnp.int32)).astype(F32)[:, None] * inv
    c, s = jnp.cos(ang), jnp.sin(ang)
    one, zero = jnp.ones((T, NOPE), F32), jnp.zeros((T, NOPE), F32)
    z32, z96 = jnp.zeros((T, 32), F32), jnp.zeros((T, 96), F32)
    return jnp.concatenate([one, c, c, z32, zero, s, s, z32, c, c, z96, s, s, z96], 1)


def _prep_mla_weights(q_norm, kv_norm, w_uq, w_ukv):
    w3 = w_uq.reshape(Q_LORA, N_HEADS, NOPE + ROPE)
    zq = jnp.zeros((Q_LORA, N_HEADS, HEAD_PAD - NOPE - ROPE), F32)
    wq = jnp.concatenate([w3, zq], -1).reshape(Q_LORA, N_HEADS * HEAD_PAD)
    w3r = jnp.concatenate([jnp.zeros((Q_LORA, N_HEADS, NOPE), F32), _rotate_half_cols(w3[..., NOPE:]), zq], -1)
    wqr = w3r.reshape(Q_LORA, N_HEADS * HEAD_PAD)
    pad_rows = jnp.zeros((2 * LANES - Q_LORA, N_HEADS * HEAD_PAD), F32)
    wq = jnp.concatenate([wq, pad_rows], 0).astype(BF16)
    wqr = jnp.concatenate([wqr, pad_rows], 0).astype(BF16)
    qn = jnp.concatenate([q_norm, jnp.zeros((2 * LANES - Q_LORA,), F32)]).reshape(1, 2 * LANES)
    kvn = kv_norm.reshape(1, KV_LORA)
    kv3 = w_ukv.reshape(KV_LORA, N_HEADS, NOPE + HEAD_D)
    zk = jnp.zeros((KV_LORA, N_HEADS, HEAD_PAD - NOPE), F32)
    wk = jnp.concatenate([kv3[..., :NOPE], zk], -1).reshape(KV_LORA, N_HEADS * HEAD_PAD).astype(BF16)
    wv = kv3[..., NOPE:].reshape(KV_LORA, GROUP_W).astype(BF16)
    e = jnp.zeros((LANES, N_HEADS, HEAD_PAD), F32)
    e = e.at[jnp.arange(ROPE)[:, None], jnp.arange(N_HEADS)[None, :], NOPE + jnp.arange(ROPE)[:, None]].set(1.0)
    e = e.reshape(LANES, N_HEADS * HEAD_PAD).astype(BF16)
    return qn, kvn, wq, wqr, wk, wv, e


def _mla_kv_kernel(lat_ref, krp_ref, wk_ref, wv_ref, e_ref, k_ref, v_ref):
    lb = lat_ref[...].astype(BF16)
    kb = krp_ref[...].astype(BF16)
    k_ref[...] = (_dot(lb, wk_ref[...]) + _dot(kb, e_ref[...])).astype(BF16)
    v_ref[...] = _dot(lb, wv_ref[...]).astype(BF16)


def _mla_kv(lat, krp, wk, wv, e, tm):
    B, S, _ = lat.shape

    def row(f):
        return pl.BlockSpec((None, tm, f), lambda bi, i: (bi, i, 0))

    return pl.pallas_call(
        _mla_kv_kernel, grid=(B, S // tm),
        in_specs=[row(KV_LORA), row(LANES), _full_spec(wk.shape), _full_spec(wv.shape), _full_spec(e.shape)],
        out_specs=[row(N_HEADS * HEAD_PAD), row(GROUP_W)],
        out_shape=[jax.ShapeDtypeStruct((B, S, N_HEADS * HEAD_PAD), BF16),
                   jax.ShapeDtypeStruct((B, S, GROUP_W), BF16)],
        compiler_params=_params(("parallel", "parallel")), name="mla_kv",
    )(lat, krp, wk, wv, e)


def _lane_tile(a, n):
    return a if n == 1 else jnp.concatenate([a] * n, axis=1)


def _per_head_lanes(stats):
    low = lax.broadcasted_iota(jnp.int32, stats[0].shape, 1) < HEAD_D
    return jnp.concatenate([jnp.where(low, stats[0], stats[1]), jnp.where(low, stats[2], stats[3])], axis=1)


def _block_diag_values(vc):
    head = lax.broadcasted_iota(jnp.int32, vc.shape, 1) // HEAD_D
    return jnp.concatenate([jnp.where(head == h, vc, jnp.zeros_like(vc)) for h in range(N_HEADS)], axis=0)


def _mla_attn_kernel(qi_ref, kj_ref, q_ref, k_ref, v_ref, o_ref, m_scr, l_scr, acc_scr, *, tq, tk, rq, ck, past):
    qi = qi_ref[pl.program_id(1)]
    kj = kj_ref[pl.program_id(1)]
    delta = past + qi * tq - kj * tk

    @pl.when(kj == 0)
    def _():
        m_scr[...] = jnp.full(m_scr.shape, NEG_INF, F32)
        l_scr[...] = jnp.zeros(l_scr.shape, F32)
        acc_scr[...] = jnp.zeros(acc_scr.shape, F32)

    def tiles(masked):
        v_bd = [_block_diag_values(v_ref[c * ck:(c + 1) * ck, :]) for c in range(tk // ck)]
        for r in range(tq // rq):
            rows = slice(r * rq, (r + 1) * rq)
            m = [m_scr[h, rows, :] for h in range(N_HEADS)]
            l = [l_scr[h, rows, :] for h in range(N_HEADS)]
            acc = acc_scr[rows, :]
            for c in range(tk // ck):
                keys = slice(c * ck, (c + 1) * ck)
                if masked:
                    q_chunk = (past + qi * tq + r * rq + lax.broadcasted_iota(jnp.int32, (rq, ck), 0)) // CHUNK
                    k_chunk = (kj * tk + c * ck + lax.broadcasted_iota(jnp.int32, (rq, ck), 1)) // CHUNK
                    allowed = k_chunk <= q_chunk
                ps, alphas = [], []
                for h in range(N_HEADS):
                    sl = slice(h * HEAD_PAD, (h + 1) * HEAD_PAD)
                    s = _dot_nt(q_ref[rows, sl], k_ref[keys, sl])
                    if masked:
                        s = jnp.where(allowed, s, NEG_INF)
                    m_new = jnp.maximum(m[h], jnp.max(s, -1, keepdims=True))
                    alpha = jnp.exp2(m[h] - m_new)
                    p = jnp.exp2(s - _lane_tile(m_new, ck // LANES))
                    l[h] = alpha * l[h] + jnp.sum(p, -1, keepdims=True)
                    m[h] = m_new
                    ps.append(p.astype(BF16))
                    alphas.append(alpha)
                acc = _per_head_lanes(alphas) * acc + _dot(jnp.concatenate(ps, axis=1), v_bd[c])
            for h in range(N_HEADS):
                m_scr[h, rows, :] = m[h]
                l_scr[h, rows, :] = l[h]
            acc_scr[rows, :] = acc

    @pl.when(delta >= tk)
    def _():
        tiles(False)

    @pl.when(delta < tk)
    def _():
        tiles(True)

    @pl.when(kj == (past + (qi + 1) * tq - 1) // tk)
    def _():
        o_ref[...] = acc_scr[...] / _per_head_lanes([l_scr[h] for h in range(N_HEADS)])


def _block_pairs(nq, last_of, newest_first):
    qs, ks = [], []
    for i in range(nq):
        blocks = range(last_of(i), -1, -1) if newest_first else range(last_of(i) + 1)
        qs += [i] * len(blocks)
        ks += list(blocks)
    return jnp.asarray(qs, jnp.int32), jnp.asarray(ks, jnp.int32)


def _mla_attn(q, k, v, past, tq, tk):
    B, T, W = q.shape
    rq, ck = min(MLA_TILE[0], tq), min(MLA_TILE[1], tk)
    qi_tab, kj_tab = _block_pairs(T // tq, lambda i: (past + (i + 1) * tq - 1) // tk, False)

    def q_map(bi, p, qt, kt):
        return (bi, qt[p], 0)

    def kv_map(bi, p, qt, kt):
        return (bi, kt[p], 0)

    kern = functools.partial(_mla_attn_kernel, tq=tq, tk=tk, rq=rq, ck=ck, past=past)
    return pl.pallas_call(
        kern,
        grid_spec=pltpu.PrefetchScalarGridSpec(
            num_scalar_prefetch=2, grid=(B, qi_tab.shape[0]),
            in_specs=[pl.BlockSpec((None, tq, W), q_map),
                      pl.BlockSpec((None, tk, W), kv_map), pl.BlockSpec((None, tk, GROUP_W), kv_map)],
            out_specs=pl.BlockSpec((None, tq, GROUP_W), q_map),
            scratch_shapes=[pltpu.VMEM((N_HEADS, tq, LANES), F32), pltpu.VMEM((N_HEADS, tq, LANES), F32),
                            pltpu.VMEM((tq, GROUP_W), F32)]),
        out_shape=jax.ShapeDtypeStruct((B, T, GROUP_W), F32),
        compiler_params=_params(("parallel", "arbitrary")), name="mla_attn",
    )(qi_tab, kj_tab, q, k, v)


def _sb_attn_kernel(qi_ref, kj_ref, q_ref, k_ref, v_ref, u_ref, o_ref, c_scr, acc_scr, *, tq, tk, rq, ck, past):
    qi = qi_ref[pl.program_id(1)]
    jb = kj_ref[pl.program_id(1)]
    delta = past + qi * tq - jb * tk

    @pl.when(jb == (past + (qi + 1) * tq - 2) // tk)
    def _():
        c_scr[...] = jnp.zeros(c_scr.shape, F32)
        acc_scr[...] = jnp.zeros(acc_scr.shape, F32)

    def tiles(masked):
        u2 = u_ref[...]
        v_bd = [_block_diag_values(v_ref[c * ck:(c + 1) * ck, :]) for c in range(tk // ck)]
        head = lax.broadcasted_iota(jnp.int32, (rq, GROUP_W), 1) // HEAD_D
        for r in range(tq // rq):
            rows = slice(r * rq, (r + 1) * rq)
            q_r = q_ref[rows, :]
            q_h = [jnp.where(head == h, q_r, jnp.zeros_like(q_r)) for h in range(N_HEADS)]
            carry = [c_scr[h, rows, :] for h in range(N_HEADS)]
            acc = acc_scr[rows, :]
            for c in reversed(range(tk // ck)):
                k_c = k_ref[c * ck:(c + 1) * ck, :]
                if masked:
                    q_pos = delta + r * rq + lax.broadcasted_iota(jnp.int32, (rq, ck), 0)
                    allowed = (c * ck + lax.broadcasted_iota(jnp.int32, (rq, ck), 1)) < q_pos
                ws = []
                for h in range(N_HEADS):
                    z = _dot_nt(q_h[h], k_c)
                    sp = jnp.log(1.0 + jnp.exp2(-jnp.abs(z))) * LOG2E
                    log_beta = jnp.minimum(z, 0.0) - sp
                    log_1m = log_beta - z
                    if masked:
                        log_1m = jnp.where(allowed, log_1m, 0.0)
                    after = _dot(log_1m.astype(BF16), u2) + _lane_tile(carry[h], ck // LANES)
                    w = jnp.exp2(log_beta + after)
                    if masked:
                        w = jnp.where(allowed, w, 0.0)
                    ws.append(w.astype(BF16))
                    carry[h] = carry[h] + jnp.sum(log_1m, -1, keepdims=True)
                acc = acc + _dot(jnp.concatenate(ws, axis=1), v_bd[c])
            for h in range(N_HEADS):
                c_scr[h, rows, :] = carry[h]
            acc_scr[rows, :] = acc

    @pl.when(delta >= tk)
    def _():
        tiles(False)

    @pl.when(delta < tk)
    def _():
        tiles(True)

    @pl.when(jb == 0)
    def _():
        o_ref[...] = acc_scr[...]


def _sb_attn(q, k, v, past, tq, tk):
    B, T, W = q.shape
    rq, ck = min(SB_TILE[0], tq), min(SB_TILE[1], tk)
    later = lax.broadcasted_iota(jnp.int32, (ck, ck), 0) > lax.broadcasted_iota(jnp.int32, (ck, ck), 1)
    u2 = later.astype(BF16)
    qi_tab, kj_tab = _block_pairs(T // tq, lambda i: (past + (i + 1) * tq - 2) // tk, True)

    def q_map(bi, p, qt, kt):
        return (bi, qt[p], 0)

    def kv_map(bi, p, qt, kt):
        return (bi, kt[p], 0)

    kern = functools.partial(_sb_attn_kernel, tq=tq, tk=tk, rq=rq, ck=ck, past=past)
    return pl.pallas_call(
        kern,
        grid_spec=pltpu.PrefetchScalarGridSpec(
            num_scalar_prefetch=2, grid=(B, qi_tab.shape[0]),
            in_specs=[pl.BlockSpec((None, tq, W), q_map),
                      pl.BlockSpec((None, tk, W), kv_map), pl.BlockSpec((None, tk, W), kv_map),
                      pl.BlockSpec((ck, ck), lambda bi, p, qt, kt: (0, 0))],
            out_specs=pl.BlockSpec((None, tq, GROUP_W), q_map),
            scratch_shapes=[pltpu.VMEM((N_HEADS, tq, LANES), F32), pltpu.VMEM((tq, GROUP_W), F32)]),
        out_shape=jax.ShapeDtypeStruct((B, T, GROUP_W), F32),
        compiler_params=_params(("parallel", "arbitrary")), name="sb_attn",
    )(qi_tab, kj_tab, q, k, v, u2)


def _softplus(x):
    return jnp.maximum(x, 0.0) + jnp.log1p(jnp.exp(-jnp.abs(x)))


def _split_bf16(a):
    hi = a.astype(BF16)
    return hi, (a - hi.astype(F32)).astype(BF16)


def _dot_split(a, b):
    a_hi, a_lo = _split_bf16(a)
    b_hi, b_lo = _split_bf16(b)
    m = a.shape[0]
    top = _dot(jnp.concatenate([a_hi, a_lo], 0), b_hi)
    return top[:m] + top[m:] + _dot(a_hi, b_lo)


def _head_sumsq(x, hsum):
    hi, lo = _split_bf16(x * x)
    m = x.shape[0]
    both = _dot(jnp.concatenate([hi, lo], 0), hsum)
    return both[:m] + both[m:]


def _gdn_kernel(qkv_ref, gz_ref, conv0_ref, s0_ref, cw_ref, alog_ref, dtb_ref, ng_ref, hsum_ref,
                o_ref, s_out_ref, xp_scr, s_scr, o_scr, *, L, nt, bb):
    ti = pl.program_id(1)
    halo = CONV_W - 1

    @pl.when(ti == 0)
    def _():
        xp_scr[:, 8 - halo:8, :] = conv0_ref[...]
        s_scr[...] = s0_ref[...]

    hsum = hsum_ref[...]
    ii = lax.broadcasted_iota(jnp.int32, (L, L), 0)
    jj = lax.broadcasted_iota(jnp.int32, (L, L), 1)
    incl = ii >= jj
    strict = ii > jj
    eye = (ii == jj).astype(F32)

    units = []
    for b in range(bb):
        x = qkv_ref[b]
        xp_scr[b, 8:8 + L, :] = x
        conv = jnp.zeros((L, N_GDN_QKV), F32)
        for i in range(CONV_W):
            conv = conv + xp_scr[b, 8 - halo + i:8 - halo + i + L, :] * cw_ref[i:i + 1, :]
        xp_scr[b, 8 - halo:8, :] = x[L - halo:, :]
        conv = jax.nn.silu(conv)
        qa = conv[:, :GROUP_W]
        ka = conv[:, GROUP_W:2 * GROUP_W]
        va = conv[:, 2 * GROUP_W:]
        qa = qa * lax.rsqrt(_head_sumsq(qa, hsum) + 1e-6) * (HEAD_D ** -0.5)
        ka = ka * lax.rsqrt(_head_sumsq(ka, hsum) + 1e-6)
        ab = gz_ref[b, :, GROUP_W:]
        g_all = -jnp.exp(alog_ref[...]) * _softplus(ab + dtb_ref[...])
        beta_all = jax.nn.sigmoid(ab)
        for h in range(N_HEADS):
            hs = slice(h * HEAD_D, (h + 1) * HEAD_D)
            units.append(dict(b=b, h=h, q=qa[:, hs], k=ka[:, hs], v=va[:, hs], g=g_all[:, h:h + 1],
                              beta=beta_all[:, N_HEADS + h:N_HEADS + h + 1]))

    for un in units:
        g_b = jnp.broadcast_to(un['g'], (L, L))
        gc_row = jnp.sum(jnp.where(ii <= jj, g_b, 0.0), 0, keepdims=True)
        gc_col = jnp.sum(jnp.where(ii == jj, jnp.broadcast_to(gc_row, (L, L)), 0.0), 1, keepdims=True)
        un['decay'] = jnp.where(incl, jnp.exp(jnp.where(incl, gc_col - gc_row, 0.0)), 0.0)
        un['e_col'] = jnp.exp(gc_col)
        gc_last = gc_row[:, L - 1:L]
        un['g_last'] = jnp.exp(gc_last)
        un['k_tail'] = jnp.exp(gc_last - gc_col)
        un['kb'] = un['k'] * un['beta']
        un['kbf'] = un['k'].astype(BF16)
    for un in units:
        m = jnp.where(strict, _dot_nt(un['kb'].astype(BF16), un['kbf']) * un['decay'], 0.0)
        un['p'] = -m
        un['t'] = eye - m

    n_sq = int(math.log2(L)) - 1
    for r in range(n_sq + 1):
        for un in units:
            p_hi, p_lo = _split_bf16(un['p'])
            if r == 0:
                lhs_hi, lhs_lo = p_hi, p_lo
            else:
                t_hi, t_lo = _split_bf16(un['t'])
                lhs_hi = jnp.concatenate([p_hi, t_hi], 0) if r < n_sq else t_hi
                lhs_lo = jnp.concatenate([p_lo, t_lo], 0) if r < n_sq else t_lo
            rows = lhs_hi.shape[0]
            top = _dot(jnp.concatenate([lhs_hi, lhs_lo], 0), p_hi)
            prod = top[:rows] + top[rows:] + _dot(lhs_hi, p_lo)
            if r == 0:
                un['p'] = prod
            elif r < n_sq:
                un['p'] = prod[:L]
                un['t'] = un['t'] + prod[L:]
            else:
                un['t'] = un['t'] + prod
    for un in units:
        un['u'] = _dot_split(un['t'], un['v'] * un['beta'])
        un['w'] = _dot_split(un['t'], un['kb'] * un['e_col'])
        un['attn'] = jnp.where(incl, _dot_nt(un['q'].astype(BF16), un['kbf']) * un['decay'], 0.0).astype(BF16)

    for un in units:
        un['S'] = s_scr[un['b'], un['h']]
        un['Sb'] = un['S'].astype(BF16)
    for un in units:
        un['v_new'] = (un['u'] - _dot(un['w'].astype(BF16), un['Sb'])).astype(BF16)
    for un in units:
        hs = slice(un['h'] * HEAD_D, (un['h'] + 1) * HEAD_D)
        o_h = _dot((un['q'] * un['e_col']).astype(BF16), un['Sb']) + _dot(un['attn'], un['v_new'])
        o_scr[un['b'], :, hs] = o_h
        k_dec = (un['k'] * un['k_tail']).astype(BF16)
        s_scr[un['b'], un['h']] = un['S'] * un['g_last'] + _dot_tn(k_dec, un['v_new'])

    for b in range(bb):
        o = o_scr[b]
        ms = _head_sumsq(o, hsum) * (1.0 / HEAD_D)
        o_ref[b] = o * lax.rsqrt(ms + RMS_EPS) * ng_ref[...] * jax.nn.silu(gz_ref[b, :, :GROUP_W])

    @pl.when(ti == nt - 1)
    def _():
        s_out_ref[...] = s_scr[...]


def _gdn(qkv, gz, conv0, s0, conv_w, a_log, dt_bias, norm_g):
    B, T, _ = qkv.shape
    L = min(CHUNK, T)
    nt = T // L
    pad = lambda a: jnp.concatenate([a, jnp.zeros((LANES - a.shape[0],), F32)]).reshape(1, LANES)
    alog = pad(a_log)
    dtb = pad(dt_bias)
    ng = jnp.tile(norm_g, N_HEADS).reshape(1, GROUP_W)
    hid = jnp.arange(GROUP_W) // HEAD_D
    hsum = (hid[:, None] == hid[None, :]).astype(BF16)
    bb = GDN_BATCH_PER_STEP

    def row(f):
        return pl.BlockSpec((bb, L, f), lambda bi, i: (bi, i, 0))

    state = pl.BlockSpec((bb, N_HEADS, HEAD_D, HEAD_D), lambda bi, i: (bi, 0, 0, 0))
    kern = functools.partial(_gdn_kernel, L=L, nt=nt, bb=bb)
    return pl.pallas_call(
        kern, grid=(B // bb, nt),
        in_specs=[row(N_GDN_QKV), row(GZ_PAD),
                  pl.BlockSpec((bb, CONV_W - 1, N_GDN_QKV), lambda bi, i: (bi, 0, 0)), state,
                  _full_spec(conv_w.shape), _full_spec(alog.shape), _full_spec(dtb.shape),
                  _full_spec(ng.shape), _full_spec(hsum.shape)],
        out_specs=[row(GROUP_W), state],
        out_shape=[jax.ShapeDtypeStruct((B, T, GROUP_W), F32),
                   jax.ShapeDtypeStruct((B, N_HEADS, HEAD_D, HEAD_D), F32)],
        scratch_shapes=[pltpu.VMEM((bb, 8 + L, N_GDN_QKV), F32), pltpu.VMEM((bb, N_HEADS, HEAD_D, HEAD_D), F32),
                        pltpu.VMEM((bb, L, GROUP_W), F32)],
        compiler_params=_params(("parallel", "arbitrary")), name="gdn",
    )(qkv, gz, conv0, s0, conv_w, alog, dtb, ng, hsum)


def _s5_kernel(u_ref, h0r_ref, h0i_ref, lr_ref, li_ref, br_ref, bi_ref, cr_ref, ci_ref, d_ref, wg_ref,
               o_ref, hr_out_ref, hi_out_ref, hr_scr, hi_scr, xr_scr, xi_scr, *, tt, nb, nt):
    ti = pl.program_id(0)

    @pl.when(ti == 0)
    def _():
        hr_scr[...] = h0r_ref[...]
        hi_scr[...] = h0i_ref[...]

    u = u_ref[...]
    ub = u.astype(BF16)
    xr_scr[...] = _dot(ub, br_ref[...])
    xi_scr[...] = _dot(ub, bi_ref[...])
    lr = jnp.broadcast_to(lr_ref[...], (nb, S5_STATE))
    li = jnp.broadcast_to(li_ref[...], (nb, S5_STATE))

    def step(t, carry):
        hr, hi = carry
        r0 = pl.multiple_of(t * nb, nb)
        nr = lr * hr - li * hi + xr_scr[pl.ds(r0, nb), :]
        ni = lr * hi + li * hr + xi_scr[pl.ds(r0, nb), :]
        xr_scr[pl.ds(r0, nb), :] = nr
        xi_scr[pl.ds(r0, nb), :] = ni
        return nr, ni

    hr, hi = lax.fori_loop(0, tt, step, (hr_scr[...], hi_scr[...]), unroll=8)
    hr_scr[...] = hr
    hi_scr[...] = hi
    y = _dot(xr_scr[...].astype(BF16), cr_ref[...]) - _dot(xi_scr[...].astype(BF16), ci_ref[...]) + d_ref[...] * u
    act = jax.nn.gelu(y)
    o_ref[...] = act * jax.nn.sigmoid(_dot(act.astype(BF16), wg_ref[...]))

    @pl.when(ti == nt - 1)
    def _():
        hr_out_ref[...] = hr
        hi_out_ref[...] = hi


def _prep_s5_weights(lam_re, lam_im, log_step, b_re, b_im, c_re, c_im, d, w_glu):
    dt = jnp.exp(log_step)[:, None]
    mag = jnp.exp(lam_re * dt)
    bar_re, bar_im = mag * jnp.cos(lam_im * dt), mag * jnp.sin(lam_im * dt)
    den = lam_re * lam_re + lam_im * lam_im
    f_re = ((bar_re - 1.0) * lam_re + bar_im * lam_im) / den
    f_im = (bar_im * lam_re - (bar_re - 1.0) * lam_im) / den
    bb_re = f_re[..., None] * b_re - f_im[..., None] * b_im
    bb_im = f_re[..., None] * b_im + f_im[..., None] * b_re
    eye = jnp.eye(S5_G, dtype=F32)

    def in_mat(b):
        return jnp.einsum('gpc,gh->gchp', b, eye).reshape(S5_G * S5_CG, S5_STATE).astype(BF16)

    def out_mat(c):
        return jnp.einsum('gcp,gh->gphc', c, eye).reshape(S5_STATE, S5_G * S5_CG).astype(BF16)

    return (bar_re.reshape(1, S5_STATE), bar_im.reshape(1, S5_STATE),
            in_mat(bb_re), in_mat(bb_im), out_mat(c_re), out_mat(c_im),
            d.reshape(1, GROUP_W), w_glu.astype(BF16))


def _s5(u_tm, h0_re, h0_im, weights, B, tt):
    T = u_tm.shape[0]
    u2 = u_tm.reshape(T * B, GROUP_W)
    nt = T // tt
    lr, li, br, bi, cr, ci, d, wg = weights
    rows = pl.BlockSpec((tt * B, GROUP_W), lambda i: (i, 0))
    st = _full_spec((B, S5_STATE))
    kern = functools.partial(_s5_kernel, tt=tt, nb=B, nt=nt)
    o, hr, hi = pl.pallas_call(
        kern, grid=(nt,),
        in_specs=[rows, st, st] + [_full_spec(w.shape) for w in weights],
        out_specs=[rows, st, st],
        out_shape=[jax.ShapeDtypeStruct((T * B, GROUP_W), F32), jax.ShapeDtypeStruct((B, S5_STATE), F32),
                   jax.ShapeDtypeStruct((B, S5_STATE), F32)],
        scratch_shapes=[pltpu.VMEM((B, S5_STATE), F32), pltpu.VMEM((B, S5_STATE), F32),
                        pltpu.VMEM((tt * B, S5_STATE), F32), pltpu.VMEM((tt * B, S5_STATE), F32)],
        compiler_params=_params(("arbitrary",)), name="s5",
    )(u2, h0_re.reshape(B, S5_STATE), h0_im.reshape(B, S5_STATE), *weights)
    return o.reshape(T, B * GROUP_W), hr.reshape(B, S5_G, S5_P), hi.reshape(B, S5_G, S5_P)


def _outproj_kernel(oa_ref, ob_ref, oc_ref, od_ref, x_ref, gn_ref, w_ref, g_ref, b_ref, o_ref, *, alpha):
    gn = gn_ref[...]
    mix = jnp.concatenate([
        _rms_rows(oa_ref[...], gn[0:1]).astype(BF16),
        _rms_rows(ob_ref[...], gn[1:2]).astype(BF16),
        oc_ref[...].astype(BF16),
        _rms_rows(od_ref[...], gn[2:3]).astype(BF16)], -1)
    y = alpha * x_ref[...] + _dot(mix, w_ref[...])
    o_ref[...] = _ln_rows(y, g_ref[...], b_ref[...])


def _outproj(oa, ob, oc, od_tm, x, gn, w_out, g, b, alpha, tm):
    B, T, D = x.shape

    def row(f):
        return pl.BlockSpec((None, tm, f), lambda bi, i: (bi, i, 0))

    kern = functools.partial(_outproj_kernel, alpha=alpha)
    return pl.pallas_call(
        kern, grid=(B, T // tm),
        in_specs=[row(GROUP_W), row(GROUP_W), row(GROUP_W), pl.BlockSpec((tm, GROUP_W), lambda bi, i: (i, bi)),
                  row(D), _full_spec(gn.shape), _full_spec(w_out.shape), _full_spec((1, D)), _full_spec((1, D))],
        out_specs=row(D), out_shape=jax.ShapeDtypeStruct((B, T, D), F32),
        compiler_params=_params(("parallel", "parallel")), name="outproj",
    )(oa, ob, oc, od_tm, x, gn, w_out, g.reshape(1, D), b.reshape(1, D))


def _ffn_kernel(x_ref, wg_ref, wu_ref, wd_ref, g_ref, b_ref, o_ref, *, alpha):
    x = x_ref[...]
    xb = x.astype(BF16)
    hid = (jax.nn.silu(_dot(xb, wg_ref[...])) * _dot(xb, wu_ref[...])).astype(BF16)
    y = alpha * x + _dot(hid, wd_ref[...])
    o_ref[...] = _ln_rows(y, g_ref[...], b_ref[...])


def _ffn(x, wg, wu, wd, g, b, alpha, tm):
    B, T, D = x.shape
    row = pl.BlockSpec((None, tm, D), lambda bi, i: (bi, i, 0))

    def resident(shape):
        return pl.BlockSpec(shape, lambda *_: (0,) * len(shape), pipeline_mode=pl.Buffered(1))

    kern = functools.partial(_ffn_kernel, alpha=alpha)
    return pl.pallas_call(
        kern, grid=(B, T // tm),
        in_specs=[row, resident(wg.shape), resident(wu.shape), resident(wd.shape),
                  _full_spec((1, D)), _full_spec((1, D))],
        out_specs=row, out_shape=jax.ShapeDtypeStruct((B, T, D), F32),
        compiler_params=_params(("parallel", "parallel")), name="ffn",
    )(x, wg, wu, wd, g.reshape(1, D), b.reshape(1, D))


def _router_kernel(x_ref, wr_ref, br_ref, idx_ref, gate_ref):
    logits = _dot_nt(wr_ref[...], x_ref[...], HIGHEST) + br_ref[...]
    eid = lax.broadcasted_iota(jnp.int32, logits.shape, 0)
    m1 = jnp.max(logits, 0, keepdims=True)
    i1 = jnp.min(jnp.where(logits == m1, eid, N_EXP), 0, keepdims=True)
    rest = jnp.where(eid == i1, -jnp.inf, logits)
    m2 = jnp.max(rest, 0, keepdims=True)
    i2 = jnp.min(jnp.where(rest == m2, eid, N_EXP), 0, keepdims=True)
    e2 = jnp.exp(m2 - m1)
    den = 1.0 + e2
    idx_ref[...] = jnp.concatenate([i1, i2], 0)
    gate_ref[...] = jnp.concatenate([1.0 / den, e2 / den], 0)


def _router(x2, w_router, b_router, tm):
    N, D = x2.shape
    return pl.pallas_call(
        _router_kernel, grid=(N // tm,),
        in_specs=[pl.BlockSpec((tm, D), lambda i: (i, 0)), _full_spec((N_EXP, D)), _full_spec((N_EXP, 1))],
        out_specs=[pl.BlockSpec((TOP_K, tm), lambda i: (0, i)), pl.BlockSpec((TOP_K, tm), lambda i: (0, i))],
        out_shape=[jax.ShapeDtypeStruct((TOP_K, N), jnp.int32), jax.ShapeDtypeStruct((TOP_K, N), F32)],
        compiler_params=_params(("parallel",)), name="moe_router",
    )(x2, w_router.T, b_router.reshape(N_EXP, 1))


def _expert_kernel(blk_exp_ref, src_ref, src_next_ref, dst_ref, x_hbm, wg_ref, wu_ref, wd_ref, out_hbm,
                   xbuf, ybuf, gsem, ssem, *, n_blk):
    del blk_exp_ref
    i = pl.program_id(0)
    n_sub = MOE_BLK // MOE_SUB

    def rows(j):
        return pl.ds(j * MOE_SUB, MOE_SUB)

    def gather(j, idx_ref):
        for r in range(j * MOE_SUB, (j + 1) * MOE_SUB):
            pltpu.make_async_copy(x_hbm.at[pl.ds(idx_ref[0, 0, r], 1)], xbuf.at[pl.ds(r, 1)], gsem.at[j]).start()

    def scatter(j):
        for r in range(j * MOE_SUB, (j + 1) * MOE_SUB):
            pltpu.make_async_copy(ybuf.at[pl.ds(r, 1)], out_hbm.at[pl.ds(dst_ref[0, 0, r], 1)], ssem.at[j]).start()

    def wait_gather(j):
        pltpu.make_async_copy(x_hbm.at[pl.ds(0, MOE_SUB)], xbuf.at[rows(j)], gsem.at[j]).wait()

    def wait_scatter(j):
        pltpu.make_async_copy(ybuf.at[rows(j)], out_hbm.at[pl.ds(0, MOE_SUB)], ssem.at[j]).wait()

    @pl.when(i == 0)
    def _():
        gather(0, src_ref)

    for j in range(n_sub):
        wait_gather(j)
        if j == n_sub - 1:
            @pl.when(i > 0)
            def _():
                wait_scatter(n_sub - 1)
        if j + 1 < n_sub:
            gather(j + 1, src_ref)
        if j == 1:
            gather(0, src_next_ref)
        if j >= 1:
            scatter(j - 1)
        xb = xbuf[rows(j), :].astype(BF16)
        hid = (jax.nn.silu(_dot(xb, wg_ref[...])) * _dot(xb, wu_ref[...])).astype(BF16)
        ybuf[rows(j), :] = _dot(hid, wd_ref[...])
    scatter(n_sub - 1)
    for j in range(n_sub - 1):
        wait_scatter(j)

    @pl.when(i == n_blk - 1)
    def _():
        wait_scatter(n_sub - 1)
        wait_gather(0)


def _combine_kernel(x_ref, y0_ref, y1_ref, gate_ref, g_ref, b_ref, o_ref, *, alpha):
    gate = gate_ref[...]
    f = y0_ref[...] * gate[:, 0:1] + y1_ref[...] * gate[:, 1:2]
    o_ref[...] = _ln_rows(alpha * x_ref[...] + f, g_ref[...], b_ref[...])


def _moe(x, w_router, b_router, wg, wu, wd, g, b, alpha):
    B, T, D = x.shape
    N = B * T
    x2 = x.reshape(N, D)
    tm = math.gcd(N, MOE_BLK)
    idx, gates = _router(x2, w_router, b_router, tm)

    n_slots = N * TOP_K
    e_flat = idx.T.reshape(-1)
    onehot = (e_flat[:, None] == jnp.arange(N_EXP, dtype=jnp.int32)[None, :]).astype(jnp.int32)
    csum = jnp.cumsum(onehot, 0)
    counts = csum[-1]
    rank = jnp.sum((csum - onehot) * onehot, 1)
    padded = (counts + MOE_BLK - 1) // MOE_BLK * MOE_BLK
    pad_end = jnp.cumsum(padded)
    pad_start = pad_end - padded
    dest = pad_start[e_flat] + rank
    n_blk = -(-(n_slots + N_EXP * (MOE_BLK - 1)) // MOE_BLK)
    rows = n_blk * MOE_BLK
    row_f = jnp.full((rows,), n_slots, jnp.int32).at[dest].set(jnp.arange(n_slots, dtype=jnp.int32))
    row_src = jnp.minimum(row_f // TOP_K, N - 1)
    row_dst = jnp.where(row_f < n_slots, (row_f % TOP_K) * N + row_f // TOP_K,
                        n_slots + jnp.arange(rows, dtype=jnp.int32) % MOE_BLK)
    blk_start = jnp.arange(n_blk, dtype=jnp.int32) * MOE_BLK
    blk_exp = jnp.minimum(jnp.sum((pad_end[None, :] <= blk_start[:, None]).astype(jnp.int32), 1), N_EXP - 1)

    idx_spec = pl.BlockSpec((1, 1, MOE_BLK), lambda i, be: (i, 0, 0), memory_space=pltpu.SMEM)
    next_spec = pl.BlockSpec((1, 1, MOE_BLK), lambda i, be: (jnp.minimum(i + 1, n_blk - 1), 0, 0),
                             memory_space=pltpu.SMEM)
    y = pl.pallas_call(
        functools.partial(_expert_kernel, n_blk=n_blk),
        grid_spec=pltpu.PrefetchScalarGridSpec(
            num_scalar_prefetch=1, grid=(n_blk,),
            in_specs=[idx_spec, next_spec, idx_spec,
                      pl.BlockSpec(memory_space=pl.ANY),
                      pl.BlockSpec((None, D, D_FF_EXP), lambda i, be: (be[i], 0, 0)),
                      pl.BlockSpec((None, D, D_FF_EXP), lambda i, be: (be[i], 0, 0)),
                      pl.BlockSpec((None, D_FF_EXP, D), lambda i, be: (be[i], 0, 0))],
            out_specs=pl.BlockSpec(memory_space=pl.ANY),
            scratch_shapes=[pltpu.VMEM((MOE_BLK, D), F32), pltpu.VMEM((MOE_BLK, D), F32),
                            pltpu.SemaphoreType.DMA((MOE_BLK // MOE_SUB,)),
                            pltpu.SemaphoreType.DMA((MOE_BLK // MOE_SUB,))]),
        out_shape=jax.ShapeDtypeStruct((n_slots + MOE_BLK, D), F32),
        compiler_params=_params(("arbitrary",)), name="moe_experts",
    )(blk_exp, row_src.reshape(n_blk, 1, MOE_BLK), row_src.reshape(n_blk, 1, MOE_BLK),
      row_dst.reshape(n_blk, 1, MOE_BLK), x2, wg, wu, wd)

    kern = functools.partial(_combine_kernel, alpha=alpha)
    rows_spec = pl.BlockSpec((tm, D), lambda i: (i, 0))
    out = pl.pallas_call(
        kern, grid=(N // tm,),
        in_specs=[rows_spec, rows_spec, pl.BlockSpec((tm, D), lambda i: (i + N // tm, 0)),
                  pl.BlockSpec((tm, TOP_K), lambda i: (i, 0)), _full_spec((1, D)), _full_spec((1, D))],
        out_specs=rows_spec, out_shape=jax.ShapeDtypeStruct((N, D), F32),
        compiler_params=_params(("parallel",)), name="moe_combine",
    )(x2, y, y, gates.T, g.reshape(1, D), b.reshape(1, D))
    return out.reshape(B, T, D)


def _round_up(n, m):
    return -(-n // m) * m


def _run_trunk(x, states, P, depth):
    lat_c, kr_c, sbk_c, sbv_c, conv_c, ssm_c, re_c, im_c = states
    B, T, D = x.shape
    past = lat_c.shape[2]
    alpha = (2 * depth) ** 0.25
    long_seq = T >= 512
    tm = 512 if long_seq else T
    tq = ATTN_TQ if long_seq else T
    tk = ATTN_TK if long_seq else 2 * LANES
    S = past + T
    S_pad = _round_up(S, tk)

    tab = _rope_table(past, T)
    outs = [[] for _ in range(8)]
    for l in range(depth):
        qn, kvn, wq, wqr, wk, wv, e = _prep_mla_weights(P['mla_q_norm'][l], P['mla_kv_norm'][l],
                                                        P['mla_w_uq'][l], P['mla_w_ukv'][l])
        proj = _inproj(x, P['ln_in_g'], P['ln_in_b'], l == 0, tab, (qn, kvn, wq, wqr),
                       _prep_inproj_weights(P['w_in'][l]), tm)
        if l == 0:
            x, *proj = proj
        q_a, lat_n, kr_n, krp_n, sbq, sbk, sbv, sbk_b, sbv_b, gq, gz, s5u = proj

        if past or S_pad != S:
            kr_cache = jnp.pad(kr_c[l], ((0, 0), (0, 0), (0, LANES - ROPE)))
            lat_all = jnp.pad(jnp.concatenate([lat_c[l], lat_n], 1), ((0, 0), (0, S_pad - S), (0, 0)))
            krp_all = jnp.pad(jnp.concatenate([kr_cache, krp_n], 1), ((0, 0), (0, S_pad - S), (0, 0)))
        else:
            lat_all, krp_all = lat_n, krp_n
        k_a, v_a = _mla_kv(lat_all, krp_all, wk, wv, e, tk)
        o_a = _mla_attn(q_a, k_a, v_a, past, tq, tk)

        if past or S_pad != S:
            k_all = jnp.pad(jnp.concatenate([sbk_c[l].reshape(B, past, GROUP_W).astype(BF16), sbk_b], 1),
                            ((0, 0), (0, S_pad - S), (0, 0)))
            v_all = jnp.pad(jnp.concatenate([sbv_c[l].reshape(B, past, GROUP_W).astype(BF16), sbv_b], 1),
                            ((0, 0), (0, S_pad - S), (0, 0)))
        else:
            k_all, v_all = sbk_b, sbv_b
        o_b = _sb_attn(sbq, k_all, v_all, past, tq, tk)

        o_c, ssm_n = _gdn(gq, gz, conv_c[l], ssm_c[l], P['gdn_conv_w'][l], P['gdn_a_log'][l],
                          P['gdn_dt_bias'][l], P['gdn_norm'][l])
        conv_n = jnp.concatenate([conv_c[l], gq], 1)[:, T:]

        s5w = _prep_s5_weights(P['s5_lam_re'][l], P['s5_lam_im'][l], P['s5_log_step'][l], P['s5_b_re'][l],
                               P['s5_b_im'][l], P['s5_c_re'][l], P['s5_c_im'][l], P['s5_d'][l], P['s5_w_glu'][l])
        o_d, re_n, im_n = _s5(s5u, re_c[l], im_c[l], s5w, B, min(64, T))

        x = _outproj(o_a, o_b, o_c, o_d, x, P['grp_norm'][l], P['w_out'][l].astype(BF16),
                     P['ln1_g'][l], P['ln1_b'][l], alpha, tm)
        i = l // 2
        if l % 2 == 0:
            x = _ffn(x, P['ffn_w_gate'][i].astype(BF16), P['ffn_w_up'][i].astype(BF16),
                     P['ffn_w_down'][i].astype(BF16), P['ln2_g'][l], P['ln2_b'][l], alpha, min(tm, 256))
        else:
            x = _moe(x, P['moe_w_router'][i], P['moe_b_router'][i], P['moe_w_gate'][i].astype(BF16),
                     P['moe_w_up'][i].astype(BF16), P['moe_w_down'][i].astype(BF16),
                     P['ln2_g'][l], P['ln2_b'][l], alpha)
        news = (lat_n, kr_n, sbk.reshape(B, T, N_HEADS, HEAD_D), sbv.reshape(B, T, N_HEADS, HEAD_D),
                conv_n, ssm_n, re_n, im_n)
        for lst, a in zip(outs, news):
            lst.append(a)
    return x, [jnp.stack(lst) for lst in outs]


def kernel(x_prompt, x_sample, cache_mla_latent, cache_mla_krope, cache_sb_k, cache_sb_v,
           state_gdn_conv, state_gdn_ssm, state_s5_re, state_s5_im,
           ln_in_g, ln_in_b, w_in, mla_q_norm, mla_kv_norm, mla_w_uq, mla_w_ukv,
           gdn_conv_w, gdn_a_log, gdn_dt_bias, gdn_norm,
           s5_lam_re, s5_lam_im, s5_log_step, s5_b_re, s5_b_im, s5_c_re, s5_c_im, s5_d, s5_w_glu,
           grp_norm, w_out, ln1_g, ln1_b, ln2_g, ln2_b,
           ffn_w_gate, ffn_w_up, ffn_w_down,
           moe_w_router, moe_b_router, moe_w_gate, moe_w_up, moe_w_down):
    P = dict(ln_in_g=ln_in_g, ln_in_b=ln_in_b, w_in=w_in, mla_q_norm=mla_q_norm, mla_kv_norm=mla_kv_norm,
             mla_w_uq=mla_w_uq, mla_w_ukv=mla_w_ukv, gdn_conv_w=gdn_conv_w, gdn_a_log=gdn_a_log,
             gdn_dt_bias=gdn_dt_bias, gdn_norm=gdn_norm, s5_lam_re=s5_lam_re, s5_lam_im=s5_lam_im,
             s5_log_step=s5_log_step, s5_b_re=s5_b_re, s5_b_im=s5_b_im, s5_c_re=s5_c_re, s5_c_im=s5_c_im,
             s5_d=s5_d, s5_w_glu=s5_w_glu, grp_norm=grp_norm, w_out=w_out, ln1_g=ln1_g, ln1_b=ln1_b,
             ln2_g=ln2_g, ln2_b=ln2_b, ffn_w_gate=ffn_w_gate, ffn_w_up=ffn_w_up, ffn_w_down=ffn_w_down,
             moe_w_router=moe_w_router, moe_b_router=moe_b_router, moe_w_gate=moe_w_gate,
             moe_w_up=moe_w_up, moe_w_down=moe_w_down)
    depth = w_in.shape[0]
    Bp = x_prompt.shape[0]
    dt = x_prompt.dtype
    prompt_states = (jnp.zeros((depth, Bp, 0, KV_LORA), dt), jnp.zeros((depth, Bp, 0, ROPE), dt),
                     jnp.zeros((depth, Bp, 0, N_HEADS, HEAD_D), dt), jnp.zeros((depth, Bp, 0, N_HEADS, HEAD_D), dt),
                     jnp.zeros((depth, Bp, CONV_W - 1, N_GDN_QKV), dt),
                     jnp.zeros((depth, Bp, N_HEADS, HEAD_D, HEAD_D), F32),
                     jnp.zeros((depth, Bp, S5_G, S5_P), F32), jnp.zeros((depth, Bp, S5_G, S5_P), F32))
    y_prompt, st_p = _run_trunk(x_prompt, prompt_states, P, depth)
    sample_states = (cache_mla_latent, cache_mla_krope, cache_sb_k, cache_sb_v,
                     state_gdn_conv, state_gdn_ssm, state_s5_re, state_s5_im)
    y_sample, st_s = _run_trunk(x_sample, sample_states, P, depth)
    return (y_prompt, y_sample, *st_p, *st_s)
```
